```python
import math
import jax, jax.numpy as jnp
from jax import lax
import numpy as np

D_MODEL = 1024
BATCH = 8
SEQ = 2048
DEPTH = 2
DEC_BATCH = 128
DEC_SEQ = 1
PAST_LEN = 2048
PAGE_SIZE = 128

HEAD_DIM = 64
D_MIX = D_MODEL
GROUP_W = D_MIX // 4
N_HEADS = GROUP_W // HEAD_DIM
Q_BLOCK = 128
SB_BIAS_INIT = -8.0
CHUNK = 64
CONV_W = 4
N_GROUPS = 4
EXPERTS_PER_GROUP = 8
N_EXPERTS = N_GROUPS * EXPERTS_PER_GROUP
TOP_K = 2
D_EXPERT = D_MODEL // 2
ROPE_BASE = 10000.0
EPS = 1e-6

SB_COLS = 3 * GROUP_W
ML_COLS = 4 * GROUP_W + 2 * N_HEADS
GD_COLS = 4 * GROUP_W + 2 * N_HEADS
RT_COLS = 4 * GROUP_W
N_IN = SB_COLS + ML_COLS + GD_COLS + RT_COLS

kernel_name = "hybrid_sb_mlstm_gdn_retention_hmoe_step"


def _rmsnorm(x, w):
    xf = x.astype(jnp.float32)
    y = xf * lax.rsqrt(jnp.mean(xf * xf, axis=-1, keepdims=True) + EPS)
    return (y * w.astype(jnp.float32)).astype(x.dtype)


def _head_rmsnorm(x):
    return x * lax.rsqrt(jnp.mean(x * x, axis=-1, keepdims=True) + EPS)


def _l2norm(x):
    return x * lax.rsqrt(jnp.sum(x * x, axis=-1, keepdims=True) + EPS)


def _chunk_len(t, c):
    return c if t % c == 0 else t


def _to_chunks(a, L):
    b, t = a.shape[:2]
    a = a.reshape((b, t // L, L) + a.shape[2:])
    return jnp.swapaxes(jnp.moveaxis(a, 1, 0), 2, 3)


def _from_chunks(a):
    a = jnp.moveaxis(jnp.swapaxes(a, 2, 3), 0, 1)
    return a.reshape((a.shape[0], a.shape[1] * a.shape[2]) + a.shape[3:])


def _rope(x, pos):
    half = HEAD_DIM // 2
    inv = ROPE_BASE ** (-jnp.arange(half, dtype=jnp.float32) / half)
    ang = pos.astype(jnp.float32)[:, None] * inv[None, :]
    cos = jnp.cos(ang)[None, :, None, :]
    sin = jnp.sin(ang)[None, :, None, :]
    x1, x2 = x[..., :half], x[..., half:]
    return jnp.concatenate([x1 * cos - x2 * sin, x1 * sin + x2 * cos], axis=-1)


def _causal_conv(x_ext, w):
    c = x_ext.shape[-1]
    return lax.conv_general_dilated(x_ext, w[:, None, :].astype(x_ext.dtype), window_strides=(1,), padding='VALID',
                                    dimension_numbers=('NWC', 'WIO', 'NWC'), feature_group_count=c)


def _sb_block(q, k, v, q_pos, k_pos, bias):
    z = jnp.einsum('bqhd,bkhd->bhqk', q, k) * (HEAD_DIM ** -0.5) + bias[None, :, None, None]
    mask = k_pos[None, :] < q_pos[:, None]
    log_keep = jnp.where(mask, jax.nn.log_sigmoid(-z), 0.0)
    later = lax.cumsum(log_keep, axis=3, reverse=True) - log_keep
    a = jnp.where(mask, jnp.exp(jax.nn.log_sigmoid(z) + later), 0.0)
    return jnp.einsum('bhqk,bkhd->bqhd', a, v)


def _stick_breaking(q, k, v, start, bias):
    b, t = q.shape[:2]
    k_pos = jnp.arange(k.shape[1])
    q_pos = start + jnp.arange(t)
    blk = _chunk_len(t, Q_BLOCK)
    nb = t // blk
    qb = jnp.moveaxis(q.reshape(b, nb, blk, N_HEADS, HEAD_DIM), 1, 0)
    pb = q_pos.reshape(nb, blk)
    ob = lax.map(lambda qp: _sb_block(qp[0], k, v, qp[1], k_pos, bias), (qb, pb))
    return jnp.moveaxis(ob, 0, 1).reshape(b, t, N_HEADS, HEAD_DIM)


def _mlstm(q, k, v, log_i, log_f, c0, n0, m0):
    L = _chunk_len(q.shape[1], CHUNK)
    causal = jnp.tril(jnp.ones((L, L), dtype=bool))

    def step(carry, inp):
        c, n, m = carry
        qc, kc, vc, li, lf = inp
        bcum = jnp.cumsum(lf, axis=-1)
        log_d = jnp.where(causal, bcum[..., :, None] - bcum[..., None, :] + li[..., None, :], -jnp.inf)
        log_inter = bcum + m[..., None]
        m_row = jnp.maximum(log_inter, jnp.max(log_d, axis=-1))
        s = jnp.einsum('bhld,bhsd->bhls', qc, kc) * jnp.exp(log_d - m_row[..., None])
        w_inter = jnp.exp(log_inter - m_row)
        num = jnp.einsum('bhls,bhsd->bhld', s, vc) + w_inter[..., None] * jnp.einsum('bhld,bhde->bhle', qc, c)
        den = jnp.sum(s, axis=-1) + w_inter * jnp.einsum('bhld,bhd->bhl', qc, n)
        h = num / jnp.maximum(jnp.abs(den), jnp.exp(-m_row))[..., None]
        b_last = bcum[..., -1]
        log_w = b_last[..., None] - bcum + li
        m_new = jnp.maximum(b_last + m, jnp.max(log_w, axis=-1))
        w = jnp.exp(log_w - m_new[..., None])
        decay = jnp.exp(b_last + m - m_new)
        c_new = decay[..., None, None] * c + jnp.einsum('bhs,bhsd,bhse->bhde', w, kc, vc)
        n_new = decay[..., None] * n + jnp.einsum('bhs,bhsd->bhd', w, kc)
        return (c_new, n_new, m_new), h

    xs = (_to_chunks(q, L), _to_chunks(k, L), _to_chunks(v, L), _to_chunks(log_i, L), _to_chunks(log_f, L))
    (c1, n1, m1), h = lax.scan(step, (c0, n0, m0), xs)
    return _from_chunks(h), c1, n1, m1


def _gated_delta(q, k, v, beta, g, s0):
    L = _chunk_len(q.shape[1], CHUNK)
    incl = jnp.tril(jnp.ones((L, L), dtype=bool))
    strict = jnp.tril(jnp.ones((L, L), dtype=bool), k=-1)
    qc, kc, vc = _to_chunks(q, L), _to_chunks(k, L), _to_chunks(v, L)
    bc = _to_chunks(beta, L)
    gc = jnp.cumsum(_to_chunks(g, L), axis=-1)
    decay = jnp.exp(jnp.where(incl, gc[..., :, None] - gc[..., None, :], -jnp.inf))
    kb = kc * bc[..., None]
    a_strict = jnp.where(strict, jnp.einsum('...ld,...sd->...ls', kb, kc) * decay, 0.0)
    rhs = jnp.concatenate([vc * bc[..., None], kb * jnp.exp(gc)[..., None]], axis=-1)
    sol = lax.linalg.triangular_solve(jnp.eye(L, dtype=rhs.dtype) + a_strict, rhs, left_side=True, lower=True)
    u, w = sol[..., :HEAD_DIM], sol[..., HEAD_DIM:]
    qk = jnp.einsum('...ld,...sd->...ls', qc, kc) * decay
    q_dec = qc * jnp.exp(gc)[..., None]
    k_dec = kc * jnp.exp(gc[..., -1:] - gc)[..., None]
    g_last = jnp.exp(gc[..., -1])

    def step(s, inp):
        u_i, w_i, qk_i, qd_i, kd_i, gl_i = inp
        v_new = u_i - jnp.einsum('bhld,bhde->bhle', w_i, s)
        o = jnp.einsum('bhld,bhde->bhle', qd_i, s) + jnp.einsum('bhls,bhse->bhle', qk_i, v_new)
        s = s * gl_i[..., None, None] + jnp.einsum('bhld,bhle->bhde', kd_i, v_new)
        return s, o

    s1, o = lax.scan(step, s0, (u, w, qk, q_dec, k_dec, g_last))
    return _from_chunks(o), s1


def _retention(q, k, v, s0):
    L = _chunk_len(q.shape[1], CHUNK)
    log_gamma = jnp.log1p(-jnp.exp2(-5.0 - jnp.arange(N_HEADS, dtype=jnp.float32)))
    causal = jnp.tril(jnp.ones((L, L), dtype=bool))
    idx = jnp.arange(L, dtype=jnp.float32)
    diff = jnp.maximum(idx[:, None] - idx[None, :], 0.0)
    dmat = jnp.where(causal, jnp.exp(log_gamma[:, None, None] * diff), 0.0)
    xi = jnp.exp(log_gamma[:, None] * (idx + 1.0))
    zeta = jnp.exp(log_gamma[:, None] * (L - 1.0 - idx))
    g_chunk = jnp.exp(log_gamma * L)
    qc, kc, vc = _to_chunks(q, L), _to_chunks(k, L), _to_chunks(v, L)
    intra = jnp.einsum('...ls,...se->...le', jnp.einsum('...ld,...sd->...ls', qc, kc) * dmat, vc)
    kz = kc * zeta[..., None]

    def step(s, inp):
        q_i, kz_i, v_i = inp
        o_inter = jnp.einsum('bhld,bhde->bhle', q_i, s) * xi[..., None]
        s = s * g_chunk[:, None, None] + jnp.einsum('bhld,bhle->bhde', kz_i, v_i)
        return s, o_inter

    s1, inter = lax.scan(step, s0, (qc, kz, vc))
    return _from_chunks(intra + inter), s1


def _mixing(h, start, sb_k_past, sb_v_past, c0, n0, m0, s_gd0, conv0, s_rt0,
            w_in, sb_bias, gate_bias, conv_w, a_log, dt_bias, head_norm_w, w_out):
    f32 = jnp.float32
    b, t, _ = h.shape
    p = jnp.einsum('btd,dn->btn', h, w_in)
    sb, ml, gd, rt = jnp.split(p, [SB_COLS, SB_COLS + ML_COLS, SB_COLS + ML_COLS + GD_COLS], axis=-1)

    def heads(a):
        return a.reshape(b, t, N_HEADS, HEAD_DIM).astype(f32)

    sb_q, sb_k_new, sb_v_new = jnp.split(sb, 3, axis=-1)
    sb_k_new = sb_k_new.reshape(b, t, N_HEADS, HEAD_DIM)
    sb_v_new = sb_v_new.reshape(b, t, N_HEADS, HEAD_DIM)
    k_all = jnp.concatenate([sb_k_past.astype(f32), sb_k_new.astype(f32)], axis=1)
    v_all = jnp.concatenate([sb_v_past.astype(f32), sb_v_new.astype(f32)], axis=1)
    o_sb = _head_rmsnorm(_stick_breaking(heads(sb_q), k_all, v_all, start, sb_bias.astype(f32)))

    ml_q, ml_k, ml_v, ml_o = jnp.split(ml[..., :4 * GROUP_W], 4, axis=-1)
    ml_if = ml[..., 4 * GROUP_W:].astype(f32) + gate_bias.astype(f32)
    log_i = ml_if[..., :N_HEADS]
    log_f = jax.nn.log_sigmoid(ml_if[..., N_HEADS:])
    h_ml, c1, n1, m1 = _mlstm(heads(ml_q), heads(ml_k) * (HEAD_DIM ** -0.5), heads(ml_v), log_i, log_f,
                              c0.astype(f32), n0.astype(f32), m0.astype(f32))
    o_ml = _head_rmsnorm(h_ml) * jax.nn.sigmoid(heads(ml_o))

    gd_qkv = gd[..., :3 * GROUP_W]
    x_ext = jnp.concatenate([conv0.astype(gd_qkv.dtype), gd_qkv], axis=1)
    conv_new = x_ext[:, x_ext.shape[1] - (CONV_W - 1):]
    qkv = jax.nn.silu(_causal_conv(x_ext, conv_w).astype(f32))
    g_q, g_k, g_v = jnp.split(qkv, 3, axis=-1)
    g_q = _l2norm(heads(g_q)) * (HEAD_DIM ** -0.5)
    g_k = _l2norm(heads(g_k))
    beta = jax.nn.sigmoid(gd[..., 4 * GROUP_W:4 * GROUP_W + N_HEADS].astype(f32))
    g = -jnp.exp(a_log.astype(f32)) * jax.nn.softplus(gd[..., 4 * GROUP_W + N_HEADS:].astype(f32) + dt_bias.astype(f32))
    o_gd, s_gd1 = _gated_delta(g_q, g_k, heads(g_v), beta, g, s_gd0.astype(f32))
    o_gd = _head_rmsnorm(o_gd) * jax.nn.silu(heads(gd[..., 3 * GROUP_W:4 * GROUP_W]))

    r_q, r_k, r_v, r_g = jnp.split(rt, 4, axis=-1)
    pos = start + jnp.arange(t)
    o_rt, s_rt1 = _retention(_rope(heads(r_q), pos), _rope(heads(r_k), pos) * (HEAD_DIM ** -0.5), heads(r_v),
                             s_rt0.astype(f32))
    o_rt = _head_rmsnorm(o_rt) * jax.nn.silu(heads(r_g))

    mixed = jnp.concatenate([o_sb, o_ml, o_gd, o_rt], axis=2).reshape(b, t, D_MIX) * head_norm_w.astype(f32)
    y = jnp.einsum('btm,md->btd', mixed, w_out.astype(f32)).astype(h.dtype)
    return y, (sb_k_new, sb_v_new, c1, n1, m1, s_gd1, conv_new, s_rt1)


def _hier_moe(h, rg_w, rg_b, re_w, re_b, w_gate, w_up, w_down):
    f32 = jnp.float32
    lead = h.shape[:-1]
    hf = h.reshape(-1, D_MODEL)
    g_logits = (hf @ rg_w).astype(f32) + rg_b.astype(f32)
    g_sel = jnp.argmax(g_logits, axis=-1)
    g_prob = jnp.take_along_axis(jax.nn.softmax(g_logits, axis=-1), g_sel[:, None], axis=1)
    e_logits = ((hf @ re_w).astype(f32) + re_b.astype(f32)).reshape(-1, N_GROUPS, EXPERTS_PER_GROUP)
    e_group = jnp.take_along_axis(e_logits, g_sel[:, None, None], axis=1)[:, 0]
    top_v, top_i = lax.top_k(e_group, TOP_K)
    gate = jax.nn.softmax(top_v, axis=-1) * g_prob
    expert_idx = g_sel[:, None] * EXPERTS_PER_GROUP + top_i
    combine = jnp.sum(jax.nn.one_hot(expert_idx, N_EXPERTS, dtype=f32) * gate[..., None], axis=1)
    out = jnp.zeros(hf.shape, f32)
    for e in range(N_EXPERTS):
        a = jax.nn.silu(hf @ w_gate[e]) * (hf @ w_up[e])
        out = out + combine[:, e:e + 1] * (a @ w_down[e]).astype(f32)
    return out.reshape(lead + (D_MODEL,)).astype(h.dtype)


def _layer(x, start, sb_k_past, sb_v_past, c0, n0, m0, s_gd0, conv0, s_rt0,
           norm_attn_w, w_in, sb_bias, gate_bias, conv_w, a_log, dt_bias, head_norm_w, w_out,
           norm_ffn_w, rg_w, rg_b, re_w, re_b, w_gate, w_up, w_down):
    y, states = _mixing(_rmsnorm(x, norm_attn_w), start, sb_k_past, sb_v_past, c0, n0, m0, s_gd0, conv0, s_rt0,
                        w_in, sb_bias, gate_bias, conv_w, a_log, dt_bias, head_norm_w, w_out)
    x = x + y
    x = x + _hier_moe(_rmsnorm(x, norm_ffn_w), rg_w, rg_b, re_w, re_b, w_gate, w_up, w_down)
    return x, states


def setup_inputs(seed: int = 0) -> dict:
    key = jax.random.key(seed)
    ks = jax.random.split(key, 32)
    nrm = jax.random.normal
    f32 = jnp.float32
    n_pages = PAST_LEN // PAGE_SIZE
    n_pool = (DEC_BATCH * n_pages * 5) // 4
    hd = HEAD_DIM
    page_table = jax.random.permutation(ks[10], n_pool)[:DEC_BATCH * n_pages].reshape(DEC_BATCH, n_pages).astype(jnp.int32)
    f_bias = jnp.linspace(3.0, 6.0, N_HEADS, dtype=f32)[None, :] + 0.1 * nrm(ks[12], (DEPTH, N_HEADS), f32)
    i_bias = 0.1 * nrm(ks[13], (DEPTH, N_HEADS), f32)
    dt = jnp.exp(jax.random.uniform(ks[16], (DEPTH, N_HEADS), f32, math.log(1e-3), math.log(0.1)))
    return {
        'x_prompt': nrm(ks[0], (BATCH, SEQ, D_MODEL), f32),
        'x_sample': nrm(ks[1], (DEC_BATCH, DEC_SEQ, D_MODEL), f32),
        'cache_sb_k': nrm(ks[2], (DEPTH, n_pool, PAGE_SIZE, N_HEADS, hd), f32),
        'cache_sb_v': nrm(ks[3], (DEPTH, n_pool, PAGE_SIZE, N_HEADS, hd), f32),
        'state_mlstm_c': 0.1 * nrm(ks[4], (DEPTH, DEC_BATCH, N_HEADS, hd, hd), f32),
        'state_mlstm_n': 0.1 * nrm(ks[5], (DEPTH, DEC_BATCH, N_HEADS, hd), f32),
        'state_mlstm_m': nrm(ks[6], (DEPTH, DEC_BATCH, N_HEADS), f32),
        'state_gdn_s': 0.1 * nrm(ks[7], (DEPTH, DEC_BATCH, N_HEADS, hd, hd), f32),
        'state_gdn_conv': nrm(ks[8], (DEPTH, DEC_BATCH, CONV_W - 1, 3 * GROUP_W), f32),
        'state_ret_s': 0.1 * nrm(ks[9], (DEPTH, DEC_BATCH, N_HEADS, hd, hd), f32),
        'page_table': page_table,
        'norm_attn_w': 1.0 + 0.1 * nrm(ks[11], (DEPTH, D_MODEL), f32),
        'w_in': nrm(ks[14], (DEPTH, D_MODEL, N_IN), f32) * D_MODEL ** -0.5,
        'sb_logit_bias': SB_BIAS_INIT + 0.3 * nrm(ks[29], (DEPTH, N_HEADS), f32),
        'mlstm_gate_bias': jnp.concatenate([i_bias, f_bias], axis=-1),
        'gdn_conv_w': nrm(ks[15], (DEPTH, CONV_W, 3 * GROUP_W), f32) * CONV_W ** -0.5,
        'gdn_a_log': jnp.log(jax.random.uniform(ks[17], (DEPTH, N_HEADS), f32, 1.0, 16.0)),
        'gdn_dt_bias': dt + jnp.log(-jnp.expm1(-dt)),
        'head_norm_w': 1.0 + 0.1 * nrm(ks[18], (DEPTH, D_MIX), f32),
        'w_out': nrm(ks[19], (DEPTH, D_MIX, D_MODEL), f32) * D_MIX ** -0.5,
        'norm_ffn_w': 1.0 + 0.1 * nrm(ks[20], (DEPTH, D_MODEL), f32),
        'router_group_w': nrm(ks[21], (DEPTH, D_MODEL, N_GROUPS), f32) * D_MODEL ** -0.5,
        'router_group_b': 0.01 * nrm(ks[22], (DEPTH, N_GROUPS), f32),
        'router_expert_w': nrm(ks[23], (DEPTH, D_MODEL, N_EXPERTS), f32) * D_MODEL ** -0.5,
        'router_expert_b': 0.01 * nrm(ks[24], (DEPTH, N_EXPERTS), f32),
        'expert_w_gate': nrm(ks[25], (DEPTH, N_EXPERTS, D_MODEL, D_EXPERT), f32) * D_MODEL ** -0.5,
        'expert_w_up': nrm(ks[26], (DEPTH, N_EXPERTS, D_MODEL, D_EXPERT), f32) * D_MODEL ** -0.5,
        'expert_w_down': nrm(ks[27], (DEPTH, N_EXPERTS, D_EXPERT, D_MODEL), f32) * D_EXPERT ** -0.5,
        'final_norm_w': 1.0 + 0.1 * nrm(ks[28], (D_MODEL,), f32),
    }


def reference(x_prompt, x_sample, cache_sb_k, cache_sb_v, state_mlstm_c, state_mlstm_n, state_mlstm_m,
              state_gdn_s, state_gdn_conv, state_ret_s, page_table, norm_attn_w, w_in, sb_logit_bias, mlstm_gate_bias,
              gdn_conv_w, gdn_a_log, gdn_dt_bias, head_norm_w, w_out, norm_ffn_w, router_group_w, router_group_b,
              router_expert_w, router_expert_b, expert_w_gate, expert_w_up, expert_w_down, final_norm_w):
    f32 = jnp.float32
    bp = x_prompt.shape[0]
    bs = x_sample.shape[0]
    n_pages = PAST_LEN // PAGE_SIZE
    empty_kv = jnp.zeros((bp, 0, N_HEADS, HEAD_DIM), x_prompt.dtype)
    zc = jnp.zeros((bp, N_HEADS, HEAD_DIM, HEAD_DIM), f32)
    zn = jnp.zeros((bp, N_HEADS, HEAD_DIM), f32)
    zm = jnp.zeros((bp, N_HEADS), f32)
    zconv = jnp.zeros((bp, CONV_W - 1, 3 * GROUP_W), x_prompt.dtype)
    xp, xs = x_prompt, x_sample
    states_p, states_s = [], []
    for l in range(DEPTH):
        xp, st_p = _layer(xp, 0, empty_kv, empty_kv, zc, zn, zm, zc, zconv, zc,
                          norm_attn_w[l], w_in[l], sb_logit_bias[l], mlstm_gate_bias[l], gdn_conv_w[l], gdn_a_log[l],
                          gdn_dt_bias[l], head_norm_w[l], w_out[l], norm_ffn_w[l], router_group_w[l], router_group_b[l],
                          router_expert_w[l], router_expert_b[l], expert_w_gate[l], expert_w_up[l], expert_w_down[l])
        k_past = cache_sb_k[l][page_table].reshape(bs, n_pages * PAGE_SIZE, N_HEADS, HEAD_DIM)
        v_past = cache_sb_v[l][page_table].reshape(bs, n_pages * PAGE_SIZE, N_HEADS, HEAD_DIM)
        xs, st_s = _layer(xs, PAST_LEN, k_past, v_past, state_mlstm_c[l], state_mlstm_n[l], state_mlstm_m[l],
                          state_gdn_s[l], state_gdn_conv[l], state_ret_s[l],
                          norm_attn_w[l], w_in[l], sb_logit_bias[l], mlstm_gate_bias[l], gdn_conv_w[l], gdn_a_log[l],
                          gdn_dt_bias[l], head_norm_w[l], w_out[l], norm_ffn_w[l], router_group_w[l], router_group_b[l],
                          router_expert_w[l], router_expert_b[l], expert_w_gate[l], expert_w_up[l], expert_w_down[l])
        states_p.append(st_p)
        states_s.append(st_s)
    y_prompt = _rmsnorm(xp, final_norm_w)
    y_sample = _rmsnorm(xs, final_norm_w)
    sp = [jnp.stack(a, axis=0) for a in zip(*states_p)]
    ss = [jnp.stack(a, axis=0) for a in zip(*states_s)]
    sb_k_p, sb_v_p, ml_c_p, ml_n_p, ml_m_p, gd_s_p, gd_conv_p, rt_s_p = sp
    sb_k_s, sb_v_s, ml_c_s, ml_n_s, ml_m_s, gd_s_s, gd_conv_s, rt_s_s = ss
    return (y_prompt, y_sample, sb_k_p, sb_v_p, sb_k_s, sb_v_s,
            ml_c_p, ml_n_p, ml_m_p, ml_c_s, ml_n_s, ml_m_s,
            gd_s_p, gd_conv_p, gd_s_s, gd_conv_s, rt_s_p, rt_s_s)
```

```python
import functools
import math

import numpy as np
import jax
import jax.numpy as jnp
from jax import lax
from jax.experimental import pallas as pl
from jax.experimental.pallas import tpu as pltpu

F32 = jnp.float32
BF16 = jnp.bfloat16
I32 = jnp.int32

D_MODEL = 1024
BATCH = 8
SEQ = 2048
DEPTH = 2
DEC_BATCH = 128
PAST_LEN = 2048
PAGE_SIZE = 128
N_PAGES = PAST_LEN // PAGE_SIZE
HEAD_DIM = 64
N_HEADS = 4
GROUP_W = N_HEADS * HEAD_DIM
CHUNK = 64
CONV_W = 4
N_GROUPS = 4
EXPERTS_PER_GROUP = 8
N_EXPERTS = N_GROUPS * EXPERTS_PER_GROUP
D_EXPERT = D_MODEL // 2
ROPE_BASE = 10000.0
EPS = 1e-6
SCALE = HEAD_DIM ** -0.5

N_PROMPT = BATCH * SEQ
LANES = 128
N_MAIN = 3 * GROUP_W + 3 * 4 * GROUP_W
N_PROJ = N_MAIN + LANES
COL_ML = 3 * GROUP_W
COL_GD = COL_ML + 4 * GROUP_W
COL_RT = COL_GD + 4 * GROUP_W
COL_GATES = N_MAIN

SB_BLOCK = 256
TM_PROMPT = 256
TM_EXPERT = 256
N_ASSIGN = 2 * (N_PROMPT + DEC_BATCH)
N_TILES = -(-N_ASSIGN // TM_EXPERT) + N_EXPERTS
N_SLOTS = N_TILES * TM_EXPERT
VMEM_LIMIT = 48 * 1024 * 1024

_NT = (((1,), (1,)), ((), ()))
_TN = (((0,), (0,)), ((), ()))


_NN = (((1,), (0,)), ((), ()))


def _mm(a, b, dims):
    return lax.dot_general(a.astype(BF16), b.astype(BF16), dims, preferred_element_type=F32)


def _dot(a, b):
    return _mm(a, b, _NN)


def _dot_nt(a, b):
    return _mm(a, b, _NT)


def _dot_tn(a, b):
    return _mm(a, b, _TN)


def _split3(x):
    x1 = x.astype(BF16)
    r1 = x - x1.astype(F32)
    x2 = r1.astype(BF16)
    x3 = (r1 - x2.astype(F32)).astype(BF16)
    return x1, x2, x3


def _tri_dot(tri, x):
    x1, x2, x3 = _split3(x)
    return _dot(tri, x1) + _dot(tri, x2) + _dot(tri, x3)


def _log_sigmoid(z):
    return jnp.minimum(z, 0.0) - jnp.log1p(jnp.exp(-jnp.abs(z)))


def _softplus(z):
    return jnp.maximum(z, 0.0) + jnp.log1p(jnp.exp(-jnp.abs(z)))


def _sigmoid(z):
    return 1.0 / (1.0 + jnp.exp(-z))


def _head_rms(x):
    return x * lax.rsqrt(jnp.mean(x * x, axis=-1, keepdims=True) + EPS)


def _cparams(sem):
    return pltpu.CompilerParams(dimension_semantics=sem, vmem_limit_bytes=VMEM_LIMIT)


def _const_spec(shape):
    nd = len(shape)
    return pl.BlockSpec(shape, lambda *_: (0,) * nd)


def _norm_rows(x, w):
    return x * lax.rsqrt(jnp.mean(x * x, axis=-1, keepdims=True) + EPS) * w


def _gates_dot(h, wgh_ref, wgl_ref):
    hb = h.astype(BF16)
    hl = (h - hb.astype(F32)).astype(BF16)
    return _dot(hb, wgh_ref[...]) + _dot(hl, wgh_ref[...]) + _dot(hb, wgl_ref[...])


def _inproj_prompt_kernel(x_ref, nw_ref, w_ref, wgh_ref, wgl_ref,
                          sbq_ref, sbk_ref, sbv_ref, ml_ref, gd_ref, rt_ref, gt_ref):
    h = _norm_rows(x_ref[...], nw_ref[...])
    hb = h.astype(BF16)
    seg = lambda a, b: _dot(hb, w_ref[:, a:b])
    sbq_ref[...] = seg(0, GROUP_W)
    sbk_ref[...] = seg(GROUP_W, 2 * GROUP_W)
    sbv_ref[...] = seg(2 * GROUP_W, 3 * GROUP_W)
    ml_ref[...] = seg(COL_ML, COL_GD)
    gd_ref[...] = seg(COL_GD, COL_RT)
    rt_ref[...] = seg(COL_RT, N_MAIN)
    gt_ref[...] = _gates_dot(h, wgh_ref, wgl_ref)


def _inproj_prompt(x, nw, w, wgh, wgl):
    n = x.shape[0]
    tm = TM_PROMPT
    row = lambda c: pl.BlockSpec((tm, c), lambda i: (i, 0))
    S = jax.ShapeDtypeStruct
    return pl.pallas_call(
        _inproj_prompt_kernel,
        grid=(n // tm,),
        in_specs=[row(D_MODEL), _const_spec((1, D_MODEL)), _const_spec((D_MODEL, N_MAIN)),
                  _const_spec((D_MODEL, LANES)), _const_spec((D_MODEL, LANES))],
        out_specs=[row(GROUP_W), row(GROUP_W), row(GROUP_W), row(4 * GROUP_W), row(4 * GROUP_W),
                   row(4 * GROUP_W), row(LANES)],
        out_shape=[S((n, GROUP_W), F32)] * 3 + [S((n, 4 * GROUP_W), F32)] * 3 + [S((n, LANES), F32)],
        compiler_params=_cparams(("parallel",)),
        name="inproj_prompt",
    )(x, nw, w, wgh, wgl)


def _inproj_decode_kernel(x_ref, nw_ref, w_ref, wgh_ref, wgl_ref, sbq_ref, gdraw_ref, pt_ref):
    h = _norm_rows(x_ref[...], nw_ref[...])
    hb = h.astype(BF16)
    for j in range(N_MAIN // LANES):
        p = _dot(hb, w_ref[:, j * LANES:(j + 1) * LANES])
        if j < GROUP_W // LANES:
            sbq_ref[:, j * LANES:(j + 1) * LANES] = p
        c0 = j * LANES - COL_GD
        if 0 <= c0 < 3 * GROUP_W:
            gdraw_ref[:, c0:c0 + LANES] = p
        pt_ref[j * LANES:(j + 1) * LANES, :] = p.T
    pt_ref[N_MAIN:N_PROJ, :] = _gates_dot(h, wgh_ref, wgl_ref).T


def _inproj_decode(x, nw, w, wgh, wgl):
    S = jax.ShapeDtypeStruct
    return pl.pallas_call(
        _inproj_decode_kernel,
        out_shape=[S((DEC_BATCH, GROUP_W), F32), S((DEC_BATCH, 3 * GROUP_W), F32), S((N_PROJ, DEC_BATCH), F32)],
        compiler_params=_cparams(None),
        name="inproj_decode",
    )(x, nw, w, wgh, wgl)


def _sb_weights(z, tri, carry, mask):
    lsz = _log_sigmoid(z)
    lk = lsz - z
    if mask is not None:
        lk = jnp.where(mask, lk, 0.0)
    lw = _dot(lk, tri)
    a = jnp.exp(lsz + lw + carry)
    if mask is not None:
        a = jnp.where(mask, a, 0.0)
    return a, lw[:, 0:1] + lk[:, 0:1]


def _sb_prompt_kernel(bias_ref, q_ref, k_ref, v_ref, tri_ref, o_ref, acc_s, carry_s):
    tb = SB_BLOCK
    qi = pl.program_id(1)
    tri = tri_ref[...]
    r = lax.broadcasted_iota(I32, (tb, tb), 0)
    c = lax.broadcasted_iota(I32, (tb, tb), 1)
    dmask = c < r
    acc_s[...] = jnp.zeros_like(acc_s)
    carry_s[...] = jnp.zeros_like(carry_s)
    q = q_ref[...] * SCALE

    def block(j, mask):
        start = pl.multiple_of(j * tb, tb)
        kj = k_ref[pl.ds(start, tb), :]
        vj = v_ref[pl.ds(start, tb), :]
        for h in range(N_HEADS):
            sl = slice(h * HEAD_DIM, (h + 1) * HEAD_DIM)
            z = _dot_nt(q[:, sl], kj[:, sl]) + bias_ref[h]
            a, tot = _sb_weights(z, tri, carry_s[:, h:h + 1], mask)
            acc_s[:, sl] += _dot(a, vj[:, sl])
            carry_s[:, h:h + 1] += tot

    block(qi, dmask)

    def body(it, _):
        block(qi - 1 - it, None)
        return 0

    lax.fori_loop(0, qi, body, 0)
    for h in range(N_HEADS):
        sl = slice(h * HEAD_DIM, (h + 1) * HEAD_DIM)
        o_ref[:, sl] = _head_rms(acc_s[:, sl])


def _sb_prompt(q, k, v, bias, tri):
    tb = SB_BLOCK
    nq = SEQ // tb
    return pl.pallas_call(
        _sb_prompt_kernel,
        grid_spec=pltpu.PrefetchScalarGridSpec(
            num_scalar_prefetch=0,
            grid=(BATCH, nq),
            in_specs=[pl.BlockSpec(memory_space=pltpu.SMEM),
                      pl.BlockSpec((tb, GROUP_W), lambda b, i: (b * nq + i, 0)),
                      pl.BlockSpec((SEQ, GROUP_W), lambda b, i: (b, 0)),
                      pl.BlockSpec((SEQ, GROUP_W), lambda b, i: (b, 0)),
                      _const_spec((tb, tb))],
            out_specs=pl.BlockSpec((tb, GROUP_W), lambda b, i: (b * nq + i, 0)),
            scratch_shapes=[pltpu.VMEM((tb, GROUP_W), F32), pltpu.VMEM((tb, LANES), F32)]),
        out_shape=jax.ShapeDtypeStruct((N_PROMPT, GROUP_W), F32),
        compiler_params=_cparams(("parallel", "parallel")),
        name="sb_prompt",
    )(bias, q, k, v, tri)


def _sb_decode_kernel(pt_ref, bias_ref, q_ref, *rest):
    k_refs = rest[:N_PAGES]
    v_refs = rest[N_PAGES:2 * N_PAGES]
    tri_ref = rest[2 * N_PAGES]
    o_ref = rest[2 * N_PAGES + 1]
    b = pl.program_id(0)
    r = b % 8
    qrow = q_ref[pl.ds(r, 1), :] * SCALE
    rowi = lax.broadcasted_iota(I32, (8, GROUP_W), 0)
    lanei = lax.broadcasted_iota(I32, (8, GROUP_W), 1)
    head_of_lane = jnp.right_shift(lanei, 6)
    qbd = jnp.where(head_of_lane == rowi, qrow, 0.0).astype(BF16)
    row8 = lax.broadcasted_iota(I32, (8, 1), 0)
    bias = jnp.zeros((8, 1), F32)
    for h in range(N_HEADS):
        bias = jnp.where(row8 == h, bias_ref[h], bias)
    tri = tri_ref[...]
    carry = jnp.zeros((8, 1), F32)
    acc = jnp.zeros((8, GROUP_W), F32)
    for j in reversed(range(N_PAGES)):
        kt = k_refs[j][0, 0]
        z = _dot(qbd, kt) + bias
        a, tot = _sb_weights(z, tri, carry, None)
        acc = acc + _dot_nt(a, v_refs[j][0, 0])
        carry = carry + tot
    own = jnp.where(head_of_lane == rowi, acc, 0.0)
    orow = jnp.sum(own, axis=0, keepdims=True)
    pieces = [_head_rms(orow[:, h * HEAD_DIM:(h + 1) * HEAD_DIM]) for h in range(N_HEADS)]
    o_ref[pl.ds(r, 1), :] = jnp.concatenate(pieces, axis=1)


def _sb_decode(q, cache_kt, cache_vt, page_table, bias, tri, layer):
    def page_spec(j):
        return pl.BlockSpec((1, 1, GROUP_W, PAGE_SIZE), lambda b, pt: (layer, pt[b, j], 0, 0))

    in_specs = ([pl.BlockSpec(memory_space=pltpu.SMEM),
                 pl.BlockSpec((8, GROUP_W), lambda b, pt: (b // 8, 0))]
                + [page_spec(j) for j in range(N_PAGES)] * 2
                + [pl.BlockSpec((PAGE_SIZE, PAGE_SIZE), lambda b, pt: (0, 0))])
    return pl.pallas_call(
        _sb_decode_kernel,
        grid_spec=pltpu.PrefetchScalarGridSpec(
            num_scalar_prefetch=1,
            grid=(DEC_BATCH,),
            in_specs=in_specs,
            out_specs=pl.BlockSpec((8, GROUP_W), lambda b, pt: (b // 8, 0))),
        out_shape=jax.ShapeDtypeStruct((DEC_BATCH, GROUP_W), F32),
        compiler_params=_cparams(("arbitrary",)),
        name="sb_decode",
    )(page_table, bias, q, *([cache_kt] * N_PAGES), *([cache_vt] * N_PAGES), tri)


def _rows_to_lanes(x):
    L = x.shape[0]
    return jnp.concatenate([x, jnp.zeros((LANES - L, LANES), F32)], axis=0).T


def _chunk_masks(L):
    r = lax.broadcasted_iota(I32, (L, L), 0)
    c = lax.broadcasted_iota(I32, (L, L), 1)
    return c <= r, c < r


def _mlstm_prompt_kernel(x_ref, g_ref, gb_ref, tril_ref, o_ref, c1_ref, n1_ref, m1_ref, c_s, n_s, m_s):
    L = CHUNK
    ci = pl.program_id(1)

    @pl.when(ci == 0)
    def _():
        c_s[...] = jnp.zeros_like(c_s)
        n_s[...] = jnp.zeros_like(n_s)
        m_s[...] = jnp.zeros_like(m_s)

    g = g_ref[...] + gb_ref[...]
    lane = lax.broadcasted_iota(I32, (L, LANES), 1)
    lf = jnp.where((lane >= N_HEADS) & (lane < 2 * N_HEADS), _log_sigmoid(g), 0.0)
    cum = _tri_dot(tril_ref[...], lf)
    xt = _rows_to_lanes(jnp.where(lane < N_HEADS, g, cum))
    causal, _ = _chunk_masks(L)
    for h in range(N_HEADS):
        sl = lambda blk: slice(blk * GROUP_W + h * HEAD_DIM, blk * GROUP_W + (h + 1) * HEAD_DIM)
        bc = cum[:, N_HEADS + h:N_HEADS + h + 1]
        li = g[:, h:h + 1]
        bc_row = xt[N_HEADS + h:N_HEADS + h + 1, 0:L]
        li_row = xt[h:h + 1, 0:L]
        m = m_s[0:1, h:h + 1]
        log_d = jnp.where(causal, bc - bc_row + li_row, -jnp.inf)
        log_inter = bc + m
        m_row = jnp.maximum(log_inter, jnp.max(log_d, axis=1, keepdims=True))
        q = x_ref[:, sl(0)]
        k = x_ref[:, sl(1)] * SCALE
        v = x_ref[:, sl(2)]
        og = x_ref[:, sl(3)]
        s = _dot_nt(q, k) * jnp.exp(log_d - m_row)
        w_inter = jnp.exp(log_inter - m_row)
        c = c_s[h]
        n = n_s[h:h + 1, :]
        num = _dot(s, v) + w_inter * _dot(q, c)
        den = jnp.sum(s, axis=1, keepdims=True) + w_inter * jnp.sum(q * n, axis=1, keepdims=True)
        hh = num / jnp.maximum(jnp.abs(den), jnp.exp(-m_row))
        b_last = bc[L - 1:L, :]
        log_w = b_last - bc + li
        m_new = jnp.maximum(b_last + m, jnp.max(log_w, axis=0, keepdims=True))
        w = jnp.exp(log_w - m_new)
        decay = jnp.exp(b_last + m - m_new)
        kw = k * w
        c_s[h] = decay * c + _dot_tn(kw, v)
        n_s[h:h + 1, :] = decay * n + jnp.sum(kw, axis=0, keepdims=True)
        m_s[0:1, h:h + 1] = m_new
        o_ref[:, h * HEAD_DIM:(h + 1) * HEAD_DIM] = _head_rms(hh) * _sigmoid(og)

    @pl.when(ci == pl.num_programs(1) - 1)
    def _():
        c1_ref[0] = c_s[...]
        n1_ref[0] = n_s[...]
        m1_ref[0] = m_s[...]


def _mlstm_prompt(ml, gates, gate_bias_row, tril):
    L = CHUNK
    nc = SEQ // L
    S = jax.ShapeDtypeStruct
    return pl.pallas_call(
        _mlstm_prompt_kernel,
        grid=(BATCH, nc),
        in_specs=[pl.BlockSpec((L, 4 * GROUP_W), lambda b, c: (b * nc + c, 0)),
                  pl.BlockSpec((L, LANES), lambda b, c: (b * nc + c, 0)),
                  _const_spec((1, LANES)), _const_spec((L, L))],
        out_specs=[pl.BlockSpec((L, GROUP_W), lambda b, c: (b * nc + c, 0)),
                   pl.BlockSpec((1, N_HEADS, HEAD_DIM, HEAD_DIM), lambda b, c: (b, 0, 0, 0)),
                   pl.BlockSpec((1, N_HEADS, HEAD_DIM), lambda b, c: (b, 0, 0)),
                   pl.BlockSpec((1, 1, LANES), lambda b, c: (b, 0, 0))],
        out_shape=[S((N_PROMPT, GROUP_W), F32), S((BATCH, N_HEADS, HEAD_DIM, HEAD_DIM), F32),
                   S((BATCH, N_HEADS, HEAD_DIM), F32), S((BATCH, 1, LANES), F32)],
        scratch_shapes=[pltpu.VMEM((N_HEADS, HEAD_DIM, HEAD_DIM), F32), pltpu.VMEM((N_HEADS, HEAD_DIM), F32),
                        pltpu.VMEM((1, LANES), F32)],
        compiler_params=_cparams(("parallel", "arbitrary")),
        name="mlstm_prompt",
    )(ml, gates, gate_bias_row, tril)


def _unit_lower_inverse(a, L):
    r = lax.broadcasted_iota(I32, (L, L), 0)
    c = lax.broadcasted_iota(I32, (L, L), 1)
    p = jnp.where(r == c, 1.0, 0.0) - a
    x = a
    power = 1
    while 2 * power < L:
        x = _dot(x, x)
        p = p + _dot(p, x)
        power *= 2
    return p


def _gdn_prompt_kernel(x_ref, g_ref, gb_ref, al_ref, cw_ref, tril_ref, o_ref, s1_ref, cv_ref, s_s, xe_s):
    L = CHUNK
    nq = 3 * GROUP_W
    ci = pl.program_id(1)

    @pl.when(ci == 0)
    def _():
        s_s[...] = jnp.zeros_like(s_s)
        xe_s[0:8, :] = jnp.zeros((8, nq), F32)

    raw = x_ref[:, 0:nq]
    xe_s[8:8 + L, :] = raw
    conv = (cw_ref[3:4, :] * raw + cw_ref[2:3, :] * xe_s[7:7 + L, :]
            + cw_ref[1:2, :] * xe_s[6:6 + L, :] + cw_ref[0:1, :] * xe_s[5:5 + L, :])
    xe_s[0:8, :] = raw[L - 8:L, :]
    qkv = conv * _sigmoid(conv)

    g = g_ref[...] + gb_ref[...]
    lane = lax.broadcasted_iota(I32, (L, LANES), 1)
    beta_all = _sigmoid(g)
    gd = jnp.where((lane >= 3 * N_HEADS) & (lane < 4 * N_HEADS), -jnp.exp(al_ref[...]) * _softplus(g), 0.0)
    gcum = _tri_dot(tril_ref[...], gd)
    xt = _rows_to_lanes(gcum)
    incl, strict = _chunk_masks(L)
    for h in range(N_HEADS):
        sl = lambda blk: slice(blk * GROUP_W + h * HEAD_DIM, blk * GROUP_W + (h + 1) * HEAD_DIM)
        q = qkv[:, sl(0)]
        k = qkv[:, sl(1)]
        v = qkv[:, sl(2)]
        gate = x_ref[:, sl(3)]
        q = q * lax.rsqrt(jnp.sum(q * q, axis=1, keepdims=True) + EPS) * SCALE
        k = k * lax.rsqrt(jnp.sum(k * k, axis=1, keepdims=True) + EPS)
        b = beta_all[:, 2 * N_HEADS + h:2 * N_HEADS + h + 1]
        gc = gcum[:, 3 * N_HEADS + h:3 * N_HEADS + h + 1]
        gc_row = xt[3 * N_HEADS + h:3 * N_HEADS + h + 1, 0:L]
        decay = jnp.exp(jnp.where(incl, gc - gc_row, -jnp.inf))
        kb = k * b
        a = jnp.where(strict, _dot_nt(kb, k) * decay, 0.0)
        t = _unit_lower_inverse(a, L)
        egc = jnp.exp(gc)
        u = _dot(t, v * b)
        w = _dot(t, kb * egc)
        qk = _dot_nt(q, k) * decay
        s = s_s[h]
        v_new = u - _dot(w, s)
        o = _dot(q * egc, s) + _dot(qk, v_new)
        gl = gc[L - 1:L, :]
        s_s[h] = s * jnp.exp(gl) + _dot_tn(k * jnp.exp(gl - gc), v_new)
        o_ref[:, h * HEAD_DIM:(h + 1) * HEAD_DIM] = _head_rms(o) * (gate * _sigmoid(gate))

    @pl.when(ci == pl.num_programs(1) - 1)
    def _():
        s1_ref[0] = s_s[...]
        cv_ref[0] = xe_s[8 - (CONV_W - 1):8, :]


def _gdn_prompt(gd, gates, gate_bias_row, a_log_row, conv_w, tril):
    L = CHUNK
    nc = SEQ // L
    S = jax.ShapeDtypeStruct
    return pl.pallas_call(
        _gdn_prompt_kernel,
        grid=(BATCH, nc),
        in_specs=[pl.BlockSpec((L, 4 * GROUP_W), lambda b, c: (b * nc + c, 0)),
                  pl.BlockSpec((L, LANES), lambda b, c: (b * nc + c, 0)),
                  _const_spec((1, LANES)), _const_spec((1, LANES)), _const_spec((CONV_W, 3 * GROUP_W)),
                  _const_spec((L, L))],
        out_specs=[pl.BlockSpec((L, GROUP_W), lambda b, c: (b * nc + c, 0)),
                   pl.BlockSpec((1, N_HEADS, HEAD_DIM, HEAD_DIM), lambda b, c: (b, 0, 0, 0)),
                   pl.BlockSpec((1, CONV_W - 1, 3 * GROUP_W), lambda b, c: (b, 0, 0))],
        out_shape=[S((N_PROMPT, GROUP_W), F32), S((BATCH, N_HEADS, HEAD_DIM, HEAD_DIM), F32),
                   S((BATCH, CONV_W - 1, 3 * GROUP_W), F32)],
        scratch_shapes=[pltpu.VMEM((N_HEADS, HEAD_DIM, HEAD_DIM), F32), pltpu.VMEM((8 + L, 3 * GROUP_W), F32)],
        compiler_params=_cparams(("parallel", "arbitrary")),
        name="gdn_prompt",
    )(gd, gates, gate_bias_row, a_log_row, conv_w, tril)


def _rope_rows(x, cos, sin_signed):
    lane = lax.broadcasted_iota(I32, x.shape, 1)
    first = jnp.bitwise_and(lane, HEAD_DIM - 1) < HEAD_DIM // 2
    w = x.shape[1]
    swapped = jnp.where(first, pltpu.roll(x, w - HEAD_DIM // 2, 1), pltpu.roll(x, HEAD_DIM // 2, 1))
    return x * cos + swapped * sin_signed


def _ret_prompt_kernel(x_ref, cos_ref, sin_ref, dm_ref, xz_ref, gch_ref, o_ref, s1_ref, s_s):
    ci = pl.program_id(1)

    @pl.when(ci == 0)
    def _():
        s_s[...] = jnp.zeros_like(s_s)

    cos = cos_ref[...]
    sin = sin_ref[...]
    qr = _rope_rows(x_ref[:, 0:GROUP_W], cos, sin)
    kr = _rope_rows(x_ref[:, GROUP_W:2 * GROUP_W], cos, sin) * SCALE
    for h in range(N_HEADS):
        sl = slice(h * HEAD_DIM, (h + 1) * HEAD_DIM)
        q = qr[:, sl]
        k = kr[:, sl]
        v = x_ref[:, 2 * GROUP_W + h * HEAD_DIM:2 * GROUP_W + (h + 1) * HEAD_DIM]
        gate = x_ref[:, 3 * GROUP_W + h * HEAD_DIM:3 * GROUP_W + (h + 1) * HEAD_DIM]
        s = s_s[h]
        intra = _dot(_dot_nt(q, k) * dm_ref[h], v)
        inter = _dot(q, s) * xz_ref[:, h:h + 1]
        s_s[h] = s * gch_ref[h] + _dot_tn(k * xz_ref[:, N_HEADS + h:N_HEADS + h + 1], v)
        o_ref[:, sl] = _head_rms(intra + inter) * (gate * _sigmoid(gate))

    @pl.when(ci == pl.num_programs(1) - 1)
    def _():
        s1_ref[0] = s_s[...]


def _ret_prompt(rt, cos, sin, dmat, xz, gch):
    L = CHUNK
    nc = SEQ // L
    S = jax.ShapeDtypeStruct
    return pl.pallas_call(
        _ret_prompt_kernel,
        grid=(BATCH, nc),
        in_specs=[pl.BlockSpec((L, 4 * GROUP_W), lambda b, c: (b * nc + c, 0)),
                  pl.BlockSpec((L, GROUP_W), lambda b, c: (c, 0)),
                  pl.BlockSpec((L, GROUP_W), lambda b, c: (c, 0)),
                  _const_spec((N_HEADS, L, L)), _const_spec((L, LANES)),
                  pl.BlockSpec(memory_space=pltpu.SMEM)],
        out_specs=[pl.BlockSpec((L, GROUP_W), lambda b, c: (b * nc + c, 0)),
                   pl.BlockSpec((1, N_HEADS, HEAD_DIM, HEAD_DIM), lambda b, c: (b, 0, 0, 0))],
        out_shape=[S((N_PROMPT, GROUP_W), F32), S((BATCH, N_HEADS, HEAD_DIM, HEAD_DIM), F32)],
        scratch_shapes=[pltpu.VMEM((N_HEADS, HEAD_DIM, HEAD_DIM), F32)],
        compiler_params=_cparams(("parallel", "arbitrary")),
        name="ret_prompt",
    )(rt, cos, sin, dmat, xz, gch)


def _gdn_decode_conv_kernel(raw_ref, c0_ref, cw_ref, qkvt_ref, cv_ref):
    raw = raw_ref[...]
    conv = (cw_ref[3:4, :] * raw + cw_ref[2:3, :] * c0_ref[2] + cw_ref[1:2, :] * c0_ref[1]
            + cw_ref[0:1, :] * c0_ref[0])
    qkv = conv * _sigmoid(conv)
    cv_ref[0] = c0_ref[1]
    cv_ref[1] = c0_ref[2]
    cv_ref[2] = raw
    for blk in range(3):
        for h in range(N_HEADS):
            lo = blk * GROUP_W + h * HEAD_DIM
            x = qkv[:, lo:lo + HEAD_DIM]
            if blk == 0:
                x = x * lax.rsqrt(jnp.sum(x * x, axis=1, keepdims=True) + EPS) * SCALE
            elif blk == 1:
                x = x * lax.rsqrt(jnp.sum(x * x, axis=1, keepdims=True) + EPS)
            if h % 2 == 0:
                pair = x
            else:
                qkvt_ref[lo - HEAD_DIM:lo + HEAD_DIM, :] = jnp.concatenate([pair, x], axis=1).T


def _gdn_decode_conv(raw, conv0, conv_w):
    S = jax.ShapeDtypeStruct
    return pl.pallas_call(
        _gdn_decode_conv_kernel,
        out_shape=[S((3 * GROUP_W, DEC_BATCH), F32), S((CONV_W - 1, DEC_BATCH, 3 * GROUP_W), F32)],
        compiler_params=_cparams(None),
        name="gdn_decode_conv",
    )(raw, conv0, conv_w)


def _decode_rec_kernel(sc_ref, mlq_ref, mlk_ref, mlv_ref, mlo_ref, gq_ref, gk_ref, gv_ref, gg_ref,
                       rq_ref, rk_ref, rv_ref, rg_ref, gt_ref, cos_ref, sin_ref,
                       c0_ref, n0_ref, m0_ref, sg0_ref, sr0_ref,
                       oml_ref, ogd_ref, ort_ref, c1_ref, n1_ref, m1_ref, sg1_ref, sr1_ref, va_s, vb_s):
    h = pl.program_id(0)
    D = HEAD_DIM
    sum0 = lambda x: jnp.sum(x, axis=0, keepdims=True)
    rms0 = lambda x: x * lax.rsqrt(sum0(x * x) * (1.0 / D) + EPS)

    li = gt_ref[pl.ds(h, 1), :] + sc_ref[0, h]
    lf = _log_sigmoid(gt_ref[pl.ds(N_HEADS + h, 1), :] + sc_ref[1, h])
    m0 = m0_ref[0]
    q = mlq_ref[...]
    k = mlk_ref[...] * SCALE
    v = mlv_ref[...]
    log_inter = lf + m0
    m_row = jnp.maximum(log_inter, li)
    s = sum0(q * k) * jnp.exp(li - m_row)
    w_inter = jnp.exp(log_inter - m_row)
    decay = jnp.exp(lf + m0 - m_row)
    kw = k * jnp.exp(li - m_row)

    va_s[...] = kw

    def ml_body(d, qc):
        c_d = c0_ref[0, d]
        c1_ref[0, d] = decay * c_d + va_s[pl.ds(d, 1), :] * v
        return qc + mlq_ref[pl.ds(d, 1), :] * c_d

    n0 = n0_ref[0]
    n1_ref[0] = decay * n0 + kw
    qc = lax.fori_loop(0, D, ml_body, jnp.zeros((D, LANES), F32))
    num = s * v + w_inter * qc
    den = s + w_inter * sum0(q * n0)
    hh = num / jnp.maximum(jnp.abs(den), jnp.exp(-m_row))
    m1_ref[0] = m_row
    oml_ref[...] = rms0(hh) * _sigmoid(mlo_ref[...])

    beta = _sigmoid(gt_ref[pl.ds(2 * N_HEADS + h, 1), :])
    gdec = -sc_ref[3, h] * _softplus(gt_ref[pl.ds(3 * N_HEADS + h, 1), :] + sc_ref[2, h])
    eg = jnp.exp(gdec)
    gv = gv_ref[...]

    def ks_body(d, acc):
        return acc + gk_ref[pl.ds(d, 1), :] * sg0_ref[0, d]

    ks = lax.fori_loop(0, D, ks_body, jnp.zeros((D, LANES), F32))
    v_new = beta * gv - (beta * eg) * ks

    def gd_body(d, acc):
        s_new = eg * sg0_ref[0, d] + gk_ref[pl.ds(d, 1), :] * v_new
        sg1_ref[0, d] = s_new
        return acc + gq_ref[pl.ds(d, 1), :] * s_new

    og = lax.fori_loop(0, D, gd_body, jnp.zeros((D, LANES), F32))
    gate = gg_ref[...]
    ogd_ref[...] = rms0(og) * (gate * _sigmoid(gate))

    half = D // 2

    def rope(ref):
        x = ref[...]
        sw = jnp.concatenate([x[half:, :], x[:half, :]], axis=0)
        return x * cos_ref[...] + sw * sin_ref[...]

    va_s[...] = rope(rq_ref)
    vb_s[...] = rope(rk_ref) * SCALE
    rv = rv_ref[...]
    gamma = sc_ref[4, h]

    def rt_body(d, acc):
        s_new = gamma * sr0_ref[0, d] + vb_s[pl.ds(d, 1), :] * rv
        sr1_ref[0, d] = s_new
        return acc + va_s[pl.ds(d, 1), :] * s_new

    ort = lax.fori_loop(0, D, rt_body, jnp.zeros((D, LANES), F32))
    gate = rg_ref[...]
    ort_ref[...] = rms0(ort) * (gate * _sigmoid(gate))


def _decode_rec(scalars, pt, qkvt, cos_t, sin_t, c0, n0, m0, sg0, sr0):
    D = HEAD_DIM
    S = jax.ShapeDtypeStruct
    prow = lambda col, blk: pl.BlockSpec((D, LANES), lambda h: ((col + blk * GROUP_W) // D + h, 0))
    vec = lambda blk: pl.BlockSpec((D, LANES), lambda h: (blk * N_HEADS + h, 0))
    st4 = pl.BlockSpec((1, D, D, LANES), lambda h: (h, 0, 0, 0))
    st3 = pl.BlockSpec((1, D, LANES), lambda h: (h, 0, 0))
    st2 = pl.BlockSpec((1, 1, LANES), lambda h: (h, 0, 0))
    in_specs = ([pl.BlockSpec(memory_space=pltpu.SMEM)]
                + [prow(COL_ML, blk) for blk in range(4)]
                + [vec(0), vec(1), vec(2), prow(COL_GD, 3)]
                + [prow(COL_RT, blk) for blk in range(4)]
                + [pl.BlockSpec((LANES, LANES), lambda h: (COL_GATES // LANES, 0)),
                   _const_spec((D, LANES)), _const_spec((D, LANES)), st4, st3, st2, st4, st4])
    return pl.pallas_call(
        _decode_rec_kernel,
        grid=(N_HEADS,),
        in_specs=in_specs,
        out_specs=[vec(0), vec(0), vec(0), st4, st3, st2, st4, st4],
        out_shape=[S((GROUP_W, LANES), F32)] * 3
        + [S((N_HEADS, D, D, LANES), F32), S((N_HEADS, D, LANES), F32), S((N_HEADS, 1, LANES), F32),
           S((N_HEADS, D, D, LANES), F32), S((N_HEADS, D, D, LANES), F32)],
        scratch_shapes=[pltpu.VMEM((D, LANES), F32), pltpu.VMEM((D, LANES), F32)],
        compiler_params=_cparams(("parallel",)),
        name="decode_rec",
    )(scalars, pt, pt, pt, pt, qkvt, qkvt, qkvt, pt, pt, pt, pt, pt, pt, cos_t, sin_t, c0, n0, m0, sg0, sr0)


def _outproj_kernel(x_ref, osb_ref, oml_ref, ogd_ref, ort_ref, gain_ref, wo_ref, nw_ref, rwh_ref, rwl_ref,
                    x1_ref, h2_ref, lg_ref, *, transposed):
    parts = [osb_ref[...]]
    for ref in (oml_ref, ogd_ref, ort_ref):
        parts.append(ref[...].T if transposed else ref[...])
    y = None
    for g, p in enumerate(parts):
        cols = slice(g * GROUP_W, (g + 1) * GROUP_W)
        t = _dot(p * gain_ref[:, cols], wo_ref[cols, :])
        y = t if y is None else y + t
    x1 = x_ref[...] + y
    h2 = _norm_rows(x1, nw_ref[...])
    x1_ref[...] = x1
    h2_ref[...] = h2
    lg_ref[...] = _gates_dot(h2, rwh_ref, rwl_ref)


def _outproj(x, osb, oml, ogd, ort, gain, wo, nw, rwh, rwl, transposed):
    n = x.shape[0]
    tm = min(TM_PROMPT, n)
    S = jax.ShapeDtypeStruct
    row = lambda c: pl.BlockSpec((tm, c), lambda i: (i, 0))
    mix = _const_spec((GROUP_W, DEC_BATCH)) if transposed else row(GROUP_W)
    return pl.pallas_call(
        functools.partial(_outproj_kernel, transposed=transposed),
        grid=(n // tm,),
        in_specs=[row(D_MODEL), row(GROUP_W), mix, mix, mix, _const_spec((1, D_MODEL)),
                  _const_spec((D_MODEL, D_MODEL)), _const_spec((1, D_MODEL)),
                  _const_spec((D_MODEL, LANES)), _const_spec((D_MODEL, LANES))],
        out_specs=[row(D_MODEL), row(D_MODEL), row(LANES)],
        out_shape=[S((n, D_MODEL), F32), S((n, D_MODEL), F32), S((n, LANES), F32)],
        compiler_params=_cparams(("parallel",)),
        name="outproj",
    )(x, osb, oml, ogd, ort, gain, wo, nw, rwh, rwl)


def _route_kernel(lg_ref, rb_ref, tri_ref, cin_ref, ri_ref, rf_ref, cnt_ref):
    @pl.when(pl.program_id(0) == 0)
    def _():
        cnt_ref[...] = cin_ref[...]

    lg = lg_ref[...] + rb_ref[...]
    lane = lax.broadcasted_iota(I32, lg.shape, 1)
    ninf = -jnp.inf
    big = LANES - 1
    rmax = lambda x: jnp.max(x, axis=1, keepdims=True)
    rmin = lambda x: jnp.min(x, axis=1, keepdims=True)
    gl = jnp.where(lane < N_GROUPS, lg, ninf)
    gmax = rmax(gl)
    g_sel = rmin(jnp.where(gl == gmax, lane, big))
    g_prob = 1.0 / jnp.sum(jnp.exp(gl - gmax), axis=1, keepdims=True)
    e_lane = (lane >= N_GROUPS) & (lane < N_GROUPS + N_EXPERTS)
    em = jnp.where(e_lane & (jnp.right_shift(lane - N_GROUPS, 3) == g_sel), lg, ninf)
    v1 = rmax(em)
    i1 = rmin(jnp.where(em == v1, lane, big))
    em2 = jnp.where(lane == i1, ninf, em)
    v2 = rmax(em2)
    i2 = rmin(jnp.where(em2 == v2, lane, big))
    t = jnp.exp(v2 - v1)
    gate1 = g_prob / (1.0 + t)
    gate2 = g_prob * t / (1.0 + t)
    e1 = i1 - N_GROUPS
    e2 = i2 - N_GROUPS
    onehot = jnp.where((lane == e1) | (lane == e2), 1.0, 0.0)
    before = _dot(tri_ref[...], onehot) + cnt_ref[...]
    r1 = jnp.sum(jnp.where(lane == e1, before, 0.0), axis=1, keepdims=True).astype(I32)
    r2 = jnp.sum(jnp.where(lane == e2, before, 0.0), axis=1, keepdims=True).astype(I32)
    cnt_ref[...] += jnp.sum(onehot, axis=0, keepdims=True)
    ri_ref[...] = jnp.where(lane == 0, e1, jnp.where(lane == 1, e2, jnp.where(lane == 2, r1,
                            jnp.where(lane == 3, r2, 0))))
    rf_ref[...] = jnp.where(lane == 0, gate1, jnp.where(lane == 1, gate2, 0.0))


def _route(logits, rb, tri, cnt_in):
    n = logits.shape[0]
    tm = tri.shape[0]
    S = jax.ShapeDtypeStruct
    row = pl.BlockSpec((tm, LANES), lambda i: (i, 0))
    return pl.pallas_call(
        _route_kernel,
        grid=(n // tm,),
        in_specs=[row, _const_spec((1, LANES)), _const_spec((tm, tm)), _const_spec((1, LANES))],
        out_specs=[row, row, _const_spec((1, LANES))],
        out_shape=[S((n, LANES), I32), S((n, LANES), F32), S((1, LANES), F32)],
        compiler_params=_cparams(("arbitrary",)),
        name="route",
    )(logits, rb, tri, cnt_in)


def _dispatch_kernel(d1_ref, d2_ref, h_ref, xs_in_ref, xs_ref, sem):
    del xs_in_ref
    tm = h_ref.shape[0]
    base = pl.program_id(0) * tm

    def body(r, _):
        src = h_ref.at[pl.ds(r, 1), :]
        pltpu.make_async_copy(src, xs_ref.at[pl.ds(d1_ref[base + r], 1), :], sem.at[0]).start()
        pltpu.make_async_copy(src, xs_ref.at[pl.ds(d2_ref[base + r], 1), :], sem.at[1]).start()
        return 0

    lax.fori_loop(0, tm, body, 0)
    pltpu.make_async_copy(h_ref, xs_ref.at[pl.ds(0, tm), :], sem.at[0]).wait()
    pltpu.make_async_copy(h_ref, xs_ref.at[pl.ds(0, tm), :], sem.at[1]).wait()


def _dispatch(d1, d2, h2, xs):
    n = h2.shape[0]
    tm = min(TM_PROMPT, n)
    return pl.pallas_call(
        _dispatch_kernel,
        grid_spec=pltpu.PrefetchScalarGridSpec(
            num_scalar_prefetch=2,
            grid=(n // tm,),
            in_specs=[pl.BlockSpec((tm, D_MODEL), lambda i, a, b: (i, 0)), pl.BlockSpec(memory_space=pl.ANY)],
            out_specs=pl.BlockSpec(memory_space=pl.ANY),
            scratch_shapes=[pltpu.SemaphoreType.DMA((2,))]),
        out_shape=jax.ShapeDtypeStruct(xs.shape, xs.dtype),
        input_output_aliases={3: 0},
        compiler_params=_cparams(("arbitrary",)),
        name="dispatch",
    )(d1, d2, h2, xs)


def _experts_kernel(te_ref, tv_ref, xs_ref, wg_ref, wu_ref, wd_ref, ys_ref):
    del te_ref
    valid = tv_ref[pl.program_id(0)] > 0

    @pl.when(valid)
    def _():
        x = xs_ref[...].astype(BF16)
        a = _dot(x, wg_ref[0])
        u = _dot(x, wu_ref[0])
        act = a * _sigmoid(a) * u
        ys_ref[...] = _dot(act, wd_ref[0])

    @pl.when(jnp.logical_not(valid))
    def _():
        ys_ref[...] = jnp.zeros_like(ys_ref)


def _experts(tile_expert, tile_valid, xs, wg, wu, wd):
    tm = TM_EXPERT
    return pl.pallas_call(
        _experts_kernel,
        grid_spec=pltpu.PrefetchScalarGridSpec(
            num_scalar_prefetch=2,
            grid=(N_TILES,),
            in_specs=[pl.BlockSpec((tm, D_MODEL), lambda i, te, tv: (i, 0)),
                      pl.BlockSpec((1, D_MODEL, D_EXPERT), lambda i, te, tv: (te[i], 0, 0)),
                      pl.BlockSpec((1, D_MODEL, D_EXPERT), lambda i, te, tv: (te[i], 0, 0)),
                      pl.BlockSpec((1, D_EXPERT, D_MODEL), lambda i, te, tv: (te[i], 0, 0))],
            out_specs=pl.BlockSpec((tm, D_MODEL), lambda i, te, tv: (i, 0))),
        out_shape=jax.ShapeDtypeStruct((N_SLOTS, D_MODEL), F32),
        compiler_params=_cparams(("arbitrary",)),
        name="experts",
    )(tile_expert, tile_valid, xs, wg, wu, wd)


def _combine_kernel(d1_ref, d2_ref, x1_ref, rf_ref, fw_ref, ys_ref, out_ref, b1_s, b2_s, sem, *, final):
    tm = x1_ref.shape[0]
    base = pl.program_id(0) * tm

    def body(r, _):
        pltpu.make_async_copy(ys_ref.at[pl.ds(d1_ref[base + r], 1), :], b1_s.at[pl.ds(r, 1), :], sem.at[0]).start()
        pltpu.make_async_copy(ys_ref.at[pl.ds(d2_ref[base + r], 1), :], b2_s.at[pl.ds(r, 1), :], sem.at[1]).start()
        return 0

    lax.fori_loop(0, tm, body, 0)
    pltpu.make_async_copy(ys_ref.at[pl.ds(0, tm), :], b1_s, sem.at[0]).wait()
    pltpu.make_async_copy(ys_ref.at[pl.ds(0, tm), :], b2_s, sem.at[1]).wait()
    x2 = x1_ref[...] + rf_ref[:, 0:1] * b1_s[...] + rf_ref[:, 1:2] * b2_s[...]
    out_ref[...] = _norm_rows(x2, fw_ref[...]) if final else x2


def _combine(d1, d2, x1, rf, fw, ys, final):
    n = x1.shape[0]
    tm = min(TM_PROMPT, n)
    row = lambda c: pl.BlockSpec((tm, c), lambda i, a, b: (i, 0))
    return pl.pallas_call(
        functools.partial(_combine_kernel, final=final),
        grid_spec=pltpu.PrefetchScalarGridSpec(
            num_scalar_prefetch=2,
            grid=(n // tm,),
            in_specs=[row(D_MODEL), row(LANES), pl.BlockSpec((1, D_MODEL), lambda i, a, b: (0, 0)),
                      pl.BlockSpec(memory_space=pl.ANY)],
            out_specs=row(D_MODEL),
            scratch_shapes=[pltpu.VMEM((tm, D_MODEL), F32), pltpu.VMEM((tm, D_MODEL), F32),
                            pltpu.SemaphoreType.DMA((2,))]),
        out_shape=jax.ShapeDtypeStruct((n, D_MODEL), F32),
        compiler_params=_cparams(("arbitrary",)),
        name="combine",
    )(d1, d2, x1, rf, fw, ys)


def _tri(n, kind):
    r = np.arange(n)[:, None]
    c = np.arange(n)[None, :]
    m = {"lower_incl": c <= r, "lower_strict": c < r, "row_gt_col": r > c}[kind]
    return jnp.asarray(m.astype(np.float32), dtype=BF16)


def _rope_tables(pos):
    half = HEAD_DIM // 2
    inv = ROPE_BASE ** (-np.arange(half, dtype=np.float64) / half)
    ang = np.asarray(pos, np.float64)[:, None] * inv[None, :]
    cos = np.concatenate([np.cos(ang), np.cos(ang)], axis=1)
    sin = np.concatenate([-np.sin(ang), np.sin(ang)], axis=1)
    return cos.astype(np.float32), sin.astype(np.float32)


def _retention_consts(L):
    log_gamma = np.log1p(-np.exp2(-5.0 - np.arange(N_HEADS, dtype=np.float64)))
    idx = np.arange(L, dtype=np.float64)
    diff = np.maximum(idx[:, None] - idx[None, :], 0.0)
    dmat = np.where(idx[None, :] <= idx[:, None], np.exp(log_gamma[:, None, None] * diff), 0.0)
    xi = np.exp(log_gamma[:, None] * (idx + 1.0))
    zeta = np.exp(log_gamma[:, None] * (L - 1.0 - idx))
    xz = np.zeros((L, LANES), np.float64)
    xz[:, 0:N_HEADS] = xi.T
    xz[:, N_HEADS:2 * N_HEADS] = zeta.T
    return (jnp.asarray(dmat, F32), jnp.asarray(xz, F32), jnp.asarray(np.exp(log_gamma * L), F32),
            np.exp(log_gamma))


def _hi_lo(w):
    hi = w.astype(BF16)
    return hi, (w - hi.astype(F32)).astype(BF16)


def _pad_lanes(w):
    return jnp.pad(w, ((0, 0), (0, LANES - w.shape[1])))


def _prep_w_in(w):
    ml0 = 3 * GROUP_W
    mlg = ml0 + 4 * GROUP_W
    gd0 = mlg + 2 * N_HEADS
    gdg = gd0 + 4 * GROUP_W
    rt0 = gdg + 2 * N_HEADS
    main = jnp.concatenate([w[:, 0:ml0], w[:, ml0:mlg], w[:, gd0:gdg], w[:, rt0:]], axis=1).astype(BF16)
    gates = _pad_lanes(jnp.concatenate([w[:, mlg:gd0], w[:, gdg:rt0]], axis=1))
    return (main,) + _hi_lo(gates)


def _dispatch_plan(cnt):
    ntile = (cnt + TM_EXPERT - 1) // TM_EXPERT
    tile_end = jnp.cumsum(ntile)
    offs = (tile_end - ntile) * TM_EXPERT
    tid = jnp.arange(N_TILES, dtype=I32)
    te = jnp.minimum(jnp.searchsorted(tile_end, tid, side="right"), N_EXPERTS - 1).astype(I32)
    tv = (tid < tile_end[-1]).astype(I32)
    return offs.astype(I32), te, tv


def kernel(x_prompt, x_sample, cache_sb_k, cache_sb_v, state_mlstm_c, state_mlstm_n, state_mlstm_m, state_gdn_s, state_gdn_conv, state_ret_s, page_table, norm_attn_w, w_in, sb_logit_bias, mlstm_gate_bias, gdn_conv_w, gdn_a_log, gdn_dt_bias, head_norm_w, w_out, norm_ffn_w, router_group_w, router_group_b, router_expert_w, router_expert_b, expert_w_gate, expert_w_up, expert_w_down, final_norm_w):
    D = HEAD_DIM
    xp = x_prompt.reshape(N_PROMPT, D_MODEL)
    xd = x_sample.reshape(DEC_BATCH, D_MODEL)
    n_pool = cache_sb_k.shape[1]
    cache_kt = cache_sb_k.transpose(0, 1, 3, 4, 2).reshape(DEPTH, n_pool, GROUP_W, PAGE_SIZE)
    cache_vt = cache_sb_v.transpose(0, 1, 3, 4, 2).reshape(DEPTH, n_pool, GROUP_W, PAGE_SIZE)

    tril = _tri(CHUNK, "lower_incl")
    tri_sb = _tri(SB_BLOCK, "row_gt_col")
    tri_page = _tri(PAGE_SIZE, "row_gt_col")
    tri_rp = _tri(TM_PROMPT, "lower_strict")
    tri_rd = _tri(DEC_BATCH, "lower_strict")
    cos_p, sin_p = _rope_tables(np.arange(SEQ))
    cos_p = jnp.asarray(np.tile(cos_p, (1, N_HEADS)))
    sin_p = jnp.asarray(np.tile(sin_p, (1, N_HEADS)))
    cos_d, sin_d = _rope_tables([PAST_LEN])
    cos_d = jnp.asarray(np.tile(cos_d.T, (1, LANES)))
    sin_d = jnp.asarray(np.tile(sin_d.T, (1, LANES)))
    dmat, xz, gch, gamma = _retention_consts(CHUNK)
    zeros4 = jnp.zeros((N_HEADS,), F32)

    outs = {k: [] for k in ("kp", "vp", "ks", "vs", "cp", "np", "mp", "cs", "ns", "ms", "gp", "gcp", "gs", "gcs",
                            "rp", "rs")}
    yp = yd = None
    for l in range(DEPTH):
        w_main, wg_hi, wg_lo = _prep_w_in(w_in[l])
        nw = norm_attn_w[l][None, :]
        gate_bias_row = _pad_lanes(jnp.concatenate([mlstm_gate_bias[l], zeros4, gdn_dt_bias[l]])[None, :])
        a_log_row = _pad_lanes(jnp.concatenate([zeros4, zeros4, zeros4, gdn_a_log[l]])[None, :])
        bias = sb_logit_bias[l]

        sbq, sbk, sbv, ml, gd, rt, gt = _inproj_prompt(xp, nw, w_main, wg_hi, wg_lo)
        osb_p = _sb_prompt(sbq, sbk, sbv, bias, tri_sb)
        oml_p, c1p, n1p, m1p = _mlstm_prompt(ml, gt, gate_bias_row, tril)
        ogd_p, s1p, cv1p = _gdn_prompt(gd, gt, gate_bias_row, a_log_row, gdn_conv_w[l], tril)
        ort_p, r1p = _ret_prompt(rt, cos_p, sin_p, dmat, xz, gch)

        sbq_d, gdraw_d, pt = _inproj_decode(xd, nw, w_main, wg_hi, wg_lo)
        osb_d = _sb_decode(sbq_d, cache_kt, cache_vt, page_table, bias, tri_page, l)
        qkvt, cv1d = _gdn_decode_conv(gdraw_d, state_gdn_conv[l].transpose(1, 0, 2), gdn_conv_w[l])
        scalars = jnp.stack([mlstm_gate_bias[l][:N_HEADS], mlstm_gate_bias[l][N_HEADS:], gdn_dt_bias[l],
                             jnp.exp(gdn_a_log[l]), jnp.asarray(gamma, F32), zeros4, zeros4, zeros4])
        oml_d, ogd_d, ort_d, c1d, n1d, m1d, s1d, r1d = _decode_rec(
            scalars, pt, qkvt, cos_d, sin_d,
            state_mlstm_c[l].transpose(1, 2, 3, 0), state_mlstm_n[l].transpose(1, 2, 0),
            state_mlstm_m[l].T.reshape(N_HEADS, 1, DEC_BATCH),
            state_gdn_s[l].transpose(1, 2, 3, 0), state_ret_s[l].transpose(1, 2, 3, 0))

        gain = head_norm_w[l][None, :]
        wo = w_out[l].astype(BF16)
        nfw = norm_ffn_w[l][None, :]
        rw_hi, rw_lo = _hi_lo(_pad_lanes(jnp.concatenate([router_group_w[l], router_expert_w[l]], axis=1)))
        rb = _pad_lanes(jnp.concatenate([router_group_b[l], router_expert_b[l]])[None, :])
        x1p, h2p, lgp = _outproj(xp, osb_p, oml_p, ogd_p, ort_p, gain, wo, nfw, rw_hi, rw_lo, False)
        x1d, h2d, lgd = _outproj(xd, osb_d, oml_d, ogd_d, ort_d, gain, wo, nfw, rw_hi, rw_lo, True)
        rip, rfp, cnt_p = _route(lgp, rb, tri_rp, jnp.zeros((1, LANES), F32))
        rid, rfd, cnt = _route(lgd, rb, tri_rd, cnt_p)

        offs, te, tv = _dispatch_plan(cnt[0, :N_EXPERTS].astype(I32))
        dest = lambda ri: (offs[ri[:, 0]] + ri[:, 2], offs[ri[:, 1]] + ri[:, 3])
        d1p, d2p = dest(rip)
        d1d, d2d = dest(rid)
        xs = _dispatch(d1p, d2p, h2p, jnp.zeros((N_SLOTS, D_MODEL), F32))
        xs = _dispatch(d1d, d2d, h2d, xs)
        ys = _experts(te, tv, xs, expert_w_gate[l].astype(BF16), expert_w_up[l].astype(BF16),
                      expert_w_down[l].astype(BF16))
        final = l == DEPTH - 1
        fw = final_norm_w[None, :]
        xp_next = _combine(d1p, d2p, x1p, rfp, fw, ys, final)
        xd_next = _combine(d1d, d2d, x1d, rfd, fw, ys, final)
        if final:
            yp, yd = xp_next, xd_next
        else:
            xp, xd = xp_next, xd_next

        heads_p = lambda a: a.reshape(BATCH, SEQ, N_HEADS, D)
        heads_t = lambda a: a.reshape(N_HEADS, D, DEC_BATCH).transpose(2, 0, 1)[:, None]
        outs["kp"].append(heads_p(sbk))
        outs["vp"].append(heads_p(sbv))
        outs["ks"].append(heads_t(pt[GROUP_W:2 * GROUP_W]))
        outs["vs"].append(heads_t(pt[2 * GROUP_W:3 * GROUP_W]))
        outs["cp"].append(c1p)
        outs["np"].append(n1p)
        outs["mp"].append(m1p[:, 0, :N_HEADS])
        outs["cs"].append(c1d.transpose(3, 0, 1, 2))
        outs["ns"].append(n1d.transpose(2, 0, 1))
        outs["ms"].append(m1d[:, 0, :].T)
        outs["gp"].append(s1p)
        outs["gcp"].append(cv1p)
        outs["gs"].append(s1d.transpose(3, 0, 1, 2))
        outs["gcs"].append(cv1d.transpose(1, 0, 2))
        outs["rp"].append(r1p)
        outs["rs"].append(r1d.transpose(3, 0, 1, 2))

    st = lambda k: jnp.stack(outs[k], axis=0)
    return (yp.reshape(BATCH, SEQ, D_MODEL), yd.reshape(DEC_BATCH, 1, D_MODEL),
            st("kp"), st("vp"), st("ks"), st("vs"),
            st("cp"), st("np"), st("mp"), st("cs"), st("ns"), st("ms"),
            st("gp"), st("gcp"), st("gs"), st("gcs"), st("rp"), st("rs"))
```

```python
import functools
import math

import numpy as np
import jax
import jax.numpy as jnp
from jax import lax
from jax.experimental import pallas as pl
from jax.experimental.pallas import tpu as pltpu

F32 = jnp.float32
BF16 = jnp.bfloat16
I32 = jnp.int32

D_MODEL = 1024
BATCH = 8
SEQ = 2048
DEPTH = 2
DEC_BATCH = 128
PAST_LEN = 2048
PAGE_SIZE = 128
N_PAGES = PAST_LEN // PAGE_SIZE
HEAD_DIM = 64
N_HEADS = 4
GROUP_W = N_HEADS * HEAD_DIM
CHUNK = 64
CONV_W = 4
N_GROUPS = 4
EXPERTS_PER_GROUP = 8
N_EXPERTS = N_GROUPS * EXPERTS_PER_GROUP
D_EXPERT = D_MODEL // 2
ROPE_BASE = 10000.0
EPS = 1e-6
SCALE = HEAD_DIM ** -0.5

N_PROMPT = BATCH * SEQ
LANES = 128
N_MAIN = 3 * GROUP_W + 3 * 4 * GROUP_W
N_PROJ = N_MAIN + LANES
COL_ML = 3 * GROUP_W
COL_GD = COL_ML + 4 * GROUP_W
COL_RT = COL_GD + 4 * GROUP_W
COL_GATES = N_MAIN

STEP_ROWS = 4 * CHUNK
SB_BLOCK = 256
TM_PROMPT = 256
TM_EXPERT = 256
N_ASSIGN = 2 * (N_PROMPT + DEC_BATCH)
N_TILES = -(-N_ASSIGN // TM_EXPERT) + N_EXPERTS
N_SLOTS = N_TILES * TM_EXPERT
VMEM_LIMIT = 48 * 1024 * 1024

_NT = (((1,), (1,)), ((), ()))
_TN = (((0,), (0,)), ((), ()))


_NN = (((1,), (0,)), ((), ()))


def _mm(a, b, dims):
    return lax.dot_general(a.astype(BF16), b.astype(BF16), dims, preferred_element_type=F32)


def _dot(a, b):
    return _mm(a, b, _NN)


def _dot_nt(a, b):
    return _mm(a, b, _NT)


def _dot_tn(a, b):
    return _mm(a, b, _TN)


def _split3(x):
    x1 = x.astype(BF16)
    r1 = x - x1.astype(F32)
    x2 = r1.astype(BF16)
    x3 = (r1 - x2.astype(F32)).astype(BF16)
    return x1, x2, x3


def _tri_dot(tri, x):
    x1, x2, x3 = _split3(x)
    return _dot(tri, x1) + _dot(tri, x2) + _dot(tri, x3)


def _log_sigmoid(z):
    return jnp.minimum(z, 0.0) - jnp.log(1.0 + jnp.exp(-jnp.abs(z)))


def _softplus(z):
    return jnp.maximum(z, 0.0) + jnp.log(1.0 + jnp.exp(-jnp.abs(z)))


def _sigmoid(z):
    return 1.0 / (1.0 + jnp.exp(-z))


def _head_rms(x):
    return x * lax.rsqrt(jnp.mean(x * x, axis=-1, keepdims=True) + EPS)


def _cparams(sem):
    return pltpu.CompilerParams(dimension_semantics=sem, vmem_limit_bytes=VMEM_LIMIT)


def _const_spec(shape):
    nd = len(shape)
    return pl.BlockSpec(shape, lambda *_: (0,) * nd)


def _norm_rows(x, w):
    return x * lax.rsqrt(jnp.mean(x * x, axis=-1, keepdims=True) + EPS) * w


def _gates_dot(h, wgh_ref, wgl_ref):
    hb = h.astype(BF16)
    hl = (h - hb.astype(F32)).astype(BF16)
    return _dot(hb, wgh_ref[...]) + _dot(hl, wgh_ref[...]) + _dot(hb, wgl_ref[...])


def _inproj_prompt_kernel(x_ref, nw_ref, w_ref, wgh_ref, wgl_ref,
                          sbq_ref, sbk_ref, sbv_ref, ml_ref, gd_ref, rt_ref, gt_ref):
    h = _norm_rows(x_ref[...], nw_ref[...])
    hb = h.astype(BF16)
    seg = lambda a, b: _dot(hb, w_ref[:, a:b])
    sbq_ref[...] = seg(0, GROUP_W)
    sbk_ref[...] = seg(GROUP_W, 2 * GROUP_W)
    sbv_ref[...] = seg(2 * GROUP_W, 3 * GROUP_W)
    ml_ref[...] = seg(COL_ML, COL_GD)
    gd_ref[...] = seg(COL_GD, COL_RT)
    rt_ref[...] = seg(COL_RT, N_MAIN)
    gt_ref[...] = _gates_dot(h, wgh_ref, wgl_ref)


def _inproj_prompt(x, nw, w, wgh, wgl):
    n = x.shape[0]
    tm = TM_PROMPT
    row = lambda c: pl.BlockSpec((tm, c), lambda i: (i, 0))
    S = jax.ShapeDtypeStruct
    return pl.pallas_call(
        _inproj_prompt_kernel,
        grid=(n // tm,),
        in_specs=[row(D_MODEL), _const_spec((1, D_MODEL)), _const_spec((D_MODEL, N_MAIN)),
                  _const_spec((D_MODEL, LANES)), _const_spec((D_MODEL, LANES))],
        out_specs=[row(GROUP_W), row(GROUP_W), row(GROUP_W), row(4 * GROUP_W), row(4 * GROUP_W),
                   row(4 * GROUP_W), row(LANES)],
        out_shape=[S((n, GROUP_W), F32)] * 3 + [S((n, 4 * GROUP_W), F32)] * 3 + [S((n, LANES), F32)],
        compiler_params=_cparams(("parallel",)),
        name="inproj_prompt",
    )(x, nw, w, wgh, wgl)


def _inproj_decode_kernel(x_ref, nw_ref, w_ref, wgh_ref, wgl_ref, sbq_ref, gdraw_ref, pt_ref):
    h = _norm_rows(x_ref[...], nw_ref[...])
    hb = h.astype(BF16)
    for j in range(N_MAIN // LANES):
        p = _dot(hb, w_ref[:, j * LANES:(j + 1) * LANES])
        if j < GROUP_W // LANES:
            sbq_ref[:, j * LANES:(j + 1) * LANES] = p
        c0 = j * LANES - COL_GD
        if 0 <= c0 < 3 * GROUP_W:
            gdraw_ref[:, c0:c0 + LANES] = p
        pt_ref[j * LANES:(j + 1) * LANES, :] = p.T
    pt_ref[N_MAIN:N_PROJ, :] = _gates_dot(h, wgh_ref, wgl_ref).T


def _inproj_decode(x, nw, w, wgh, wgl):
    S = jax.ShapeDtypeStruct
    return pl.pallas_call(
        _inproj_decode_kernel,
        out_shape=[S((DEC_BATCH, GROUP_W), F32), S((DEC_BATCH, 3 * GROUP_W), F32), S((N_PROJ, DEC_BATCH), F32)],
        compiler_params=_cparams(None),
        name="inproj_decode",
    )(x, nw, w, wgh, wgl)


def _sb_prompt_kernel(bias_ref, q_ref, k_ref, v_ref, tri_ref, o_ref, acc_s, carry_s):
    tb = SB_BLOCK
    qi = pl.program_id(1)
    tri = tri_ref[...]
    r = lax.broadcasted_iota(I32, (tb, tb), 0)
    c = lax.broadcasted_iota(I32, (tb, tb), 1)
    dmask = c < r
    acc_s[...] = jnp.zeros_like(acc_s)
    carry_s[...] = jnp.zeros_like(carry_s)
    q = q_ref[...] * SCALE

    def block(j, mask):
        start = pl.multiple_of(j * tb, tb)
        kj = k_ref[pl.ds(start, tb), :]
        vj = v_ref[pl.ds(start, tb), :]
        heads = range(N_HEADS)
        sls = [slice(h * HEAD_DIM, (h + 1) * HEAD_DIM) for h in heads]
        z = [_dot_nt(q[:, sls[h]], kj[:, sls[h]]) + bias_ref[h] for h in heads]
        lsz = [_log_sigmoid(z[h]) for h in heads]
        lk = [lsz[h] - z[h] for h in heads]
        if mask is not None:
            lk = [jnp.where(mask, x, 0.0) for x in lk]
        lw_all = _dot(jnp.concatenate([x.astype(BF16) for x in lk], axis=0), tri)
        lw = [lw_all[h * tb:(h + 1) * tb, :] for h in heads]
        a = [jnp.exp(lsz[h] + lw[h] + carry_s[:, h:h + 1]) for h in heads]
        if mask is not None:
            a = [jnp.where(mask, x, 0.0) for x in a]
        av = [_dot(a[h], vj[:, sls[h]]) for h in heads]
        for h in heads:
            acc_s[:, sls[h]] += av[h]
            carry_s[:, h:h + 1] += lw[h][:, 0:1] + lk[h][:, 0:1]

    block(qi, dmask)

    def body(it, _):
        block(qi - 1 - it, None)
        return 0

    lax.fori_loop(0, qi, body, 0)
    for h in range(N_HEADS):
        sl = slice(h * HEAD_DIM, (h + 1) * HEAD_DIM)
        o_ref[:, sl] = _head_rms(acc_s[:, sl])


def _sb_prompt(q, k, v, bias, tri):
    tb = SB_BLOCK
    nq = SEQ // tb
    return pl.pallas_call(
        _sb_prompt_kernel,
        grid_spec=pltpu.PrefetchScalarGridSpec(
            num_scalar_prefetch=0,
            grid=(BATCH, nq),
            in_specs=[pl.BlockSpec(memory_space=pltpu.SMEM),
                      pl.BlockSpec((tb, GROUP_W), lambda b, i: (b * nq + i, 0)),
                      pl.BlockSpec((SEQ, GROUP_W), lambda b, i: (b, 0)),
                      pl.BlockSpec((SEQ, GROUP_W), lambda b, i: (b, 0)),
                      _const_spec((tb, tb))],
            out_specs=pl.BlockSpec((tb, GROUP_W), lambda b, i: (b * nq + i, 0)),
            scratch_shapes=[pltpu.VMEM((tb, GROUP_W), F32), pltpu.VMEM((tb, LANES), F32)]),
        out_shape=jax.ShapeDtypeStruct((N_PROMPT, GROUP_W), F32),
        compiler_params=_cparams(("parallel", "parallel")),
        name="sb_prompt",
    )(bias, q, k, v, tri)


def _sb_decode_kernel(pt_ref, bias_ref, q_ref, *rest):
    k_refs = rest[:N_PAGES]
    v_refs = rest[N_PAGES:2 * N_PAGES]
    tri_ref, pg_ref, o_ref = rest[2 * N_PAGES:]
    n_rows = 8 * N_PAGES
    b = pl.program_id(0)
    r = b % 8
    qrow = q_ref[pl.ds(r, 1), :] * SCALE
    rowi = lax.broadcasted_iota(I32, (8, GROUP_W), 0)
    lanei = lax.broadcasted_iota(I32, (8, GROUP_W), 1)
    head_of_lane = jnp.right_shift(lanei, 6)
    qbd = jnp.where(head_of_lane == rowi, qrow, 0.0).astype(BF16)
    row8 = lax.broadcasted_iota(I32, (8, 1), 0)
    bias = jnp.zeros((8, 1), F32)
    for h in range(N_HEADS):
        bias = jnp.where(row8 == h, bias_ref[h], bias)
    z = jnp.concatenate([_dot(qbd, k_refs[j][0, 0]) + bias for j in range(N_PAGES)], axis=0)
    lsz = _log_sigmoid(z)
    lk = lsz - z
    lw = _dot(lk, tri_ref[...])
    tot = jnp.broadcast_to(lw[:, 0:1] + lk[:, 0:1], (n_rows, LANES))
    later_pages = _tri_dot(pg_ref[...], tot)
    a = jnp.exp(lsz + lw + later_pages)
    acc = jnp.zeros((8, GROUP_W), F32)
    for j in range(N_PAGES):
        acc = acc + _dot_nt(a[j * 8:(j + 1) * 8, :], v_refs[j][0, 0])
    own = jnp.where(head_of_lane == rowi, acc, 0.0)
    orow = jnp.sum(own, axis=0, keepdims=True)
    pieces = [_head_rms(orow[:, h * HEAD_DIM:(h + 1) * HEAD_DIM]) for h in range(N_HEADS)]
    o_ref[pl.ds(r, 1), :] = jnp.concatenate(pieces, axis=1)


def _sb_decode(q, cache_kt, cache_vt, page_table, bias, tri, later_pages, layer):
    def page_spec(j):
        return pl.BlockSpec((1, 1, GROUP_W, PAGE_SIZE), lambda b, pt: (layer, pt[b, j], 0, 0))

    in_specs = ([pl.BlockSpec(memory_space=pltpu.SMEM),
                 pl.BlockSpec((8, GROUP_W), lambda b, pt: (b // 8, 0))]
                + [page_spec(j) for j in range(N_PAGES)] * 2
                + [pl.BlockSpec((PAGE_SIZE, PAGE_SIZE), lambda b, pt: (0, 0)),
                   pl.BlockSpec((8 * N_PAGES, 8 * N_PAGES), lambda b, pt: (0, 0))])
    return pl.pallas_call(
        _sb_decode_kernel,
        grid_spec=pltpu.PrefetchScalarGridSpec(
            num_scalar_prefetch=1,
            grid=(DEC_BATCH,),
            in_specs=in_specs,
            out_specs=pl.BlockSpec((8, GROUP_W), lambda b, pt: (b // 8, 0))),
        out_shape=jax.ShapeDtypeStruct((DEC_BATCH, GROUP_W), F32),
        compiler_params=_cparams(("arbitrary",)),
        name="sb_decode",
    )(page_table, bias, q, *([cache_kt] * N_PAGES), *([cache_vt] * N_PAGES), tri, later_pages)


def _chunk_masks(L):
    r = lax.broadcasted_iota(I32, (L, L), 0)
    c = lax.broadcasted_iota(I32, (L, L), 1)
    return c <= r, c < r


def _head_cols(blk, h):
    return slice(blk * GROUP_W + h * HEAD_DIM, blk * GROUP_W + (h + 1) * HEAD_DIM)


def _mlstm_prompt_kernel(x_ref, g_ref, gb_ref, tril_ref, o_ref, c1_ref, n1_ref, m1_ref, c_s, n_s, m_s):
    L = CHUNK
    R = STEP_ROWS
    ci = pl.program_id(1)

    @pl.when(ci == 0)
    def _():
        c_s[...] = jnp.zeros_like(c_s)
        n_s[...] = jnp.zeros_like(n_s)
        m_s[...] = jnp.zeros_like(m_s)

    g = g_ref[...] + gb_ref[...]
    lane = lax.broadcasted_iota(I32, (R, LANES), 1)
    lf = jnp.where((lane >= N_HEADS) & (lane < 2 * N_HEADS), _log_sigmoid(g), 0.0)
    cum = _tri_dot(tril_ref[...], lf)
    xt = jnp.where(lane < N_HEADS, g, cum).T
    causal, _ = _chunk_masks(L)
    P = [(s_i, h) for s_i in range(R // L) for h in range(N_HEADS)]
    rows = [slice(s_i * L, (s_i + 1) * L) for s_i, _ in P]
    hs = [h for _, h in P]
    np_ = range(len(P))
    bc = [cum[rows[p], N_HEADS + hs[p]:N_HEADS + hs[p] + 1] for p in np_]
    li = [g[rows[p], hs[p]:hs[p] + 1] for p in np_]
    log_d = [jnp.where(causal, bc[p] - xt[N_HEADS + hs[p]:N_HEADS + hs[p] + 1, rows[p]] + xt[hs[p]:hs[p] + 1, rows[p]],
                       -jnp.inf) for p in np_]
    b_last = [bc[p][L - 1:L, :] for p in np_]
    log_w = [b_last[p] - bc[p] + li[p] for p in np_]
    d_max = [jnp.max(log_d[p], axis=1, keepdims=True) for p in np_]
    w_max = [jnp.max(log_w[p], axis=0, keepdims=True) for p in np_]
    m = [m_s[0:1, h:h + 1] for h in range(N_HEADS)]
    m_in, m_out = [], []
    for p in np_:
        m_in.append(m[hs[p]])
        m[hs[p]] = jnp.maximum(b_last[p] + m[hs[p]], w_max[p])
        m_out.append(m[hs[p]])
    log_inter = [bc[p] + m_in[p] for p in np_]
    m_row = [jnp.maximum(log_inter[p], d_max[p]) for p in np_]
    dexp = [jnp.exp(log_d[p] - m_row[p]) for p in np_]
    w_inter = [jnp.exp(log_inter[p] - m_row[p]) for p in np_]
    decay = [jnp.exp(b_last[p] + m_in[p] - m_out[p]) for p in np_]
    q = [x_ref[rows[p], _head_cols(0, hs[p])] for p in np_]
    k = [x_ref[rows[p], _head_cols(1, hs[p])] * SCALE for p in np_]
    v = [x_ref[rows[p], _head_cols(2, hs[p])] for p in np_]
    kw = [k[p] * jnp.exp(log_w[p] - m_out[p]) for p in np_]
    s_mat = [_dot_nt(q[p], k[p]) * dexp[p] for p in np_]
    sv = [_dot(s_mat[p], v[p]) for p in np_]
    kv = [_dot_tn(kw[p], v[p]) for p in np_]
    kw_sum = [jnp.sum(kw[p], axis=0, keepdims=True) for p in np_]
    c = [c_s[h] for h in range(N_HEADS)]
    n = [n_s[h:h + 1, :] for h in range(N_HEADS)]
    c_in, n_in = [], []
    for p in np_:
        c_in.append(c[hs[p]])
        n_in.append(n[hs[p]])
        c[hs[p]] = decay[p] * c[hs[p]] + kv[p]
        n[hs[p]] = decay[p] * n[hs[p]] + kw_sum[p]
    qc = [_dot(q[p], c_in[p]) for p in np_]
    s_sum = [jnp.sum(s_mat[p], axis=1, keepdims=True) for p in np_]
    qn = [jnp.sum(q[p] * n_in[p], axis=1, keepdims=True) for p in np_]
    hh = [(sv[p] + w_inter[p] * qc[p]) / jnp.maximum(jnp.abs(s_sum[p] + w_inter[p] * qn[p]), jnp.exp(-m_row[p]))
          for p in np_]
    ms = [jnp.mean(hh[p] * hh[p], axis=1, keepdims=True) for p in np_]
    for p in np_:
        og = x_ref[rows[p], _head_cols(3, hs[p])]
        o_ref[rows[p], hs[p] * HEAD_DIM:(hs[p] + 1) * HEAD_DIM] = hh[p] * lax.rsqrt(ms[p] + EPS) * _sigmoid(og)
    for h in range(N_HEADS):
        c_s[h] = c[h]
        n_s[h:h + 1, :] = n[h]
        m_s[0:1, h:h + 1] = m[h]

    @pl.when(ci == pl.num_programs(1) - 1)
    def _():
        c1_ref[0] = c_s[...]
        n1_ref[0] = n_s[...]
        m1_ref[0] = m_s[...]


def _mlstm_prompt(ml, gates, gate_bias_row, tril):
    L = STEP_ROWS
    nc = SEQ // L
    S = jax.ShapeDtypeStruct
    return pl.pallas_call(
        _mlstm_prompt_kernel,
        grid=(BATCH, nc),
        in_specs=[pl.BlockSpec((L, 4 * GROUP_W), lambda b, c: (b * nc + c, 0)),
                  pl.BlockSpec((L, LANES), lambda b, c: (b * nc + c, 0)),
                  _const_spec((1, LANES)), _const_spec((L, L))],
        out_specs=[pl.BlockSpec((L, GROUP_W), lambda b, c: (b * nc + c, 0)),
                   pl.BlockSpec((1, N_HEADS, HEAD_DIM, HEAD_DIM), lambda b, c: (b, 0, 0, 0)),
                   pl.BlockSpec((1, N_HEADS, HEAD_DIM), lambda b, c: (b, 0, 0)),
                   pl.BlockSpec((1, 1, LANES), lambda b, c: (b, 0, 0))],
        out_shape=[S((N_PROMPT, GROUP_W), F32), S((BATCH, N_HEADS, HEAD_DIM, HEAD_DIM), F32),
                   S((BATCH, N_HEADS, HEAD_DIM), F32), S((BATCH, 1, LANES), F32)],
        scratch_shapes=[pltpu.VMEM((N_HEADS, HEAD_DIM, HEAD_DIM), F32), pltpu.VMEM((N_HEADS, HEAD_DIM), F32),
                        pltpu.VMEM((1, LANES), F32)],
        compiler_params=_cparams(("parallel", "arbitrary")),
        name="mlstm_prompt",
    )(ml, gates, gate_bias_row, tril)


def _unit_lower_inverse(a, L):
    r = lax.broadcasted_iota(I32, (L, L), 0)
    c = lax.broadcasted_iota(I32, (L, L), 1)
    p = jnp.where(r == c, 1.0, 0.0) - a
    x = a
    power = 1
    while 2 * power < L:
        x = _dot(x, x)
        p = p + _dot(p, x)
        power *= 2
    return p


def _gdn_prompt_kernel(x_ref, g_ref, gb_ref, al_ref, cw_ref, tril_ref, o_ref, s1_ref, cv_ref, s_s, xe_s):
    L = CHUNK
    R = STEP_ROWS
    nq = 3 * GROUP_W
    ci = pl.program_id(1)

    @pl.when(ci == 0)
    def _():
        s_s[...] = jnp.zeros_like(s_s)
        xe_s[0:8, :] = jnp.zeros((8, nq), F32)

    raw = x_ref[:, 0:nq]
    xe_s[8:8 + R, :] = raw
    conv = (cw_ref[3:4, :] * raw + cw_ref[2:3, :] * xe_s[7:7 + R, :]
            + cw_ref[1:2, :] * xe_s[6:6 + R, :] + cw_ref[0:1, :] * xe_s[5:5 + R, :])
    xe_s[0:8, :] = raw[R - 8:R, :]
    qkv = conv * _sigmoid(conv)

    g = g_ref[...] + gb_ref[...]
    lane = lax.broadcasted_iota(I32, (R, LANES), 1)
    beta_all = _sigmoid(g)
    gd = jnp.where((lane >= 3 * N_HEADS) & (lane < 4 * N_HEADS), -jnp.exp(al_ref[...]) * _softplus(g), 0.0)
    gcum = _tri_dot(tril_ref[...], gd)
    xt = gcum.T
    incl, strict = _chunk_masks(L)
    prob = [(s_i, h) for s_i in range(R // L) for h in range(N_HEADS)]
    loc = []
    for s_i, h in prob:
        rows = slice(s_i * L, (s_i + 1) * L)
        q = qkv[rows, _head_cols(0, h)]
        k = qkv[rows, _head_cols(1, h)]
        v = qkv[rows, _head_cols(2, h)]
        q = q * lax.rsqrt(jnp.sum(q * q, axis=1, keepdims=True) + EPS) * SCALE
        k = k * lax.rsqrt(jnp.sum(k * k, axis=1, keepdims=True) + EPS)
        b = beta_all[rows, 2 * N_HEADS + h:2 * N_HEADS + h + 1]
        gc = gcum[rows, 3 * N_HEADS + h:3 * N_HEADS + h + 1]
        gc_row = xt[3 * N_HEADS + h:3 * N_HEADS + h + 1, rows]
        decay = jnp.exp(jnp.where(incl, gc - gc_row, -jnp.inf))
        kb = k * b
        egc = jnp.exp(gc)
        gl = gc[L - 1:L, :]
        loc.append(dict(q=q, k=k, kb=kb, decay=decay, vb=v * b, kbe=kb * egc, q_dec=q * egc,
                        k_dec=k * jnp.exp(gl - gc), g_last=jnp.exp(gl)))
    a_mat = [jnp.where(strict, _dot_nt(d["kb"], d["k"]) * d["decay"], 0.0) for d in loc]
    qk_mat = [_dot_nt(d["q"], d["k"]) * d["decay"] for d in loc]
    eye = jnp.where(incl & jnp.logical_not(strict), 1.0, 0.0)
    pw = a_mat
    inv = [eye - a for a in a_mat]
    power = 1
    while 2 * power < L:
        pw = [_dot(x, x) for x in pw]
        inv = [p + _dot(p, x) for p, x in zip(inv, pw)]
        power *= 2
    us = [_dot(t, d["vb"]) for t, d in zip(inv, loc)]
    ws = [_dot(t, d["kbe"]) for t, d in zip(inv, loc)]
    s = [s_s[h] for h in range(N_HEADS)]
    heads = range(N_HEADS)
    for s_i in range(R // L):
        rows = slice(s_i * L, (s_i + 1) * L)
        p0 = s_i * N_HEADS
        w_s = [_dot(ws[p0 + h], s[h]) for h in heads]
        q_s = [_dot(loc[p0 + h]["q_dec"], s[h]) for h in heads]
        v_new = [us[p0 + h] - w_s[h] for h in heads]
        o = [q_s[h] + _dot(qk_mat[p0 + h], v_new[h]) for h in heads]
        s = [s[h] * loc[p0 + h]["g_last"] + _dot_tn(loc[p0 + h]["k_dec"], v_new[h]) for h in heads]
        for h in heads:
            gate = x_ref[rows, _head_cols(3, h)]
            o_ref[rows, h * HEAD_DIM:(h + 1) * HEAD_DIM] = _head_rms(o[h]) * (gate * _sigmoid(gate))
    for h in range(N_HEADS):
        s_s[h] = s[h]

    @pl.when(ci == pl.num_programs(1) - 1)
    def _():
        s1_ref[0] = s_s[...]
        cv_ref[0] = xe_s[8 - (CONV_W - 1):8, :]


def _gdn_prompt(gd, gates, gate_bias_row, a_log_row, conv_w, tril):
    L = STEP_ROWS
    nc = SEQ // L
    S = jax.ShapeDtypeStruct
    return pl.pallas_call(
        _gdn_prompt_kernel,
        grid=(BATCH, nc),
        in_specs=[pl.BlockSpec((L, 4 * GROUP_W), lambda b, c: (b * nc + c, 0)),
                  pl.BlockSpec((L, LANES), lambda b, c: (b * nc + c, 0)),
                  _const_spec((1, LANES)), _const_spec((1, LANES)), _const_spec((CONV_W, 3 * GROUP_W)),
                  _const_spec((L, L))],
        out_specs=[pl.BlockSpec((L, GROUP_W), lambda b, c: (b * nc + c, 0)),
                   pl.BlockSpec((1, N_HEADS, HEAD_DIM, HEAD_DIM), lambda b, c: (b, 0, 0, 0)),
                   pl.BlockSpec((1, CONV_W - 1, 3 * GROUP_W), lambda b, c: (b, 0, 0))],
        out_shape=[S((N_PROMPT, GROUP_W), F32), S((BATCH, N_HEADS, HEAD_DIM, HEAD_DIM), F32),
                   S((BATCH, CONV_W - 1, 3 * GROUP_W), F32)],
        scratch_shapes=[pltpu.VMEM((N_HEADS, HEAD_DIM, HEAD_DIM), F32), pltpu.VMEM((8 + L, 3 * GROUP_W), F32)],
        compiler_params=_cparams(("parallel", "arbitrary")),
        name="gdn_prompt",
    )(gd, gates, gate_bias_row, a_log_row, conv_w, tril)


def _rope_rows(x, cos, sin_signed):
    lane = lax.broadcasted_iota(I32, x.shape, 1)
    first = jnp.bitwise_and(lane, HEAD_DIM - 1) < HEAD_DIM // 2
    w = x.shape[1]
    swapped = jnp.where(first, pltpu.roll(x, w - HEAD_DIM // 2, 1), pltpu.roll(x, HEAD_DIM // 2, 1))
    return x * cos + swapped * sin_signed


def _ret_prompt_kernel(x_ref, cos_ref, sin_ref, dm_ref, xz_ref, gch_ref, o_ref, s1_ref, s_s):
    ci = pl.program_id(1)

    @pl.when(ci == 0)
    def _():
        s_s[...] = jnp.zeros_like(s_s)

    L = CHUNK
    cos = cos_ref[...]
    sin = sin_ref[...]
    qr = _rope_rows(x_ref[:, 0:GROUP_W], cos, sin)
    kr = _rope_rows(x_ref[:, GROUP_W:2 * GROUP_W], cos, sin) * SCALE
    prob = []
    for s_i in range(STEP_ROWS // L):
        rows = slice(s_i * L, (s_i + 1) * L)
        for h in range(N_HEADS):
            sl = slice(h * HEAD_DIM, (h + 1) * HEAD_DIM)
            prob.append(dict(rows=rows, h=h, sl=sl, q=qr[rows, sl], k=kr[rows, sl], v=x_ref[rows, _head_cols(2, h)]))
    qk = [_dot_nt(p["q"], p["k"]) * dm_ref[p["h"]] for p in prob]
    intra = [_dot(a, p["v"]) for a, p in zip(qk, prob)]
    kv = [_dot_tn(p["k"] * xz_ref[:, N_HEADS + p["h"]:N_HEADS + p["h"] + 1], p["v"]) for p in prob]
    s = [s_s[h] for h in range(N_HEADS)]
    s_in = []
    for p, kv_p in zip(prob, kv):
        s_in.append(s[p["h"]])
        s[p["h"]] = s[p["h"]] * gch_ref[p["h"]] + kv_p
    inter = [_dot(p["q"], s0) * xz_ref[:, p["h"]:p["h"] + 1] for p, s0 in zip(prob, s_in)]
    for p, a, b in zip(prob, intra, inter):
        gate = x_ref[p["rows"], _head_cols(3, p["h"])]
        o_ref[p["rows"], p["sl"]] = _head_rms(a + b) * (gate * _sigmoid(gate))
    for h in range(N_HEADS):
        s_s[h] = s[h]

    @pl.when(ci == pl.num_programs(1) - 1)
    def _():
        s1_ref[0] = s_s[...]


def _ret_prompt(rt, cos, sin, dmat, xz, gch):
    L = STEP_ROWS
    nc = SEQ // L
    S = jax.ShapeDtypeStruct
    return pl.pallas_call(
        _ret_prompt_kernel,
        grid=(BATCH, nc),
        in_specs=[pl.BlockSpec((L, 4 * GROUP_W), lambda b, c: (b * nc + c, 0)),
                  pl.BlockSpec((L, GROUP_W), lambda b, c: (c, 0)),
                  pl.BlockSpec((L, GROUP_W), lambda b, c: (c, 0)),
                  _const_spec((N_HEADS, CHUNK, CHUNK)), _const_spec((CHUNK, LANES)),
                  pl.BlockSpec(memory_space=pltpu.SMEM)],
        out_specs=[pl.BlockSpec((L, GROUP_W), lambda b, c: (b * nc + c, 0)),
                   pl.BlockSpec((1, N_HEADS, HEAD_DIM, HEAD_DIM), lambda b, c: (b, 0, 0, 0))],
        out_shape=[S((N_PROMPT, GROUP_W), F32), S((BATCH, N_HEADS, HEAD_DIM, HEAD_DIM), F32)],
        scratch_shapes=[pltpu.VMEM((N_HEADS, HEAD_DIM, HEAD_DIM), F32)],
        compiler_params=_cparams(("parallel", "arbitrary")),
        name="ret_prompt",
    )(rt, cos, sin, dmat, xz, gch)


def _gdn_decode_conv_kernel(raw_ref, c0_ref, cw_ref, qkvt_ref, cv_ref):
    raw = raw_ref[...]
    conv = (cw_ref[3:4, :] * raw + cw_ref[2:3, :] * c0_ref[2] + cw_ref[1:2, :] * c0_ref[1]
            + cw_ref[0:1, :] * c0_ref[0])
    qkv = conv * _sigmoid(conv)
    cv_ref[0] = c0_ref[1]
    cv_ref[1] = c0_ref[2]
    cv_ref[2] = raw
    for blk in range(3):
        for h in range(N_HEADS):
            lo = blk * GROUP_W + h * HEAD_DIM
            x = qkv[:, lo:lo + HEAD_DIM]
            if blk == 0:
                x = x * lax.rsqrt(jnp.sum(x * x, axis=1, keepdims=True) + EPS) * SCALE
            elif blk == 1:
                x = x * lax.rsqrt(jnp.sum(x * x, axis=1, keepdims=True) + EPS)
            if h % 2 == 0:
                pair = x
            else:
                qkvt_ref[lo - HEAD_DIM:lo + HEAD_DIM, :] = jnp.concatenate([pair, x], axis=1).T


def _gdn_decode_conv(raw, conv0, conv_w):
    S = jax.ShapeDtypeStruct
    return pl.pallas_call(
        _gdn_decode_conv_kernel,
        out_shape=[S((3 * GROUP_W, DEC_BATCH), F32), S((CONV_W - 1, DEC_BATCH, 3 * GROUP_W), F32)],
        compiler_params=_cparams(None),
        name="gdn_decode_conv",
    )(raw, conv0, conv_w)


def _decode_rec_kernel(sc_ref, mlq_ref, mlk_ref, mlv_ref, mlo_ref, gq_ref, gk_ref, gv_ref, gg_ref,
                       rq_ref, rk_ref, rv_ref, rg_ref, gt_ref, cos_ref, sin_ref,
                       c0_ref, n0_ref, m0_ref, sg0_ref, sr0_ref,
                       oml_ref, ogd_ref, ort_ref, c1_ref, n1_ref, m1_ref, sg1_ref, sr1_ref, va_s, vb_s):
    h = pl.program_id(0)
    D = HEAD_DIM
    sum0 = lambda x: jnp.sum(x, axis=0, keepdims=True)
    rms0 = lambda x: x * lax.rsqrt(sum0(x * x) * (1.0 / D) + EPS)

    li = gt_ref[pl.ds(h, 1), :] + sc_ref[0, h]
    lf = _log_sigmoid(gt_ref[pl.ds(N_HEADS + h, 1), :] + sc_ref[1, h])
    m0 = m0_ref[0]
    q = mlq_ref[...]
    k = mlk_ref[...] * SCALE
    v = mlv_ref[...]
    log_inter = lf + m0
    m_row = jnp.maximum(log_inter, li)
    s = sum0(q * k) * jnp.exp(li - m_row)
    w_inter = jnp.exp(log_inter - m_row)
    decay = jnp.exp(lf + m0 - m_row)
    kw = k * jnp.exp(li - m_row)

    va_s[...] = kw

    def ml_body(d, qc):
        c_d = c0_ref[0, d]
        c1_ref[0, d] = decay * c_d + va_s[pl.ds(d, 1), :] * v
        return qc + mlq_ref[pl.ds(d, 1), :] * c_d

    n0 = n0_ref[0]
    n1_ref[0] = decay * n0 + kw
    qc = lax.fori_loop(0, D, ml_body, jnp.zeros((D, LANES), F32))
    num = s * v + w_inter * qc
    den = s + w_inter * sum0(q * n0)
    hh = num / jnp.maximum(jnp.abs(den), jnp.exp(-m_row))
    m1_ref[0] = m_row
    oml_ref[...] = rms0(hh) * _sigmoid(mlo_ref[...])

    beta = _sigmoid(gt_ref[pl.ds(2 * N_HEADS + h, 1), :])
    gdec = -sc_ref[3, h] * _softplus(gt_ref[pl.ds(3 * N_HEADS + h, 1), :] + sc_ref[2, h])
    eg = jnp.exp(gdec)
    gv = gv_ref[...]

    def ks_body(d, acc):
        return acc + gk_ref[pl.ds(d, 1), :] * sg0_ref[0, d]

    ks = lax.fori_loop(0, D, ks_body, jnp.zeros((D, LANES), F32))
    v_new = beta * gv - (beta * eg) * ks

    def gd_body(d, acc):
        s_new = eg * sg0_ref[0, d] + gk_ref[pl.ds(d, 1), :] * v_new
        sg1_ref[0, d] = s_new
        return acc + gq_ref[pl.ds(d, 1), :] * s_new

    og = lax.fori_loop(0, D, gd_body, jnp.zeros((D, LANES), F32))
    gate = gg_ref[...]
    ogd_ref[...] = rms0(og) * (gate * _sigmoid(gate))

    half = D // 2

    def rope(ref):
        x = ref[...]
        sw = jnp.concatenate([x[half:, :], x[:half, :]], axis=0)
        return x * cos_ref[...] + sw * sin_ref[...]

    va_s[...] = rope(rq_ref)
    vb_s[...] = rope(rk_ref) * SCALE
    rv = rv_ref[...]
    gamma = sc_ref[4, h]

    def rt_body(d, acc):
        s_new = gamma * sr0_ref[0, d] + vb_s[pl.ds(d, 1), :] * rv
        sr1_ref[0, d] = s_new
        return acc + va_s[pl.ds(d, 1), :] * s_new

    ort = lax.fori_loop(0, D, rt_body, jnp.zeros((D, LANES), F32))
    gate = rg_ref[...]
    ort_ref[...] = rms0(ort) * (gate * _sigmoid(gate))


def _decode_rec(scalars, pt, qkvt, cos_t, sin_t, c0, n0, m0, sg0, sr0):
    D = HEAD_DIM
    S = jax.ShapeDtypeStruct
    prow = lambda col, blk: pl.BlockSpec((D, LANES), lambda h: ((col + blk * GROUP_W) // D + h, 0))
    vec = lambda blk: pl.BlockSpec((D, LANES), lambda h: (blk * N_HEADS + h, 0))
    st4 = pl.BlockSpec((1, D, D, LANES), lambda h: (h, 0, 0, 0))
    st3 = pl.BlockSpec((1, D, LANES), lambda h: (h, 0, 0))
    st2 = pl.BlockSpec((1, 1, LANES), lambda h: (h, 0, 0))
    in_specs = ([pl.BlockSpec(memory_space=pltpu.SMEM)]
                + [prow(COL_ML, blk) for blk in range(4)]
                + [vec(0), vec(1), vec(2), prow(COL_GD, 3)]
                + [prow(COL_RT, blk) for blk in range(4)]
                + [pl.BlockSpec((LANES, LANES), lambda h: (COL_GATES // LANES, 0)),
                   _const_spec((D, LANES)), _const_spec((D, LANES)), st4, st3, st2, st4, st4])
    return pl.pallas_call(
        _decode_rec_kernel,
        grid=(N_HEADS,),
        in_specs=in_specs,
        out_specs=[vec(0), vec(0), vec(0), st4, st3, st2, st4, st4],
        out_shape=[S((GROUP_W, LANES), F32)] * 3
        + [S((N_HEADS, D, D, LANES), F32), S((N_HEADS, D, LANES), F32), S((N_HEADS, 1, LANES), F32),
           S((N_HEADS, D, D, LANES), F32), S((N_HEADS, D, D, LANES), F32)],
        scratch_shapes=[pltpu.VMEM((D, LANES), F32), pltpu.VMEM((D, LANES), F32)],
        compiler_params=_cparams(("parallel",)),
        name="decode_rec",
    )(scalars, pt, pt, pt, pt, qkvt, qkvt, qkvt, pt, pt, pt, pt, pt, pt, cos_t, sin_t, c0, n0, m0, sg0, sr0)


def _outproj_kernel(x_ref, osb_ref, oml_ref, ogd_ref, ort_ref, gain_ref, wo_ref, nw_ref, rwh_ref, rwl_ref,
                    x1_ref, h2_ref, lg_ref, *, transposed):
    parts = [osb_ref[...]]
    for ref in (oml_ref, ogd_ref, ort_ref):
        parts.append(ref[...].T if transposed else ref[...])
    y = None
    for g, p in enumerate(parts):
        cols = slice(g * GROUP_W, (g + 1) * GROUP_W)
        t = _dot(p * gain_ref[:, cols], wo_ref[cols, :])
        y = t if y is None else y + t
    x1 = x_ref[...] + y
    h2 = _norm_rows(x1, nw_ref[...])
    x1_ref[...] = x1
    h2_ref[...] = h2
    lg_ref[...] = _gates_dot(h2, rwh_ref, rwl_ref)


def _outproj(x, osb, oml, ogd, ort, gain, wo, nw, rwh, rwl, transposed):
    n = x.shape[0]
    tm = min(TM_PROMPT, n)
    S = jax.ShapeDtypeStruct
    row = lambda c: pl.BlockSpec((tm, c), lambda i: (i, 0))
    mix = _const_spec((GROUP_W, DEC_BATCH)) if transposed else row(GROUP_W)
    return pl.pallas_call(
        functools.partial(_outproj_kernel, transposed=transposed),
        grid=(n // tm,),
        in_specs=[row(D_MODEL), row(GROUP_W), mix, mix, mix, _const_spec((1, D_MODEL)),
                  _const_spec((D_MODEL, D_MODEL)), _const_spec((1, D_MODEL)),
                  _const_spec((D_MODEL, LANES)), _const_spec((D_MODEL, LANES))],
        out_specs=[row(D_MODEL), row(D_MODEL), row(LANES)],
        out_shape=[S((n, D_MODEL), F32), S((n, D_MODEL), F32), S((n, LANES), F32)],
        compiler_params=_cparams(("parallel",)),
        name="outproj",
    )(x, osb, oml, ogd, ort, gain, wo, nw, rwh, rwl)


def _route_kernel(lg_ref, rb_ref, tri_ref, cin_ref, ri_ref, rf_ref, cnt_ref):
    @pl.when(pl.program_id(0) == 0)
    def _():
        cnt_ref[...] = cin_ref[...]

    lg = lg_ref[...] + rb_ref[...]
    lane = lax.broadcasted_iota(I32, lg.shape, 1)
    ninf = -jnp.inf
    big = LANES - 1
    rmax = lambda x: jnp.max(x, axis=1, keepdims=True)
    rmin = lambda x: jnp.min(x, axis=1, keepdims=True)
    gl = jnp.where(lane < N_GROUPS, lg, ninf)
    gmax = rmax(gl)
    g_sel = rmin(jnp.where(gl == gmax, lane, big))
    g_prob = 1.0 / jnp.sum(jnp.exp(gl - gmax), axis=1, keepdims=True)
    e_lane = (lane >= N_GROUPS) & (lane < N_GROUPS + N_EXPERTS)
    em = jnp.where(e_lane & (jnp.right_shift(lane - N_GROUPS, 3) == g_sel), lg, ninf)
    v1 = rmax(em)
    i1 = rmin(jnp.where(em == v1, lane, big))
    em2 = jnp.where(lane == i1, ninf, em)
    v2 = rmax(em2)
    i2 = rmin(jnp.where(em2 == v2, lane, big))
    t = jnp.exp(v2 - v1)
    gate1 = g_prob / (1.0 + t)
    gate2 = g_prob * t / (1.0 + t)
    e1 = i1 - N_GROUPS
    e2 = i2 - N_GROUPS
    onehot = jnp.where((lane == e1) | (lane == e2), 1.0, 0.0)
    before = _dot(tri_ref[...], onehot) + cnt_ref[...]
    r1 = jnp.sum(jnp.where(lane == e1, before, 0.0), axis=1, keepdims=True).astype(I32)
    r2 = jnp.sum(jnp.where(lane == e2, before, 0.0), axis=1, keepdims=True).astype(I32)
    cnt_ref[...] += jnp.sum(onehot, axis=0, keepdims=True)
    ri = jnp.where(lane == 0, e1, jnp.where(lane == 1, e2, jnp.where(lane == 2, r1, jnp.where(lane == 3, r2, 0))))
    ri_ref[...] = ri.T[0:8, :]
    rf_ref[...] = jnp.where(lane == 0, gate1, jnp.where(lane == 1, gate2, 0.0))


def _route(logits, rb, tri, cnt_in):
    n = logits.shape[0]
    tm = tri.shape[0]
    S = jax.ShapeDtypeStruct
    row = pl.BlockSpec((tm, LANES), lambda i: (i, 0))
    return pl.pallas_call(
        _route_kernel,
        grid=(n // tm,),
        in_specs=[row, _const_spec((1, LANES)), _const_spec((tm, tm)), _const_spec((1, LANES))],
        out_specs=[pl.BlockSpec((8, tm), lambda i: (0, i)), row, _const_spec((1, LANES))],
        out_shape=[S((8, n), I32), S((n, LANES), F32), S((1, LANES), F32)],
        compiler_params=_cparams(("arbitrary",)),
        name="route",
    )(logits, rb, tri, cnt_in)


def _slot(route_refs, t, k):
    offs_ref, e1_ref, e2_ref, r1_ref, r2_ref = route_refs
    return offs_ref[(e1_ref, e2_ref)[k][t]] + (r1_ref, r2_ref)[k][t]


def _dispatch_kernel(offs_ref, e1_ref, e2_ref, r1_ref, r2_ref, h_ref, xs_in_ref, xs_ref, sem):
    del xs_in_ref
    route_refs = (offs_ref, e1_ref, e2_ref, r1_ref, r2_ref)
    tm = h_ref.shape[0]
    base = pl.program_id(0) * tm

    def body(r, _):
        src = h_ref.at[pl.ds(r, 1), :]
        for k in range(2):
            pltpu.make_async_copy(src, xs_ref.at[pl.ds(_slot(route_refs, base + r, k), 1), :], sem.at[k]).start()
        return 0

    lax.fori_loop(0, tm, body, 0)
    for k in range(2):
        pltpu.make_async_copy(h_ref, xs_ref.at[pl.ds(0, tm), :], sem.at[k]).wait()


def _dispatch(route, h2, xs):
    n = h2.shape[0]
    tm = min(TM_PROMPT, n)
    return pl.pallas_call(
        _dispatch_kernel,
        grid_spec=pltpu.PrefetchScalarGridSpec(
            num_scalar_prefetch=5,
            grid=(n // tm,),
            in_specs=[pl.BlockSpec((tm, D_MODEL), lambda i, *_: (i, 0)), pl.BlockSpec(memory_space=pl.ANY)],
            out_specs=pl.BlockSpec(memory_space=pl.ANY),
            scratch_shapes=[pltpu.SemaphoreType.DMA((2,))]),
        out_shape=jax.ShapeDtypeStruct(xs.shape, xs.dtype),
        input_output_aliases={6: 0},
        compiler_params=_cparams(("arbitrary",)),
        name="dispatch",
    )(*route, h2, xs)


def _experts_kernel(te_ref, tv_ref, xs_ref, wg_ref, wu_ref, wd_ref, ys_ref, wg_s, wu_s, wd_s):
    i = pl.program_id(0)
    valid = tv_ref[i] > 0
    fresh = (i == 0) | (te_ref[i] != te_ref[jnp.maximum(i - 1, 0)])

    @pl.when(valid & fresh)
    def _():
        wg_s[...] = wg_ref[0].astype(BF16)
        wu_s[...] = wu_ref[0].astype(BF16)
        wd_s[...] = wd_ref[0].astype(BF16)

    @pl.when(valid)
    def _():
        x = xs_ref[...].astype(BF16)
        a = _dot(x, wg_s[...])
        u = _dot(x, wu_s[...])
        act = a * _sigmoid(a) * u
        ys_ref[...] = _dot(act, wd_s[...])

    @pl.when(jnp.logical_not(valid))
    def _():
        ys_ref[...] = jnp.zeros_like(ys_ref)


def _experts(tile_expert, tile_valid, xs, wg, wu, wd):
    tm = TM_EXPERT
    return pl.pallas_call(
        _experts_kernel,
        grid_spec=pltpu.PrefetchScalarGridSpec(
            num_scalar_prefetch=2,
            grid=(N_TILES,),
            in_specs=[pl.BlockSpec((tm, D_MODEL), lambda i, te, tv: (i, 0)),
                      pl.BlockSpec((1, D_MODEL, D_EXPERT), lambda i, te, tv: (te[i], 0, 0)),
                      pl.BlockSpec((1, D_MODEL, D_EXPERT), lambda i, te, tv: (te[i], 0, 0)),
                      pl.BlockSpec((1, D_EXPERT, D_MODEL), lambda i, te, tv: (te[i], 0, 0))],
            out_specs=pl.BlockSpec((tm, D_MODEL), lambda i, te, tv: (i, 0)),
            scratch_shapes=[pltpu.VMEM((D_MODEL, D_EXPERT), BF16), pltpu.VMEM((D_MODEL, D_EXPERT), BF16),
                            pltpu.VMEM((D_EXPERT, D_MODEL), BF16)]),
        out_shape=jax.ShapeDtypeStruct((N_SLOTS, D_MODEL), F32),
        compiler_params=_cparams(("arbitrary",)),
        name="experts",
    )(tile_expert, tile_valid, xs, wg, wu, wd)


def _combine_kernel(offs_ref, e1_ref, e2_ref, r1_ref, r2_ref, x1_ref, rf_ref, fw_ref, ys_ref, out_ref,
                    b1_s, b2_s, sem, *, final):
    route_refs = (offs_ref, e1_ref, e2_ref, r1_ref, r2_ref)
    bufs = (b1_s, b2_s)
    tm = x1_ref.shape[0]
    base = pl.program_id(0) * tm

    def body(r, _):
        for k in range(2):
            pltpu.make_async_copy(ys_ref.at[pl.ds(_slot(route_refs, base + r, k), 1), :],
                                  bufs[k].at[pl.ds(r, 1), :], sem.at[k]).start()
        return 0

    lax.fori_loop(0, tm, body, 0)
    for k in range(2):
        pltpu.make_async_copy(ys_ref.at[pl.ds(0, tm), :], bufs[k], sem.at[k]).wait()
    x2 = x1_ref[...] + rf_ref[:, 0:1] * b1_s[...] + rf_ref[:, 1:2] * b2_s[...]
    out_ref[...] = _norm_rows(x2, fw_ref[...]) if final else x2


def _combine(route, x1, rf, fw, ys, final):
    n = x1.shape[0]
    tm = min(TM_PROMPT, n)
    row = lambda c: pl.BlockSpec((tm, c), lambda i, *_: (i, 0))
    return pl.pallas_call(
        functools.partial(_combine_kernel, final=final),
        grid_spec=pltpu.PrefetchScalarGridSpec(
            num_scalar_prefetch=5,
            grid=(n // tm,),
            in_specs=[row(D_MODEL), row(LANES), pl.BlockSpec((1, D_MODEL), lambda i, *_: (0, 0)),
                      pl.BlockSpec(memory_space=pl.ANY)],
            out_specs=row(D_MODEL),
            scratch_shapes=[pltpu.VMEM((tm, D_MODEL), F32), pltpu.VMEM((tm, D_MODEL), F32),
                            pltpu.SemaphoreType.DMA((2,))]),
        out_shape=jax.ShapeDtypeStruct((n, D_MODEL), F32),
        compiler_params=_cparams(("arbitrary",)),
        name="combine",
    )(*route, x1, rf, fw, ys)


def _tri(n, kind):
    r = np.arange(n)[:, None]
    c = np.arange(n)[None, :]
    m = {"chunk_lower_incl": (c <= r) & (r // CHUNK == c // CHUNK),
         "lower_strict": c < r,
         "row_gt_col": r > c,
         "later_page": (r % 8 == c % 8) & (c // 8 > r // 8)}[kind]
    return jnp.asarray(m.astype(np.float32), dtype=BF16)


def _rope_tables(pos):
    half = HEAD_DIM // 2
    inv = ROPE_BASE ** (-np.arange(half, dtype=np.float64) / half)
    ang = np.asarray(pos, np.float64)[:, None] * inv[None, :]
    cos = np.concatenate([np.cos(ang), np.cos(ang)], axis=1)
    sin = np.concatenate([-np.sin(ang), np.sin(ang)], axis=1)
    return cos.astype(np.float32), sin.astype(np.float32)


def _retention_consts(L):
    log_gamma = np.log1p(-np.exp2(-5.0 - np.arange(N_HEADS, dtype=np.float64)))
    idx = np.arange(L, dtype=np.float64)
    diff = np.maximum(idx[:, None] - idx[None, :], 0.0)
    dmat = np.where(idx[None, :] <= idx[:, None], np.exp(log_gamma[:, None, None] * diff), 0.0)
    xi = np.exp(log_gamma[:, None] * (idx + 1.0))
    zeta = np.exp(log_gamma[:, None] * (L - 1.0 - idx))
    xz = np.zeros((L, LANES), np.float64)
    xz[:, 0:N_HEADS] = xi.T
    xz[:, N_HEADS:2 * N_HEADS] = zeta.T
    return (jnp.asarray(dmat, F32), jnp.asarray(xz, F32), jnp.asarray(np.exp(log_gamma * L), F32),
            np.exp(log_gamma))


def _hi_lo(w):
    hi = w.astype(BF16)
    return hi, (w - hi.astype(F32)).astype(BF16)


def _pad_lanes(w):
    return jnp.pad(w, ((0, 0), (0, LANES - w.shape[1])))


def _prep_w_in(w):
    ml0 = 3 * GROUP_W
    mlg = ml0 + 4 * GROUP_W
    gd0 = mlg + 2 * N_HEADS
    gdg = gd0 + 4 * GROUP_W
    rt0 = gdg + 2 * N_HEADS
    main = jnp.concatenate([w[:, 0:ml0], w[:, ml0:mlg], w[:, gd0:gdg], w[:, rt0:]], axis=1).astype(BF16)
    gates = _pad_lanes(jnp.concatenate([w[:, mlg:gd0], w[:, gdg:rt0]], axis=1))
    return (main,) + _hi_lo(gates)


def _dispatch_plan(cnt):
    ntile = (cnt + TM_EXPERT - 1) // TM_EXPERT
    tile_end = jnp.cumsum(ntile)
    offs = (tile_end - ntile) * TM_EXPERT
    tid = jnp.arange(N_TILES, dtype=I32)
    te = jnp.minimum(jnp.sum(tid[:, None] >= tile_end[None, :], axis=1), N_EXPERTS - 1).astype(I32)
    tv = (tid < tile_end[-1]).astype(I32)
    return offs.astype(I32), te, tv


def kernel(x_prompt, x_sample, cache_sb_k, cache_sb_v, state_mlstm_c, state_mlstm_n, state_mlstm_m, state_gdn_s, state_gdn_conv, state_ret_s, page_table, norm_attn_w, w_in, sb_logit_bias, mlstm_gate_bias, gdn_conv_w, gdn_a_log, gdn_dt_bias, head_norm_w, w_out, norm_ffn_w, router_group_w, router_group_b, router_expert_w, router_expert_b, expert_w_gate, expert_w_up, expert_w_down, final_norm_w):
    D = HEAD_DIM
    xp = x_prompt.reshape(N_PROMPT, D_MODEL)
    xd = x_sample.reshape(DEC_BATCH, D_MODEL)
    n_pool = cache_sb_k.shape[1]
    cache_kt = cache_sb_k.transpose(0, 1, 3, 4, 2).reshape(DEPTH, n_pool, GROUP_W, PAGE_SIZE)
    cache_vt = cache_sb_v.transpose(0, 1, 3, 4, 2).reshape(DEPTH, n_pool, GROUP_W, PAGE_SIZE)

    tril = _tri(STEP_ROWS, "chunk_lower_incl")
    tri_sb = _tri(SB_BLOCK, "row_gt_col")
    tri_page = _tri(PAGE_SIZE, "row_gt_col")
    later_pages = _tri(8 * N_PAGES, "later_page")
    tri_rp = _tri(TM_PROMPT, "lower_strict")
    tri_rd = _tri(DEC_BATCH, "lower_strict")
    cos_p, sin_p = _rope_tables(np.arange(SEQ))
    cos_p = jnp.asarray(np.tile(cos_p, (1, N_HEADS)))
    sin_p = jnp.asarray(np.tile(sin_p, (1, N_HEADS)))
    cos_d, sin_d = _rope_tables([PAST_LEN])
    cos_d = jnp.asarray(np.tile(cos_d.T, (1, LANES)))
    sin_d = jnp.asarray(np.tile(sin_d.T, (1, LANES)))
    dmat, xz, gch, gamma = _retention_consts(CHUNK)
    zeros4 = jnp.zeros((N_HEADS,), F32)

    outs = {k: [] for k in ("kp", "vp", "ks", "vs", "cp", "np", "mp", "cs", "ns", "ms", "gp", "gcp", "gs", "gcs",
                            "rp", "rs")}
    yp = yd = None
    for l in range(DEPTH):
        w_main, wg_hi, wg_lo = _prep_w_in(w_in[l])
        nw = norm_attn_w[l][None, :]
        gate_bias_row = _pad_lanes(jnp.concatenate([mlstm_gate_bias[l], zeros4, gdn_dt_bias[l]])[None, :])
        a_log_row = _pad_lanes(jnp.concatenate([zeros4, zeros4, zeros4, gdn_a_log[l]])[None, :])
        bias = sb_logit_bias[l]

        sbq, sbk, sbv, ml, gd, rt, gt = _inproj_prompt(xp, nw, w_main, wg_hi, wg_lo)
        osb_p = _sb_prompt(sbq, sbk, sbv, bias, tri_sb)
        oml_p, c1p, n1p, m1p = _mlstm_prompt(ml, gt, gate_bias_row, tril)
        ogd_p, s1p, cv1p = _gdn_prompt(gd, gt, gate_bias_row, a_log_row, gdn_conv_w[l], tril)
        ort_p, r1p = _ret_prompt(rt, cos_p, sin_p, dmat, xz, gch)

        sbq_d, gdraw_d, pt = _inproj_decode(xd, nw, w_main, wg_hi, wg_lo)
        osb_d = _sb_decode(sbq_d, cache_kt, cache_vt, page_table, bias, tri_page, later_pages, l)
        qkvt, cv1d = _gdn_decode_conv(gdraw_d, state_gdn_conv[l].transpose(1, 0, 2), gdn_conv_w[l])
        scalars = jnp.stack([mlstm_gate_bias[l][:N_HEADS], mlstm_gate_bias[l][N_HEADS:], gdn_dt_bias[l],
                             jnp.exp(gdn_a_log[l]), jnp.asarray(gamma, F32), zeros4, zeros4, zeros4])
        oml_d, ogd_d, ort_d, c1d, n1d, m1d, s1d, r1d = _decode_rec(
            scalars, pt, qkvt, cos_d, sin_d,
            state_mlstm_c[l].transpose(1, 2, 3, 0), state_mlstm_n[l].transpose(1, 2, 0),
            state_mlstm_m[l].T.reshape(N_HEADS, 1, DEC_BATCH),
            state_gdn_s[l].transpose(1, 2, 3, 0), state_ret_s[l].transpose(1, 2, 3, 0))

        gain = head_norm_w[l][None, :]
        wo = w_out[l].astype(BF16)
        nfw = norm_ffn_w[l][None, :]
        rw_hi, rw_lo = _hi_lo(_pad_lanes(jnp.concatenate([router_group_w[l], router_expert_w[l]], axis=1)))
        rb = _pad_lanes(jnp.concatenate([router_group_b[l], router_expert_b[l]])[None, :])
        x1p, h2p, lgp = _outproj(xp, osb_p, oml_p, ogd_p, ort_p, gain, wo, nfw, rw_hi, rw_lo, False)
        x1d, h2d, lgd = _outproj(xd, osb_d, oml_d, ogd_d, ort_d, gain, wo, nfw, rw_hi, rw_lo, True)
        rip, rfp, cnt_p = _route(lgp, rb, tri_rp, jnp.zeros((1, LANES), F32))
        rid, rfd, cnt = _route(lgd, rb, tri_rd, cnt_p)

        offs, te, tv = _dispatch_plan(cnt[0, :N_EXPERTS].astype(I32))
        route_p = (offs, rip[0], rip[1], rip[2], rip[3])
        route_d = (offs, rid[0], rid[1], rid[2], rid[3])
        xs = _dispatch(route_p, h2p, jnp.zeros((N_SLOTS, D_MODEL), F32))
        xs = _dispatch(route_d, h2d, xs)
        ys = _experts(te, tv, xs, expert_w_gate[l], expert_w_up[l], expert_w_down[l])
        final = l == DEPTH - 1
        fw = final_norm_w[None, :]
        xp_next = _combine(route_p, x1p, rfp, fw, ys, final)
        xd_next = _combine(route_d, x1d, rfd, fw, ys, final)
        if final:
            yp, yd = xp_next, xd_next
        else:
            xp, xd = xp_next, xd_next

        heads_p = lambda a: a.reshape(BATCH, SEQ, N_HEADS, D)
        heads_t = lambda a: a.reshape(N_HEADS, D, DEC_BATCH).transpose(2, 0, 1)[:, None]
        outs["kp"].append(heads_p(sbk))
        outs["vp"].append(heads_p(sbv))
        outs["ks"].append(heads_t(pt[GROUP_W:2 * GROUP_W]))
        outs["vs"].append(heads_t(pt[2 * GROUP_W:3 * GROUP_W]))
        outs["cp"].append(c1p)
        outs["np"].append(n1p)
        outs["mp"].append(m1p[:, 0, :N_HEADS])
        outs["cs"].append(c1d.transpose(3, 0, 1, 2))
        outs["ns"].append(n1d.transpose(2, 0, 1))
        outs["ms"].append(m1d[:, 0, :].T)
        outs["gp"].append(s1p)
        outs["gcp"].append(cv1p)
        outs["gs"].append(s1d.transpose(3, 0, 1, 2))
        outs["gcs"].append(cv1d.transpose(1, 0, 2))
        outs["rp"].append(r1p)
        outs["rs"].append(r1d.transpose(3, 0, 1, 2))

    st = lambda k: jnp.stack(outs[k], axis=0)
    return (yp.reshape(BATCH, SEQ, D_MODEL), yd.reshape(DEC_BATCH, 1, D_MODEL),
            st("kp"), st("vp"), st("ks"), st("vs"),
            st("cp"), st("np"), st("mp"), st("cs"), st("ns"), st("ms"),
            st("gp"), st("gcp"), st("gs"), st("gcs"), st("rp"), st("rs"))
```

```python
import functools
import math

import numpy as np
import jax
import jax.numpy as jnp
from jax import lax
from jax.experimental import pallas as pl
from jax.experimental.pallas import tpu as pltpu

F32 = jnp.float32
BF16 = jnp.bfloat16
I32 = jnp.int32

D_MODEL = 1024
BATCH = 8
SEQ = 2048
DEPTH = 2
DEC_BATCH = 128
PAST_LEN = 2048
PAGE_SIZE = 128
N_PAGES = PAST_LEN // PAGE_SIZE
HEAD_DIM = 64
N_HEADS = 4
GROUP_W = N_HEADS * HEAD_DIM
CHUNK = 64
CONV_W = 4
N_GROUPS = 4
EXPERTS_PER_GROUP = 8
N_EXPERTS = N_GROUPS * EXPERTS_PER_GROUP
D_EXPERT = D_MODEL // 2
ROPE_BASE = 10000.0
EPS = 1e-6
SCALE = HEAD_DIM ** -0.5

N_PROMPT = BATCH * SEQ
LANES = 128
N_MAIN = 3 * GROUP_W + 3 * 4 * GROUP_W
N_PROJ = N_MAIN + LANES
COL_ML = 3 * GROUP_W
COL_GD = COL_ML + 4 * GROUP_W
COL_RT = COL_GD + 4 * GROUP_W
COL_GATES = N_MAIN

STEP_ROWS = 4 * CHUNK
SB_BLOCK = 256
TM_PROMPT = 256
TM_EXPERT = 256
N_ASSIGN = 2 * (N_PROMPT + DEC_BATCH)
TM_ROUTE = 512
RUN_ALIGN = 8
N_TOKEN_TILES = N_PROMPT // TM_ROUTE + 1
N_TILES = -(-(N_ASSIGN + N_TOKEN_TILES * N_EXPERTS * (RUN_ALIGN - 1)) // TM_EXPERT) + N_EXPERTS
N_SLOTS = N_TILES * TM_EXPERT
VMEM_LIMIT = 48 * 1024 * 1024

_NT = (((1,), (1,)), ((), ()))
_TN = (((0,), (0,)), ((), ()))


_NN = (((1,), (0,)), ((), ()))


def _mm(a, b, dims):
    return lax.dot_general(a.astype(BF16), b.astype(BF16), dims, preferred_element_type=F32)


def _dot(a, b):
    return _mm(a, b, _NN)


def _dot_nt(a, b):
    return _mm(a, b, _NT)


def _dot_tn(a, b):
    return _mm(a, b, _TN)


def _split3(x):
    x1 = x.astype(BF16)
    r1 = x - x1.astype(F32)
    x2 = r1.astype(BF16)
    x3 = (r1 - x2.astype(F32)).astype(BF16)
    return x1, x2, x3


def _tri_dot(tri, x):
    x1, x2, x3 = _split3(x)
    return _dot(tri, x1) + _dot(tri, x2) + _dot(tri, x3)


def _log_sigmoid(z):
    return jnp.minimum(z, 0.0) - jnp.log(1.0 + jnp.exp(-jnp.abs(z)))


def _softplus(z):
    return jnp.maximum(z, 0.0) + jnp.log(1.0 + jnp.exp(-jnp.abs(z)))


def _sigmoid(z):
    return 1.0 / (1.0 + jnp.exp(-z))


def _head_rms(x):
    return x * lax.rsqrt(jnp.mean(x * x, axis=-1, keepdims=True) + EPS)


def _cparams(sem):
    return pltpu.CompilerParams(dimension_semantics=sem, vmem_limit_bytes=VMEM_LIMIT)


def _const_spec(shape):
    nd = len(shape)
    return pl.BlockSpec(shape, lambda *_: (0,) * nd)


def _norm_rows(x, w):
    return x * lax.rsqrt(jnp.mean(x * x, axis=-1, keepdims=True) + EPS) * w


def _gates_dot(h, wgh_ref, wgl_ref):
    hb = h.astype(BF16)
    hl = (h - hb.astype(F32)).astype(BF16)
    return _dot(hb, wgh_ref[...]) + _dot(hl, wgh_ref[...]) + _dot(hb, wgl_ref[...])


def _inproj_prompt_kernel(x_ref, nw_ref, w_ref, wgh_ref, wgl_ref,
                          sbq_ref, sbk_ref, sbv_ref, ml_ref, gd_ref, rt_ref, gt_ref):
    h = _norm_rows(x_ref[...], nw_ref[...])
    hb = h.astype(BF16)
    seg = lambda a, b: _dot(hb, w_ref[:, a:b])
    sbq_ref[...] = seg(0, GROUP_W)
    sbk_ref[...] = seg(GROUP_W, 2 * GROUP_W)
    sbv_ref[...] = seg(2 * GROUP_W, 3 * GROUP_W)
    ml_ref[...] = seg(COL_ML, COL_GD)
    gd_ref[...] = seg(COL_GD, COL_RT)
    rt_ref[...] = seg(COL_RT, N_MAIN)
    gt_ref[...] = _gates_dot(h, wgh_ref, wgl_ref)


def _inproj_prompt(x, nw, w, wgh, wgl):
    n = x.shape[0]
    tm = TM_PROMPT
    row = lambda c: pl.BlockSpec((tm, c), lambda i: (i, 0))
    S = jax.ShapeDtypeStruct
    return pl.pallas_call(
        _inproj_prompt_kernel,
        grid=(n // tm,),
        in_specs=[row(D_MODEL), _const_spec((1, D_MODEL)), _const_spec((D_MODEL, N_MAIN)),
                  _const_spec((D_MODEL, LANES)), _const_spec((D_MODEL, LANES))],
        out_specs=[row(GROUP_W), row(GROUP_W), row(GROUP_W), row(4 * GROUP_W), row(4 * GROUP_W),
                   row(4 * GROUP_W), row(LANES)],
        out_shape=[S((n, GROUP_W), F32)] * 3 + [S((n, 4 * GROUP_W), F32)] * 3 + [S((n, LANES), F32)],
        compiler_params=_cparams(("parallel",)),
        name="inproj_prompt",
    )(x, nw, w, wgh, wgl)


def _inproj_decode_kernel(x_ref, nw_ref, w_ref, wgh_ref, wgl_ref, sbq_ref, gdraw_ref, pt_ref):
    h = _norm_rows(x_ref[...], nw_ref[...])
    hb = h.astype(BF16)
    for j in range(N_MAIN // LANES):
        p = _dot(hb, w_ref[:, j * LANES:(j + 1) * LANES])
        if j < GROUP_W // LANES:
            sbq_ref[:, j * LANES:(j + 1) * LANES] = p
        c0 = j * LANES - COL_GD
        if 0 <= c0 < 3 * GROUP_W:
            gdraw_ref[:, c0:c0 + LANES] = p
        pt_ref[j * LANES:(j + 1) * LANES, :] = p.T
    pt_ref[N_MAIN:N_PROJ, :] = _gates_dot(h, wgh_ref, wgl_ref).T


def _inproj_decode(x, nw, w, wgh, wgl):
    S = jax.ShapeDtypeStruct
    return pl.pallas_call(
        _inproj_decode_kernel,
        out_shape=[S((DEC_BATCH, GROUP_W), F32), S((DEC_BATCH, 3 * GROUP_W), F32), S((N_PROJ, DEC_BATCH), F32)],
        compiler_params=_cparams(None),
        name="inproj_decode",
    )(x, nw, w, wgh, wgl)


def _sb_prompt_kernel(bias_ref, q_ref, k_ref, v_ref, tri_ref, o_ref, acc_s, carry_s):
    tb = SB_BLOCK
    qi = pl.program_id(1)
    tri = tri_ref[...]
    r = lax.broadcasted_iota(I32, (tb, tb), 0)
    c = lax.broadcasted_iota(I32, (tb, tb), 1)
    dmask = c < r
    acc_s[...] = jnp.zeros_like(acc_s)
    carry_s[...] = jnp.zeros_like(carry_s)
    q = q_ref[...] * SCALE

    def block(j, mask):
        start = pl.multiple_of(j * tb, tb)
        kj = k_ref[pl.ds(start, tb), :]
        vj = v_ref[pl.ds(start, tb), :]
        heads = range(N_HEADS)
        sls = [slice(h * HEAD_DIM, (h + 1) * HEAD_DIM) for h in heads]
        z = [_dot_nt(q[:, sls[h]], kj[:, sls[h]]) + bias_ref[h] for h in heads]
        lsz = [_log_sigmoid(z[h]) for h in heads]
        lk = [lsz[h] - z[h] for h in heads]
        if mask is not None:
            lk = [jnp.where(mask, x, 0.0) for x in lk]
        lw_all = _dot(jnp.concatenate([x.astype(BF16) for x in lk], axis=0), tri)
        lw = [lw_all[h * tb:(h + 1) * tb, :] for h in heads]
        a = [jnp.exp(lsz[h] + lw[h] + carry_s[:, h:h + 1]) for h in heads]
        if mask is not None:
            a = [jnp.where(mask, x, 0.0) for x in a]
        av = [_dot(a[h], vj[:, sls[h]]) for h in heads]
        for h in heads:
            acc_s[:, sls[h]] += av[h]
            carry_s[:, h:h + 1] += lw[h][:, 0:1] + lk[h][:, 0:1]

    block(qi, dmask)

    def body(it, _):
        block(qi - 1 - it, None)
        return 0

    lax.fori_loop(0, qi, body, 0)
    for h in range(N_HEADS):
        sl = slice(h * HEAD_DIM, (h + 1) * HEAD_DIM)
        o_ref[:, sl] = _head_rms(acc_s[:, sl])


def _sb_prompt(q, k, v, bias, tri):
    tb = SB_BLOCK
    nq = SEQ // tb
    return pl.pallas_call(
        _sb_prompt_kernel,
        grid_spec=pltpu.PrefetchScalarGridSpec(
            num_scalar_prefetch=0,
            grid=(BATCH, nq),
            in_specs=[pl.BlockSpec(memory_space=pltpu.SMEM),
                      pl.BlockSpec((tb, GROUP_W), lambda b, i: (b * nq + i, 0)),
                      pl.BlockSpec((SEQ, GROUP_W), lambda b, i: (b, 0)),
                      pl.BlockSpec((SEQ, GROUP_W), lambda b, i: (b, 0)),
                      _const_spec((tb, tb))],
            out_specs=pl.BlockSpec((tb, GROUP_W), lambda b, i: (b * nq + i, 0)),
            scratch_shapes=[pltpu.VMEM((tb, GROUP_W), F32), pltpu.VMEM((tb, LANES), F32)]),
        out_shape=jax.ShapeDtypeStruct((N_PROMPT, GROUP_W), F32),
        compiler_params=_cparams(("parallel", "parallel")),
        name="sb_prompt",
    )(bias, q, k, v, tri)


def _sb_decode_kernel(pt_ref, bias_ref, q_ref, *rest):
    k_refs = rest[:N_PAGES]
    v_refs = rest[N_PAGES:2 * N_PAGES]
    tri_ref, pg_ref, o_ref = rest[2 * N_PAGES:]
    n_rows = 8 * N_PAGES
    b = pl.program_id(0)
    r = b % 8
    qrow = q_ref[pl.ds(r, 1), :] * SCALE
    rowi = lax.broadcasted_iota(I32, (8, GROUP_W), 0)
    lanei = lax.broadcasted_iota(I32, (8, GROUP_W), 1)
    head_of_lane = jnp.right_shift(lanei, 6)
    qbd = jnp.where(head_of_lane == rowi, qrow, 0.0).astype(BF16)
    row8 = lax.broadcasted_iota(I32, (8, 1), 0)
    bias = jnp.zeros((8, 1), F32)
    for h in range(N_HEADS):
        bias = jnp.where(row8 == h, bias_ref[h], bias)
    z = jnp.concatenate([_dot(qbd, k_refs[j][0, 0]) + bias for j in range(N_PAGES)], axis=0)
    lsz = _log_sigmoid(z)
    lk = lsz - z
    lw = _dot(lk, tri_ref[...])
    tot = jnp.broadcast_to(lw[:, 0:1] + lk[:, 0:1], (n_rows, LANES))
    later_pages = _tri_dot(pg_ref[...], tot)
    a = jnp.exp(lsz + lw + later_pages)
    acc = jnp.zeros((8, GROUP_W), F32)
    for j in range(N_PAGES):
        acc = acc + _dot_nt(a[j * 8:(j + 1) * 8, :], v_refs[j][0, 0])
    own = jnp.where(head_of_lane == rowi, acc, 0.0)
    orow = jnp.sum(own, axis=0, keepdims=True)
    pieces = [_head_rms(orow[:, h * HEAD_DIM:(h + 1) * HEAD_DIM]) for h in range(N_HEADS)]
    o_ref[pl.ds(r, 1), :] = jnp.concatenate(pieces, axis=1)


def _sb_decode(q, cache_kt, cache_vt, page_table, bias, tri, later_pages, layer):
    def page_spec(j):
        return pl.BlockSpec((1, 1, GROUP_W, PAGE_SIZE), lambda b, pt: (layer, pt[b, j], 0, 0))

    in_specs = ([pl.BlockSpec(memory_space=pltpu.SMEM),
                 pl.BlockSpec((8, GROUP_W), lambda b, pt: (b // 8, 0))]
                + [page_spec(j) for j in range(N_PAGES)] * 2
                + [pl.BlockSpec((PAGE_SIZE, PAGE_SIZE), lambda b, pt: (0, 0)),
                   pl.BlockSpec((8 * N_PAGES, 8 * N_PAGES), lambda b, pt: (0, 0))])
    return pl.pallas_call(
        _sb_decode_kernel,
        grid_spec=pltpu.PrefetchScalarGridSpec(
            num_scalar_prefetch=1,
            grid=(DEC_BATCH,),
            in_specs=in_specs,
            out_specs=pl.BlockSpec((8, GROUP_W), lambda b, pt: (b // 8, 0))),
        out_shape=jax.ShapeDtypeStruct((DEC_BATCH, GROUP_W), F32),
        compiler_params=_cparams(("arbitrary",)),
        name="sb_decode",
    )(page_table, bias, q, *([cache_kt] * N_PAGES), *([cache_vt] * N_PAGES), tri, later_pages)


def _chunk_masks(L):
    r = lax.broadcasted_iota(I32, (L, L), 0)
    c = lax.broadcasted_iota(I32, (L, L), 1)
    return c <= r, c < r


def _head_cols(blk, h):
    return slice(blk * GROUP_W + h * HEAD_DIM, blk * GROUP_W + (h + 1) * HEAD_DIM)


def _mlstm_prompt_kernel(x_ref, g_ref, gb_ref, tril_ref, o_ref, c1_ref, n1_ref, m1_ref, c_s, n_s, m_s):
    L = CHUNK
    R = STEP_ROWS
    ci = pl.program_id(1)

    @pl.when(ci == 0)
    def _():
        c_s[...] = jnp.zeros_like(c_s)
        n_s[...] = jnp.zeros_like(n_s)
        m_s[...] = jnp.zeros_like(m_s)

    g = g_ref[...] + gb_ref[...]
    lane = lax.broadcasted_iota(I32, (R, LANES), 1)
    lf = jnp.where((lane >= N_HEADS) & (lane < 2 * N_HEADS), _log_sigmoid(g), 0.0)
    cum = _tri_dot(tril_ref[...], lf)
    xt = jnp.where(lane < N_HEADS, g, cum).T
    causal, _ = _chunk_masks(L)
    P = [(s_i, h) for s_i in range(R // L) for h in range(N_HEADS)]
    rows = [slice(s_i * L, (s_i + 1) * L) for s_i, _ in P]
    hs = [h for _, h in P]
    np_ = range(len(P))
    bc = [cum[rows[p], N_HEADS + hs[p]:N_HEADS + hs[p] + 1] for p in np_]
    li = [g[rows[p], hs[p]:hs[p] + 1] for p in np_]
    log_d = [jnp.where(causal, bc[p] - xt[N_HEADS + hs[p]:N_HEADS + hs[p] + 1, rows[p]] + xt[hs[p]:hs[p] + 1, rows[p]],
                       -jnp.inf) for p in np_]
    b_last = [bc[p][L - 1:L, :] for p in np_]
    log_w = [b_last[p] - bc[p] + li[p] for p in np_]
    d_max = [jnp.max(log_d[p], axis=1, keepdims=True) for p in np_]
    w_max = [jnp.max(log_w[p], axis=0, keepdims=True) for p in np_]
    m = [m_s[0:1, h:h + 1] for h in range(N_HEADS)]
    m_in, m_out = [], []
    for p in np_:
        m_in.append(m[hs[p]])
        m[hs[p]] = jnp.maximum(b_last[p] + m[hs[p]], w_max[p])
        m_out.append(m[hs[p]])
    log_inter = [bc[p] + m_in[p] for p in np_]
    m_row = [jnp.maximum(log_inter[p], d_max[p]) for p in np_]
    dexp = [jnp.exp(log_d[p] - m_row[p]) for p in np_]
    w_inter = [jnp.exp(log_inter[p] - m_row[p]) for p in np_]
    decay = [jnp.exp(b_last[p] + m_in[p] - m_out[p]) for p in np_]
    q = [x_ref[rows[p], _head_cols(0, hs[p])] for p in np_]
    k = [x_ref[rows[p], _head_cols(1, hs[p])] * SCALE for p in np_]
    v = [x_ref[rows[p], _head_cols(2, hs[p])] for p in np_]
    kw = [k[p] * jnp.exp(log_w[p] - m_out[p]) for p in np_]
    s_mat = [_dot_nt(q[p], k[p]) * dexp[p] for p in np_]
    sv = [_dot(s_mat[p], v[p]) for p in np_]
    kv = [_dot_tn(kw[p], v[p]) for p in np_]
    kw_sum = [jnp.sum(kw[p], axis=0, keepdims=True) for p in np_]
    c = [c_s[h] for h in range(N_HEADS)]
    n = [n_s[h:h + 1, :] for h in range(N_HEADS)]
    c_in, n_in = [], []
    for p in np_:
        c_in.append(c[hs[p]])
        n_in.append(n[hs[p]])
        c[hs[p]] = decay[p] * c[hs[p]] + kv[p]
        n[hs[p]] = decay[p] * n[hs[p]] + kw_sum[p]
    qc = [_dot(q[p], c_in[p]) for p in np_]
    s_sum = [jnp.sum(s_mat[p], axis=1, keepdims=True) for p in np_]
    qn = [jnp.sum(q[p] * n_in[p], axis=1, keepdims=True) for p in np_]
    hh = [(sv[p] + w_inter[p] * qc[p]) / jnp.maximum(jnp.abs(s_sum[p] + w_inter[p] * qn[p]), jnp.exp(-m_row[p]))
          for p in np_]
    ms = [jnp.mean(hh[p] * hh[p], axis=1, keepdims=True) for p in np_]
    for p in np_:
        og = x_ref[rows[p], _head_cols(3, hs[p])]
        o_ref[rows[p], hs[p] * HEAD_DIM:(hs[p] + 1) * HEAD_DIM] = hh[p] * lax.rsqrt(ms[p] + EPS) * _sigmoid(og)
    for h in range(N_HEADS):
        c_s[h] = c[h]
        n_s[h:h + 1, :] = n[h]
        m_s[0:1, h:h + 1] = m[h]

    @pl.when(ci == pl.num_programs(1) - 1)
    def _():
        c1_ref[0] = c_s[...]
        n1_ref[0] = n_s[...]
        m1_ref[0] = m_s[...]


def _mlstm_prompt(ml, gates, gate_bias_row, tril):
    L = STEP_ROWS
    nc = SEQ // L
    S = jax.ShapeDtypeStruct
    return pl.pallas_call(
        _mlstm_prompt_kernel,
        grid=(BATCH, nc),
        in_specs=[pl.BlockSpec((L, 4 * GROUP_W), lambda b, c: (b * nc + c, 0)),
                  pl.BlockSpec((L, LANES), lambda b, c: (b * nc + c, 0)),
                  _const_spec((1, LANES)), _const_spec((L, L))],
        out_specs=[pl.BlockSpec((L, GROUP_W), lambda b, c: (b * nc + c, 0)),
                   pl.BlockSpec((1, N_HEADS, HEAD_DIM, HEAD_DIM), lambda b, c: (b, 0, 0, 0)),
                   pl.BlockSpec((1, N_HEADS, HEAD_DIM), lambda b, c: (b, 0, 0)),
                   pl.BlockSpec((1, 1, LANES), lambda b, c: (b, 0, 0))],
        out_shape=[S((N_PROMPT, GROUP_W), F32), S((BATCH, N_HEADS, HEAD_DIM, HEAD_DIM), F32),
                   S((BATCH, N_HEADS, HEAD_DIM), F32), S((BATCH, 1, LANES), F32)],
        scratch_shapes=[pltpu.VMEM((N_HEADS, HEAD_DIM, HEAD_DIM), F32), pltpu.VMEM((N_HEADS, HEAD_DIM), F32),
                        pltpu.VMEM((1, LANES), F32)],
        compiler_params=_cparams(("parallel", "arbitrary")),
        name="mlstm_prompt",
    )(ml, gates, gate_bias_row, tril)


def _unit_lower_inverse(a, L):
    r = lax.broadcasted_iota(I32, (L, L), 0)
    c = lax.broadcasted_iota(I32, (L, L), 1)
    p = jnp.where(r == c, 1.0, 0.0) - a
    x = a
    power = 1
    while 2 * power < L:
        x = _dot(x, x)
        p = p + _dot(p, x)
        power *= 2
    return p


def _gdn_prompt_kernel(x_ref, g_ref, gb_ref, al_ref, cw_ref, tril_ref, o_ref, s1_ref, cv_ref, s_s, xe_s):
    L = CHUNK
    R = STEP_ROWS
    nq = 3 * GROUP_W
    ci = pl.program_id(1)

    @pl.when(ci == 0)
    def _():
        s_s[...] = jnp.zeros_like(s_s)
        xe_s[0:8, :] = jnp.zeros((8, nq), F32)

    raw = x_ref[:, 0:nq]
    xe_s[8:8 + R, :] = raw
    conv = (cw_ref[3:4, :] * raw + cw_ref[2:3, :] * xe_s[7:7 + R, :]
            + cw_ref[1:2, :] * xe_s[6:6 + R, :] + cw_ref[0:1, :] * xe_s[5:5 + R, :])
    xe_s[0:8, :] = raw[R - 8:R, :]
    qkv = conv * _sigmoid(conv)

    g = g_ref[...] + gb_ref[...]
    lane = lax.broadcasted_iota(I32, (R, LANES), 1)
    beta_all = _sigmoid(g)
    gd = jnp.where((lane >= 3 * N_HEADS) & (lane < 4 * N_HEADS), -jnp.exp(al_ref[...]) * _softplus(g), 0.0)
    gcum = _tri_dot(tril_ref[...], gd)
    xt = gcum.T
    incl, strict = _chunk_masks(L)
    prob = [(s_i, h) for s_i in range(R // L) for h in range(N_HEADS)]
    loc = []
    for s_i, h in prob:
        rows = slice(s_i * L, (s_i + 1) * L)
        q = qkv[rows, _head_cols(0, h)]
        k = qkv[rows, _head_cols(1, h)]
        v = qkv[rows, _head_cols(2, h)]
        q = q * lax.rsqrt(jnp.sum(q * q, axis=1, keepdims=True) + EPS) * SCALE
        k = k * lax.rsqrt(jnp.sum(k * k, axis=1, keepdims=True) + EPS)
        b = beta_all[rows, 2 * N_HEADS + h:2 * N_HEADS + h + 1]
        gc = gcum[rows, 3 * N_HEADS + h:3 * N_HEADS + h + 1]
        gc_row = xt[3 * N_HEADS + h:3 * N_HEADS + h + 1, rows]
        decay = jnp.exp(jnp.where(incl, gc - gc_row, -jnp.inf))
        kb = k * b
        egc = jnp.exp(gc)
        gl = gc[L - 1:L, :]
        loc.append(dict(q=q, k=k, kb=kb, decay=decay, vb=v * b, kbe=kb * egc, q_dec=q * egc,
                        k_dec=k * jnp.exp(gl - gc), g_last=jnp.exp(gl)))
    a_mat = [jnp.where(strict, _dot_nt(d["kb"], d["k"]) * d["decay"], 0.0) for d in loc]
    qk_mat = [_dot_nt(d["q"], d["k"]) * d["decay"] for d in loc]
    eye = jnp.where(incl & jnp.logical_not(strict), 1.0, 0.0)
    pw = a_mat
    inv = [eye - a for a in a_mat]
    power = 1
    while 2 * power < L:
        pw = [_dot(x, x) for x in pw]
        inv = [p + _dot(p, x) for p, x in zip(inv, pw)]
        power *= 2
    us = [_dot(t, d["vb"]) for t, d in zip(inv, loc)]
    ws = [_dot(t, d["kbe"]) for t, d in zip(inv, loc)]
    s = [s_s[h] for h in range(N_HEADS)]
    heads = range(N_HEADS)
    for s_i in range(R // L):
        rows = slice(s_i * L, (s_i + 1) * L)
        p0 = s_i * N_HEADS
        w_s = [_dot(ws[p0 + h], s[h]) for h in heads]
        q_s = [_dot(loc[p0 + h]["q_dec"], s[h]) for h in heads]
        v_new = [us[p0 + h] - w_s[h] for h in heads]
        o = [q_s[h] + _dot(qk_mat[p0 + h], v_new[h]) for h in heads]
        s = [s[h] * loc[p0 + h]["g_last"] + _dot_tn(loc[p0 + h]["k_dec"], v_new[h]) for h in heads]
        for h in heads:
            gate = x_ref[rows, _head_cols(3, h)]
            o_ref[rows, h * HEAD_DIM:(h + 1) * HEAD_DIM] = _head_rms(o[h]) * (gate * _sigmoid(gate))
    for h in range(N_HEADS):
        s_s[h] = s[h]

    @pl.when(ci == pl.num_programs(1) - 1)
    def _():
        s1_ref[0] = s_s[...]
        cv_ref[0] = xe_s[8 - (CONV_W - 1):8, :]


def _gdn_prompt(gd, gates, gate_bias_row, a_log_row, conv_w, tril):
    L = STEP_ROWS
    nc = SEQ // L
    S = jax.ShapeDtypeStruct
    return pl.pallas_call(
        _gdn_prompt_kernel,
        grid=(BATCH, nc),
        in_specs=[pl.BlockSpec((L, 4 * GROUP_W), lambda b, c: (b * nc + c, 0)),
                  pl.BlockSpec((L, LANES), lambda b, c: (b * nc + c, 0)),
                  _const_spec((1, LANES)), _const_spec((1, LANES)), _const_spec((CONV_W, 3 * GROUP_W)),
                  _const_spec((L, L))],
        out_specs=[pl.BlockSpec((L, GROUP_W), lambda b, c: (b * nc + c, 0)),
                   pl.BlockSpec((1, N_HEADS, HEAD_DIM, HEAD_DIM), lambda b, c: (b, 0, 0, 0)),
                   pl.BlockSpec((1, CONV_W - 1, 3 * GROUP_W), lambda b, c: (b, 0, 0))],
        out_shape=[S((N_PROMPT, GROUP_W), F32), S((BATCH, N_HEADS, HEAD_DIM, HEAD_DIM), F32),
                   S((BATCH, CONV_W - 1, 3 * GROUP_W), F32)],
        scratch_shapes=[pltpu.VMEM((N_HEADS, HEAD_DIM, HEAD_DIM), F32), pltpu.VMEM((8 + L, 3 * GROUP_W), F32)],
        compiler_params=_cparams(("parallel", "arbitrary")),
        name="gdn_prompt",
    )(gd, gates, gate_bias_row, a_log_row, conv_w, tril)


def _rope_rows(x, cos, sin_signed):
    lane = lax.broadcasted_iota(I32, x.shape, 1)
    first = jnp.bitwise_and(lane, HEAD_DIM - 1) < HEAD_DIM // 2
    w = x.shape[1]
    swapped = jnp.where(first, pltpu.roll(x, w - HEAD_DIM // 2, 1), pltpu.roll(x, HEAD_DIM // 2, 1))
    return x * cos + swapped * sin_signed


def _ret_prompt_kernel(x_ref, cos_ref, sin_ref, dm_ref, xz_ref, gch_ref, o_ref, s1_ref, s_s):
    ci = pl.program_id(1)

    @pl.when(ci == 0)
    def _():
        s_s[...] = jnp.zeros_like(s_s)

    L = CHUNK
    cos = cos_ref[...]
    sin = sin_ref[...]
    qr = _rope_rows(x_ref[:, 0:GROUP_W], cos, sin)
    kr = _rope_rows(x_ref[:, GROUP_W:2 * GROUP_W], cos, sin) * SCALE
    prob = []
    for s_i in range(STEP_ROWS // L):
        rows = slice(s_i * L, (s_i + 1) * L)
        for h in range(N_HEADS):
            sl = slice(h * HEAD_DIM, (h + 1) * HEAD_DIM)
            prob.append(dict(rows=rows, h=h, sl=sl, q=qr[rows, sl], k=kr[rows, sl], v=x_ref[rows, _head_cols(2, h)]))
    qk = [_dot_nt(p["q"], p["k"]) * dm_ref[p["h"]] for p in prob]
    intra = [_dot(a, p["v"]) for a, p in zip(qk, prob)]
    kv = [_dot_tn(p["k"] * xz_ref[:, N_HEADS + p["h"]:N_HEADS + p["h"] + 1], p["v"]) for p in prob]
    s = [s_s[h] for h in range(N_HEADS)]
    s_in = []
    for p, kv_p in zip(prob, kv):
        s_in.append(s[p["h"]])
        s[p["h"]] = s[p["h"]] * gch_ref[p["h"]] + kv_p
    inter = [_dot(p["q"], s0) * xz_ref[:, p["h"]:p["h"] + 1] for p, s0 in zip(prob, s_in)]
    for p, a, b in zip(prob, intra, inter):
        gate = x_ref[p["rows"], _head_cols(3, p["h"])]
        o_ref[p["rows"], p["sl"]] = _head_rms(a + b) * (gate * _sigmoid(gate))
    for h in range(N_HEADS):
        s_s[h] = s[h]

    @pl.when(ci == pl.num_programs(1) - 1)
    def _():
        s1_ref[0] = s_s[...]


def _ret_prompt(rt, cos, sin, dmat, xz, gch):
    L = STEP_ROWS
    nc = SEQ // L
    S = jax.ShapeDtypeStruct
    return pl.pallas_call(
        _ret_prompt_kernel,
        grid=(BATCH, nc),
        in_specs=[pl.BlockSpec((L, 4 * GROUP_W), lambda b, c: (b * nc + c, 0)),
                  pl.BlockSpec((L, GROUP_W), lambda b, c: (c, 0)),
                  pl.BlockSpec((L, GROUP_W), lambda b, c: (c, 0)),
                  _const_spec((N_HEADS, CHUNK, CHUNK)), _const_spec((CHUNK, LANES)),
                  pl.BlockSpec(memory_space=pltpu.SMEM)],
        out_specs=[pl.BlockSpec((L, GROUP_W), lambda b, c: (b * nc + c, 0)),
                   pl.BlockSpec((1, N_HEADS, HEAD_DIM, HEAD_DIM), lambda b, c: (b, 0, 0, 0))],
        out_shape=[S((N_PROMPT, GROUP_W), F32), S((BATCH, N_HEADS, HEAD_DIM, HEAD_DIM), F32)],
        scratch_shapes=[pltpu.VMEM((N_HEADS, HEAD_DIM, HEAD_DIM), F32)],
        compiler_params=_cparams(("parallel", "arbitrary")),
        name="ret_prompt",
    )(rt, cos, sin, dmat, xz, gch)


def _gdn_decode_conv_kernel(raw_ref, c0_ref, cw_ref, qkvt_ref, cv_ref):
    raw = raw_ref[...]
    conv = (cw_ref[3:4, :] * raw + cw_ref[2:3, :] * c0_ref[2] + cw_ref[1:2, :] * c0_ref[1]
            + cw_ref[0:1, :] * c0_ref[0])
    qkv = conv * _sigmoid(conv)
    cv_ref[0] = c0_ref[1]
    cv_ref[1] = c0_ref[2]
    cv_ref[2] = raw
    for blk in range(3):
        for h in range(N_HEADS):
            lo = blk * GROUP_W + h * HEAD_DIM
            x = qkv[:, lo:lo + HEAD_DIM]
            if blk == 0:
                x = x * lax.rsqrt(jnp.sum(x * x, axis=1, keepdims=True) + EPS) * SCALE
            elif blk == 1:
                x = x * lax.rsqrt(jnp.sum(x * x, axis=1, keepdims=True) + EPS)
            if h % 2 == 0:
                pair = x
            else:
                qkvt_ref[lo - HEAD_DIM:lo + HEAD_DIM, :] = jnp.concatenate([pair, x], axis=1).T


def _gdn_decode_conv(raw, conv0, conv_w):
    S = jax.ShapeDtypeStruct
    return pl.pallas_call(
        _gdn_decode_conv_kernel,
        out_shape=[S((3 * GROUP_W, DEC_BATCH), F32), S((CONV_W - 1, DEC_BATCH, 3 * GROUP_W), F32)],
        compiler_params=_cparams(None),
        name="gdn_decode_conv",
    )(raw, conv0, conv_w)


def _decode_rec_kernel(sc_ref, mlq_ref, mlk_ref, mlv_ref, mlo_ref, gq_ref, gk_ref, gv_ref, gg_ref,
                       rq_ref, rk_ref, rv_ref, rg_ref, gt_ref, cos_ref, sin_ref,
                       c0_ref, n0_ref, m0_ref, sg0_ref, sr0_ref,
                       oml_ref, ogd_ref, ort_ref, c1_ref, n1_ref, m1_ref, sg1_ref, sr1_ref, va_s, vb_s):
    h = pl.program_id(0)
    D = HEAD_DIM
    sum0 = lambda x: jnp.sum(x, axis=0, keepdims=True)
    rms0 = lambda x: x * lax.rsqrt(sum0(x * x) * (1.0 / D) + EPS)

    li = gt_ref[pl.ds(h, 1), :] + sc_ref[0, h]
    lf = _log_sigmoid(gt_ref[pl.ds(N_HEADS + h, 1), :] + sc_ref[1, h])
    m0 = m0_ref[0]
    q = mlq_ref[...]
    k = mlk_ref[...] * SCALE
    v = mlv_ref[...]
    log_inter = lf + m0
    m_row = jnp.maximum(log_inter, li)
    s = sum0(q * k) * jnp.exp(li - m_row)
    w_inter = jnp.exp(log_inter - m_row)
    decay = jnp.exp(lf + m0 - m_row)
    kw = k * jnp.exp(li - m_row)

    va_s[...] = kw

    def ml_body(d, qc):
        c_d = c0_ref[0, d]
        c1_ref[0, d] = decay * c_d + va_s[pl.ds(d, 1), :] * v
        return qc + mlq_ref[pl.ds(d, 1), :] * c_d

    n0 = n0_ref[0]
    n1_ref[0] = decay * n0 + kw
    qc = lax.fori_loop(0, D, ml_body, jnp.zeros((D, LANES), F32))
    num = s * v + w_inter * qc
    den = s + w_inter * sum0(q * n0)
    hh = num / jnp.maximum(jnp.abs(den), jnp.exp(-m_row))
    m1_ref[0] = m_row
    oml_ref[...] = rms0(hh) * _sigmoid(mlo_ref[...])

    beta = _sigmoid(gt_ref[pl.ds(2 * N_HEADS + h, 1), :])
    gdec = -sc_ref[3, h] * _softplus(gt_ref[pl.ds(3 * N_HEADS + h, 1), :] + sc_ref[2, h])
    eg = jnp.exp(gdec)
    gv = gv_ref[...]

    def ks_body(d, acc):
        return acc + gk_ref[pl.ds(d, 1), :] * sg0_ref[0, d]

    ks = lax.fori_loop(0, D, ks_body, jnp.zeros((D, LANES), F32))
    v_new = beta * gv - (beta * eg) * ks

    def gd_body(d, acc):
        s_new = eg * sg0_ref[0, d] + gk_ref[pl.ds(d, 1), :] * v_new
        sg1_ref[0, d] = s_new
        return acc + gq_ref[pl.ds(d, 1), :] * s_new

    og = lax.fori_loop(0, D, gd_body, jnp.zeros((D, LANES), F32))
    gate = gg_ref[...]
    ogd_ref[...] = rms0(og) * (gate * _sigmoid(gate))

    half = D // 2

    def rope(ref):
        x = ref[...]
        sw = jnp.concatenate([x[half:, :], x[:half, :]], axis=0)
        return x * cos_ref[...] + sw * sin_ref[...]

    va_s[...] = rope(rq_ref)
    vb_s[...] = rope(rk_ref) * SCALE
    rv = rv_ref[...]
    gamma = sc_ref[4, h]

    def rt_body(d, acc):
        s_new = gamma * sr0_ref[0, d] + vb_s[pl.ds(d, 1), :] * rv
        sr1_ref[0, d] = s_new
        return acc + va_s[pl.ds(d, 1), :] * s_new

    ort = lax.fori_loop(0, D, rt_body, jnp.zeros((D, LANES), F32))
    gate = rg_ref[...]
    ort_ref[...] = rms0(ort) * (gate * _sigmoid(gate))


def _decode_rec(scalars, pt, qkvt, cos_t, sin_t, c0, n0, m0, sg0, sr0):
    D = HEAD_DIM
    S = jax.ShapeDtypeStruct
    prow = lambda col, blk: pl.BlockSpec((D, LANES), lambda h: ((col + blk * GROUP_W) // D + h, 0))
    vec = lambda blk: pl.BlockSpec((D, LANES), lambda h: (blk * N_HEADS + h, 0))
    st4 = pl.BlockSpec((1, D, D, LANES), lambda h: (h, 0, 0, 0))
    st3 = pl.BlockSpec((1, D, LANES), lambda h: (h, 0, 0))
    st2 = pl.BlockSpec((1, 1, LANES), lambda h: (h, 0, 0))
    in_specs = ([pl.BlockSpec(memory_space=pltpu.SMEM)]
                + [prow(COL_ML, blk) for blk in range(4)]
                + [vec(0), vec(1), vec(2), prow(COL_GD, 3)]
                + [prow(COL_RT, blk) for blk in range(4)]
                + [pl.BlockSpec((LANES, LANES), lambda h: (COL_GATES // LANES, 0)),
                   _const_spec((D, LANES)), _const_spec((D, LANES)), st4, st3, st2, st4, st4])
    return pl.pallas_call(
        _decode_rec_kernel,
        grid=(N_HEADS,),
        in_specs=in_specs,
        out_specs=[vec(0), vec(0), vec(0), st4, st3, st2, st4, st4],
        out_shape=[S((GROUP_W, LANES), F32)] * 3
        + [S((N_HEADS, D, D, LANES), F32), S((N_HEADS, D, LANES), F32), S((N_HEADS, 1, LANES), F32),
           S((N_HEADS, D, D, LANES), F32), S((N_HEADS, D, D, LANES), F32)],
        scratch_shapes=[pltpu.VMEM((D, LANES), F32), pltpu.VMEM((D, LANES), F32)],
        compiler_params=_cparams(("parallel",)),
        name="decode_rec",
    )(scalars, pt, pt, pt, pt, qkvt, qkvt, qkvt, pt, pt, pt, pt, pt, pt, cos_t, sin_t, c0, n0, m0, sg0, sr0)


def _outproj_kernel(x_ref, osb_ref, oml_ref, ogd_ref, ort_ref, gain_ref, wo_ref, nw_ref, rwh_ref, rwl_ref,
                    x1_ref, h2_ref, lg_ref, *, transposed):
    parts = [osb_ref[...]]
    for ref in (oml_ref, ogd_ref, ort_ref):
        parts.append(ref[...].T if transposed else ref[...])
    y = None
    for g, p in enumerate(parts):
        cols = slice(g * GROUP_W, (g + 1) * GROUP_W)
        t = _dot(p * gain_ref[:, cols], wo_ref[cols, :])
        y = t if y is None else y + t
    x1 = x_ref[...] + y
    h2 = _norm_rows(x1, nw_ref[...])
    x1_ref[...] = x1
    h2_ref[...] = h2
    lg_ref[...] = _gates_dot(h2, rwh_ref, rwl_ref)


def _outproj(x, osb, oml, ogd, ort, gain, wo, nw, rwh, rwl, transposed):
    n = x.shape[0]
    tm = min(TM_PROMPT, n)
    S = jax.ShapeDtypeStruct
    row = lambda c: pl.BlockSpec((tm, c), lambda i: (i, 0))
    mix = _const_spec((GROUP_W, DEC_BATCH)) if transposed else row(GROUP_W)
    return pl.pallas_call(
        functools.partial(_outproj_kernel, transposed=transposed),
        grid=(n // tm,),
        in_specs=[row(D_MODEL), row(GROUP_W), mix, mix, mix, _const_spec((1, D_MODEL)),
                  _const_spec((D_MODEL, D_MODEL)), _const_spec((1, D_MODEL)),
                  _const_spec((D_MODEL, LANES)), _const_spec((D_MODEL, LANES))],
        out_specs=[row(D_MODEL), row(D_MODEL), row(LANES)],
        out_shape=[S((n, D_MODEL), F32), S((n, D_MODEL), F32), S((n, LANES), F32)],
        compiler_params=_cparams(("parallel",)),
        name="outproj",
    )(x, osb, oml, ogd, ort, gain, wo, nw, rwh, rwl)


def _route_kernel(lg_ref, rb_ref, tri_ref, lt_ref, cin_ref, pr_ref, rf_ref, ti_ref, cnt_ref):
    @pl.when(pl.program_id(0) == 0)
    def _():
        cnt_ref[...] = cin_ref[...]

    lg = lg_ref[...] + rb_ref[...]
    lane = lax.broadcasted_iota(I32, lg.shape, 1)
    ninf = -jnp.inf
    big = LANES - 1
    rmax = lambda x: jnp.max(x, axis=1, keepdims=True)
    rmin = lambda x: jnp.min(x, axis=1, keepdims=True)
    gl = jnp.where(lane < N_GROUPS, lg, ninf)
    gmax = rmax(gl)
    g_sel = rmin(jnp.where(gl == gmax, lane, big))
    g_prob = 1.0 / jnp.sum(jnp.exp(gl - gmax), axis=1, keepdims=True)
    e_lane = (lane >= N_GROUPS) & (lane < N_GROUPS + N_EXPERTS)
    em = jnp.where(e_lane & (jnp.right_shift(lane - N_GROUPS, 3) == g_sel), lg, ninf)
    v1 = rmax(em)
    i1 = rmin(jnp.where(em == v1, lane, big))
    em2 = jnp.where(lane == i1, ninf, em)
    v2 = rmax(em2)
    i2 = rmin(jnp.where(em2 == v2, lane, big))
    t = jnp.exp(v2 - v1)
    gate1 = g_prob / (1.0 + t)
    gate2 = g_prob * t / (1.0 + t)
    e1 = i1 - N_GROUPS
    e2 = i2 - N_GROUPS
    onehot = jnp.where((lane == e1) | (lane == e2), 1.0, 0.0)
    before = _dot(tri_ref[...], onehot)
    tile_cnt = jnp.sum(onehot, axis=0, keepdims=True)
    padded = jnp.floor((tile_cnt + (RUN_ALIGN - 1)) * (1.0 / RUN_ALIGN)) * RUN_ALIGN
    run_start = _dot(jnp.broadcast_to(padded, (8, LANES)), lt_ref[...])[0:1, :]
    pos = before + run_start
    p1 = jnp.sum(jnp.where(lane == e1, pos, 0.0), axis=1, keepdims=True)
    p2 = jnp.sum(jnp.where(lane == e2, pos, 0.0), axis=1, keepdims=True)
    pr = jnp.where(lane == 0, p1, jnp.where(lane == 1, p2, 0.0)).astype(I32)
    pr_ref[...] = pr.T[0:8, :]
    rf_ref[...] = jnp.where(lane == 0, gate1, jnp.where(lane == 1, gate2, jnp.where(lane == 2, p1,
                            jnp.where(lane == 3, p2, 0.0))))
    row8 = lax.broadcasted_iota(I32, (8, LANES), 0)
    ti_ref[...] = jnp.where(row8 == 0, padded, jnp.where(row8 == 1, cnt_ref[...],
                            jnp.where(row8 == 2, run_start, 0.0))).astype(I32)
    cnt_ref[...] += padded


def _route(logits, rb, tri, lanes_lt, cnt_in):
    n = logits.shape[0]
    tm = tri.shape[0]
    S = jax.ShapeDtypeStruct
    row = pl.BlockSpec((tm, LANES), lambda i: (i, 0))
    return pl.pallas_call(
        _route_kernel,
        grid=(n // tm,),
        in_specs=[row, _const_spec((1, LANES)), _const_spec((tm, tm)), _const_spec((LANES, LANES)),
                  _const_spec((1, LANES))],
        out_specs=[pl.BlockSpec((8, tm), lambda i: (0, i)), row, pl.BlockSpec((8, LANES), lambda i: (i, 0)),
                   _const_spec((1, LANES))],
        out_shape=[S((8, n), I32), S((n, LANES), F32), S((8 * (n // tm), LANES), I32), S((1, LANES), F32)],
        compiler_params=_cparams(("arbitrary",)),
        name="route",
    )(logits, rb, tri, lanes_lt, cnt_in)


def _packed_rows(tm):
    return 2 * tm + N_EXPERTS * RUN_ALIGN


def _for_each_run_chunk(plan_refs, fn):
    offs_ref, len_ref, before_ref, start_ref = plan_refs
    base = pl.program_id(0) * N_EXPERTS

    def per_expert(e, _):
        n_chunks = len_ref[base + e] // RUN_ALIGN
        packed0 = start_ref[base + e]
        slot0 = offs_ref[e] + before_ref[base + e]

        def per_chunk(c, _):
            fn(pl.multiple_of(packed0 + c * RUN_ALIGN, RUN_ALIGN), pl.multiple_of(slot0 + c * RUN_ALIGN, RUN_ALIGN))
            return 0

        lax.fori_loop(0, n_chunks, per_chunk, 0)
        return 0

    lax.fori_loop(0, N_EXPERTS, per_expert, 0)


def _dispatch_kernel(offs_ref, len_ref, before_ref, start_ref, h_ref, pr_ref, xs_in_ref, xs_ref, pk_s, sem):
    del xs_in_ref
    plan_refs = (offs_ref, len_ref, before_ref, start_ref)
    tm = h_ref.shape[0]
    slot = lax.broadcasted_iota(I32, (_packed_rows(tm), tm), 0)
    sel = jnp.where((slot == pr_ref[0:1, :]) | (slot == pr_ref[1:2, :]), 1.0, 0.0)
    pk_s[...] = _dot(sel, h_ref[...])

    def copy(packed_row, slot_row):
        return pltpu.make_async_copy(pk_s.at[pl.ds(packed_row, RUN_ALIGN), :],
                                     xs_ref.at[pl.ds(slot_row, RUN_ALIGN), :], sem.at[0])

    _for_each_run_chunk(plan_refs, lambda p, s: copy(p, s).start())
    _for_each_run_chunk(plan_refs, lambda p, s: copy(p, s).wait())


def _dispatch(plan, h2, pr, xs):
    n = h2.shape[0]
    tm = min(TM_ROUTE, n)
    return pl.pallas_call(
        _dispatch_kernel,
        grid_spec=pltpu.PrefetchScalarGridSpec(
            num_scalar_prefetch=4,
            grid=(n // tm,),
            in_specs=[pl.BlockSpec((tm, D_MODEL), lambda i, *_: (i, 0)),
                      pl.BlockSpec((8, tm), lambda i, *_: (0, i)),
                      pl.BlockSpec(memory_space=pl.ANY)],
            out_specs=pl.BlockSpec(memory_space=pl.ANY),
            scratch_shapes=[pltpu.VMEM((_packed_rows(tm), D_MODEL), F32), pltpu.SemaphoreType.DMA((1,))]),
        out_shape=jax.ShapeDtypeStruct(xs.shape, xs.dtype),
        input_output_aliases={6: 0},
        compiler_params=_cparams(("arbitrary",)),
        name="dispatch",
    )(*plan, h2, pr, xs)


def _experts_kernel(te_ref, tv_ref, xs_ref, wg_ref, wu_ref, wd_ref, ys_ref, wg_s, wu_s, wd_s):
    i = pl.program_id(0)
    valid = tv_ref[i] > 0
    fresh = (i == 0) | (te_ref[i] != te_ref[jnp.maximum(i - 1, 0)])

    @pl.when(valid & fresh)
    def _():
        wg_s[...] = wg_ref[0].astype(BF16)
        wu_s[...] = wu_ref[0].astype(BF16)
        wd_s[...] = wd_ref[0].astype(BF16)

    @pl.when(valid)
    def _():
        x = xs_ref[...].astype(BF16)
        a = _dot(x, wg_s[...])
        u = _dot(x, wu_s[...])
        act = a * _sigmoid(a) * u
        ys_ref[...] = _dot(act, wd_s[...])

    @pl.when(jnp.logical_not(valid))
    def _():
        ys_ref[...] = jnp.zeros_like(ys_ref)


def _experts(tile_expert, tile_valid, xs, wg, wu, wd, layer):
    tm = TM_EXPERT
    wspec = lambda a, b: pl.BlockSpec((None, 1, a, b), lambda i, te, tv: (layer, te[i], 0, 0))
    return pl.pallas_call(
        _experts_kernel,
        grid_spec=pltpu.PrefetchScalarGridSpec(
            num_scalar_prefetch=2,
            grid=(N_TILES,),
            in_specs=[pl.BlockSpec((tm, D_MODEL), lambda i, te, tv: (i, 0)),
                      wspec(D_MODEL, D_EXPERT), wspec(D_MODEL, D_EXPERT), wspec(D_EXPERT, D_MODEL)],
            out_specs=pl.BlockSpec((tm, D_MODEL), lambda i, te, tv: (i, 0)),
            scratch_shapes=[pltpu.VMEM((D_MODEL, D_EXPERT), BF16), pltpu.VMEM((D_MODEL, D_EXPERT), BF16),
                            pltpu.VMEM((D_EXPERT, D_MODEL), BF16)]),
        out_shape=jax.ShapeDtypeStruct((N_SLOTS, D_MODEL), F32),
        compiler_params=_cparams(("arbitrary",)),
        name="experts",
    )(tile_expert, tile_valid, xs, wg, wu, wd)


def _combine_kernel(offs_ref, len_ref, before_ref, start_ref, x1_ref, rf_ref, fw_ref, ys_ref, out_ref,
                    pk_s, sem, *, final):
    plan_refs = (offs_ref, len_ref, before_ref, start_ref)
    tm = x1_ref.shape[0]

    @pl.when(pl.program_id(0) == 0)
    def _():
        pk_s[...] = jnp.zeros_like(pk_s)

    def copy(packed_row, slot_row):
        return pltpu.make_async_copy(ys_ref.at[pl.ds(slot_row, RUN_ALIGN), :],
                                     pk_s.at[pl.ds(packed_row, RUN_ALIGN), :], sem.at[0])

    _for_each_run_chunk(plan_refs, lambda p, s: copy(p, s).start())
    _for_each_run_chunk(plan_refs, lambda p, s: copy(p, s).wait())
    slot = lax.broadcasted_iota(I32, (tm, _packed_rows(tm)), 1)
    packed = pk_s[...].astype(BF16)
    x2 = x1_ref[...]
    for k in range(2):
        sel = jnp.where(slot == rf_ref[:, 2 + k:3 + k].astype(I32), 1.0, 0.0)
        x2 = x2 + rf_ref[:, k:k + 1] * _dot(sel, packed)
    out_ref[...] = _norm_rows(x2, fw_ref[...]) if final else x2


def _combine(plan, x1, rf, fw, ys, final):
    n = x1.shape[0]
    tm = min(TM_ROUTE, n)
    row = lambda c: pl.BlockSpec((tm, c), lambda i, *_: (i, 0))
    return pl.pallas_call(
        functools.partial(_combine_kernel, final=final),
        grid_spec=pltpu.PrefetchScalarGridSpec(
            num_scalar_prefetch=4,
            grid=(n // tm,),
            in_specs=[row(D_MODEL), row(LANES), pl.BlockSpec((1, D_MODEL), lambda i, *_: (0, 0)),
                      pl.BlockSpec(memory_space=pl.ANY)],
            out_specs=row(D_MODEL),
            scratch_shapes=[pltpu.VMEM((_packed_rows(tm), D_MODEL), F32), pltpu.SemaphoreType.DMA((1,))]),
        out_shape=jax.ShapeDtypeStruct((n, D_MODEL), F32),
        compiler_params=_cparams(("arbitrary",)),
        name="combine",
    )(*plan, x1, rf, fw, ys)


def _tri(n, kind):
    r = np.arange(n)[:, None]
    c = np.arange(n)[None, :]
    m = {"chunk_lower_incl": (c <= r) & (r // CHUNK == c // CHUNK),
         "lower_strict": c < r,
         "row_gt_col": r > c,
         "row_lt_col": r < c,
         "later_page": (r % 8 == c % 8) & (c // 8 > r // 8)}[kind]
    return jnp.asarray(m.astype(np.float32), dtype=BF16)


def _rope_tables(pos):
    half = HEAD_DIM // 2
    inv = ROPE_BASE ** (-np.arange(half, dtype=np.float64) / half)
    ang = np.asarray(pos, np.float64)[:, None] * inv[None, :]
    cos = np.concatenate([np.cos(ang), np.cos(ang)], axis=1)
    sin = np.concatenate([-np.sin(ang), np.sin(ang)], axis=1)
    return cos.astype(np.float32), sin.astype(np.float32)


def _retention_consts(L):
    log_gamma = np.log1p(-np.exp2(-5.0 - np.arange(N_HEADS, dtype=np.float64)))
    idx = np.arange(L, dtype=np.float64)
    diff = np.maximum(idx[:, None] - idx[None, :], 0.0)
    dmat = np.where(idx[None, :] <= idx[:, None], np.exp(log_gamma[:, None, None] * diff), 0.0)
    xi = np.exp(log_gamma[:, None] * (idx + 1.0))
    zeta = np.exp(log_gamma[:, None] * (L - 1.0 - idx))
    xz = np.zeros((L, LANES), np.float64)
    xz[:, 0:N_HEADS] = xi.T
    xz[:, N_HEADS:2 * N_HEADS] = zeta.T
    return (jnp.asarray(dmat, F32), jnp.asarray(xz, F32), jnp.asarray(np.exp(log_gamma * L), F32),
            np.exp(log_gamma))


def _hi_lo(w):
    hi = w.astype(BF16)
    return hi, (w - hi.astype(F32)).astype(BF16)


def _pad_lanes(w):
    return jnp.pad(w, ((0, 0), (0, LANES - w.shape[1])))


def _prep_w_in(w):
    ml0 = 3 * GROUP_W
    mlg = ml0 + 4 * GROUP_W
    gd0 = mlg + 2 * N_HEADS
    gdg = gd0 + 4 * GROUP_W
    rt0 = gdg + 2 * N_HEADS
    main = jnp.concatenate([w[:, 0:ml0], w[:, ml0:mlg], w[:, gd0:gdg], w[:, rt0:]], axis=1).astype(BF16)
    gates = _pad_lanes(jnp.concatenate([w[:, mlg:gd0], w[:, gdg:rt0]], axis=1))
    return (main,) + _hi_lo(gates)


def _tile_plan(tile_info):
    t = tile_info.reshape(-1, 8, LANES)[:, :, :N_EXPERTS]
    return t[:, 0].reshape(-1), t[:, 1].reshape(-1), t[:, 2].reshape(-1)


def _dispatch_plan(cnt):
    ntile = (cnt + TM_EXPERT - 1) // TM_EXPERT
    tile_end = jnp.cumsum(ntile)
    offs = (tile_end - ntile) * TM_EXPERT
    tid = jnp.arange(N_TILES, dtype=I32)
    te = jnp.minimum(jnp.sum(tid[:, None] >= tile_end[None, :], axis=1), N_EXPERTS - 1).astype(I32)
    tv = (tid < tile_end[-1]).astype(I32)
    return offs.astype(I32), te, tv


def kernel(x_prompt, x_sample, cache_sb_k, cache_sb_v, state_mlstm_c, state_mlstm_n, state_mlstm_m, state_gdn_s, state_gdn_conv, state_ret_s, page_table, norm_attn_w, w_in, sb_logit_bias, mlstm_gate_bias, gdn_conv_w, gdn_a_log, gdn_dt_bias, head_norm_w, w_out, norm_ffn_w, router_group_w, router_group_b, router_expert_w, router_expert_b, expert_w_gate, expert_w_up, expert_w_down, final_norm_w):
    D = HEAD_DIM
    xp = x_prompt.reshape(N_PROMPT, D_MODEL)
    xd = x_sample.reshape(DEC_BATCH, D_MODEL)
    n_pool = cache_sb_k.shape[1]
    cache_kt = cache_sb_k.transpose(0, 1, 3, 4, 2).reshape(DEPTH, n_pool, GROUP_W, PAGE_SIZE)
    cache_vt = cache_sb_v.transpose(0, 1, 3, 4, 2).reshape(DEPTH, n_pool, GROUP_W, PAGE_SIZE)

    tril = _tri(STEP_ROWS, "chunk_lower_incl")
    tri_sb = _tri(SB_BLOCK, "row_gt_col")
    tri_page = _tri(PAGE_SIZE, "row_gt_col")
    later_pages = _tri(8 * N_PAGES, "later_page")
    lanes_lt = _tri(LANES, "row_lt_col")
    tri_rp = _tri(TM_ROUTE, "lower_strict")
    tri_rd = _tri(DEC_BATCH, "lower_strict")
    cos_p, sin_p = _rope_tables(np.arange(SEQ))
    cos_p = jnp.asarray(np.tile(cos_p, (1, N_HEADS)))
    sin_p = jnp.asarray(np.tile(sin_p, (1, N_HEADS)))
    cos_d, sin_d = _rope_tables([PAST_LEN])
    cos_d = jnp.asarray(np.tile(cos_d.T, (1, LANES)))
    sin_d = jnp.asarray(np.tile(sin_d.T, (1, LANES)))
    dmat, xz, gch, gamma = _retention_consts(CHUNK)
    zeros4 = jnp.zeros((N_HEADS,), F32)

    outs = {k: [] for k in ("kp", "vp", "ks", "vs", "cp", "np", "mp", "cs", "ns", "ms", "gp", "gcp", "gs", "gcs",
                            "rp", "rs")}
    yp = yd = None
    for l in range(DEPTH):
        w_main, wg_hi, wg_lo = _prep_w_in(w_in[l])
        nw = norm_attn_w[l][None, :]
        gate_bias_row = _pad_lanes(jnp.concatenate([mlstm_gate_bias[l], zeros4, gdn_dt_bias[l]])[None, :])
        a_log_row = _pad_lanes(jnp.concatenate([zeros4, zeros4, zeros4, gdn_a_log[l]])[None, :])
        bias = sb_logit_bias[l]

        sbq, sbk, sbv, ml, gd, rt, gt = _inproj_prompt(xp, nw, w_main, wg_hi, wg_lo)
        osb_p = _sb_prompt(sbq, sbk, sbv, bias, tri_sb)
        oml_p, c1p, n1p, m1p = _mlstm_prompt(ml, gt, gate_bias_row, tril)
        ogd_p, s1p, cv1p = _gdn_prompt(gd, gt, gate_bias_row, a_log_row, gdn_conv_w[l], tril)
        ort_p, r1p = _ret_prompt(rt, cos_p, sin_p, dmat, xz, gch)

        sbq_d, gdraw_d, pt = _inproj_decode(xd, nw, w_main, wg_hi, wg_lo)
        osb_d = _sb_decode(sbq_d, cache_kt, cache_vt, page_table, bias, tri_page, later_pages, l)
        qkvt, cv1d = _gdn_decode_conv(gdraw_d, state_gdn_conv[l].transpose(1, 0, 2), gdn_conv_w[l])
        scalars = jnp.stack([mlstm_gate_bias[l][:N_HEADS], mlstm_gate_bias[l][N_HEADS:], gdn_dt_bias[l],
                             jnp.exp(gdn_a_log[l]), jnp.asarray(gamma, F32), zeros4, zeros4, zeros4])
        oml_d, ogd_d, ort_d, c1d, n1d, m1d, s1d, r1d = _decode_rec(
            scalars, pt, qkvt, cos_d, sin_d,
            state_mlstm_c[l].transpose(1, 2, 3, 0), state_mlstm_n[l].transpose(1, 2, 0),
            state_mlstm_m[l].T.reshape(N_HEADS, 1, DEC_BATCH),
            state_gdn_s[l].transpose(1, 2, 3, 0), state_ret_s[l].transpose(1, 2, 3, 0))

        gain = head_norm_w[l][None, :]
        wo = w_out[l].astype(BF16)
        nfw = norm_ffn_w[l][None, :]
        rw_hi, rw_lo = _hi_lo(_pad_lanes(jnp.concatenate([router_group_w[l], router_expert_w[l]], axis=1)))
        rb = _pad_lanes(jnp.concatenate([router_group_b[l], router_expert_b[l]])[None, :])
        x1p, h2p, lgp = _outproj(xp, osb_p, oml_p, ogd_p, ort_p, gain, wo, nfw, rw_hi, rw_lo, False)
        x1d, h2d, lgd = _outproj(xd, osb_d, oml_d, ogd_d, ort_d, gain, wo, nfw, rw_hi, rw_lo, True)
        prp, rfp, tip, cnt_p = _route(lgp, rb, tri_rp, lanes_lt, jnp.zeros((1, LANES), F32))
        prd, rfd, tid, cnt = _route(lgd, rb, tri_rd, lanes_lt, cnt_p)

        offs, te, tv = _dispatch_plan(cnt[0, :N_EXPERTS].astype(I32))
        plan_p = (offs,) + _tile_plan(tip)
        plan_d = (offs,) + _tile_plan(tid)
        xs = _dispatch(plan_p, h2p, prp, jnp.zeros((N_SLOTS, D_MODEL), F32))
        xs = _dispatch(plan_d, h2d, prd, xs)
        ys = _experts(te, tv, xs, expert_w_gate, expert_w_up, expert_w_down, l)
        final = l == DEPTH - 1
        fw = final_norm_w[None, :]
        xp_next = _combine(plan_p, x1p, rfp, fw, ys, final)
        xd_next = _combine(plan_d, x1d, rfd, fw, ys, final)
        if final:
            yp, yd = xp_next, xd_next
        else:
            xp, xd = xp_next, xd_next

        heads_p = lambda a: a.reshape(BATCH, SEQ, N_HEADS, D)
        heads_t = lambda a: a.reshape(N_HEADS, D, DEC_BATCH).transpose(2, 0, 1)[:, None]
        outs["kp"].append(heads_p(sbk))
        outs["vp"].append(heads_p(sbv))
        outs["ks"].append(heads_t(pt[GROUP_W:2 * GROUP_W]))
        outs["vs"].append(heads_t(pt[2 * GROUP_W:3 * GROUP_W]))
        outs["cp"].append(c1p)
        outs["np"].append(n1p)
        outs["mp"].append(m1p[:, 0, :N_HEADS])
        outs["cs"].append(c1d.transpose(3, 0, 1, 2))
        outs["ns"].append(n1d.transpose(2, 0, 1))
        outs["ms"].append(m1d[:, 0, :].T)
        outs["gp"].append(s1p)
        outs["gcp"].append(cv1p)
        outs["gs"].append(s1d.transpose(3, 0, 1, 2))
        outs["gcs"].append(cv1d.transpose(1, 0, 2))
        outs["rp"].append(r1p)
        outs["rs"].append(r1d.transpose(3, 0, 1, 2))

    st = lambda k: jnp.stack(outs[k], axis=0)
    return (yp.reshape(BATCH, SEQ, D_MODEL), yd.reshape(DEC_BATCH, 1, D_MODEL),
            st("kp"), st("vp"), st("ks"), st("vs"),
            st("cp"), st("np"), st("mp"), st("cs"), st("ns"), st("ms"),
            st("gp"), st("gcp"), st("gs"), st("gcs"), st("rp"), st("rs"))
```

```python
import functools
import math

import numpy as np
import jax
import jax.numpy as jnp
from jax import lax
from jax.experimental import pallas as pl
from jax.experimental.pallas import tpu as pltpu

F32 = jnp.float32
BF16 = jnp.bfloat16
I32 = jnp.int32

D_MODEL = 1024
BATCH = 8
SEQ = 2048
DEPTH = 2
DEC_BATCH = 128
PAST_LEN = 2048
PAGE_SIZE = 128
N_PAGES = PAST_LEN // PAGE_SIZE
HEAD_DIM = 64
N_HEADS = 4
GROUP_W = N_HEADS * HEAD_DIM
CHUNK = 128
GDN_CHUNK = 64
CONV_W = 4
N_GROUPS = 4
EXPERTS_PER_GROUP = 8
N_EXPERTS = N_GROUPS * EXPERTS_PER_GROUP
D_EXPERT = D_MODEL // 2
ROPE_BASE = 10000.0
EPS = 1e-6
SCALE = HEAD_DIM ** -0.5

N_PROMPT = BATCH * SEQ
LANES = 128
N_MAIN = 3 * GROUP_W + 3 * 4 * GROUP_W
N_PROJ = N_MAIN + LANES
COL_ML = 3 * GROUP_W
COL_GD = COL_ML + 4 * GROUP_W
COL_RT = COL_GD + 4 * GROUP_W
COL_GATES = N_MAIN

MLSTM_STEP = 2 * CHUNK
GDN_STEP = 4 * GDN_CHUNK
RET_STEP = 4 * CHUNK
SB_BLOCK = 256
TM_PROMPT = 256
TM_EXPERT = 512
N_ASSIGN = 2 * (N_PROMPT + DEC_BATCH)
TM_ROUTE = 512
RUN_ALIGN = 8
N_TOKEN_TILES = N_PROMPT // TM_ROUTE + 1
N_TILES = -(-(N_ASSIGN + N_TOKEN_TILES * N_EXPERTS * (RUN_ALIGN - 1)) // TM_EXPERT) + N_EXPERTS
N_SLOTS = N_TILES * TM_EXPERT
VMEM_LIMIT = 48 * 1024 * 1024

_NT = (((1,), (1,)), ((), ()))
_TN = (((0,), (0,)), ((), ()))


_NN = (((1,), (0,)), ((), ()))


def _mm(a, b, dims):
    return lax.dot_general(a.astype(BF16), b.astype(BF16), dims, preferred_element_type=F32)


def _dot(a, b):
    return _mm(a, b, _NN)


def _dot_nt(a, b):
    return _mm(a, b, _NT)


def _dot_tn(a, b):
    return _mm(a, b, _TN)


def _split3(x):
    x1 = x.astype(BF16)
    r1 = x - x1.astype(F32)
    x2 = r1.astype(BF16)
    x3 = (r1 - x2.astype(F32)).astype(BF16)
    return x1, x2, x3


def _tri_dot(tri, x):
    x1, x2, x3 = _split3(x)
    return _dot(tri, x1) + _dot(tri, x2) + _dot(tri, x3)


def _log_sigmoid(z):
    return jnp.minimum(z, 0.0) - jnp.log(1.0 + jnp.exp(-jnp.abs(z)))


def _softplus(z):
    return jnp.maximum(z, 0.0) + jnp.log(1.0 + jnp.exp(-jnp.abs(z)))


def _sigmoid(z):
    return 1.0 / (1.0 + jnp.exp(-z))


def _head_rms(x):
    return x * lax.rsqrt(jnp.mean(x * x, axis=-1, keepdims=True) + EPS)


def _cparams(sem):
    return pltpu.CompilerParams(dimension_semantics=sem, vmem_limit_bytes=VMEM_LIMIT)


def _const_spec(shape):
    nd = len(shape)
    return pl.BlockSpec(shape, lambda *_: (0,) * nd)


def _norm_rows(x, w):
    return x * lax.rsqrt(jnp.mean(x * x, axis=-1, keepdims=True) + EPS) * w


def _gates_dot(h, wgh_ref, wgl_ref):
    hb = h.astype(BF16)
    hl = (h - hb.astype(F32)).astype(BF16)
    return _dot(hb, wgh_ref[...]) + _dot(hl, wgh_ref[...]) + _dot(hb, wgl_ref[...])


def _inproj_prompt_kernel(x_ref, nw_ref, w_ref, wgh_ref, wgl_ref,
                          sbq_ref, sbk_ref, sbv_ref, ml_ref, gd_ref, rt_ref, gt_ref):
    h = _norm_rows(x_ref[...], nw_ref[...])
    hb = h.astype(BF16)
    seg = lambda a, b: _dot(hb, w_ref[:, a:b])
    sbq_ref[...] = seg(0, GROUP_W)
    sbk_ref[...] = seg(GROUP_W, 2 * GROUP_W)
    sbv_ref[...] = seg(2 * GROUP_W, 3 * GROUP_W)
    ml_ref[...] = seg(COL_ML, COL_GD)
    gd_ref[...] = seg(COL_GD, COL_RT)
    rt_ref[...] = seg(COL_RT, N_MAIN)
    gt_ref[...] = _gates_dot(h, wgh_ref, wgl_ref)


def _inproj_prompt(x, nw, w, wgh, wgl):
    n = x.shape[0]
    tm = TM_PROMPT
    row = lambda c: pl.BlockSpec((tm, c), lambda i: (i, 0))
    S = jax.ShapeDtypeStruct
    return pl.pallas_call(
        _inproj_prompt_kernel,
        grid=(n // tm,),
        in_specs=[row(D_MODEL), _const_spec((1, D_MODEL)), _const_spec((D_MODEL, N_MAIN)),
                  _const_spec((D_MODEL, LANES)), _const_spec((D_MODEL, LANES))],
        out_specs=[row(GROUP_W), row(GROUP_W), row(GROUP_W), row(4 * GROUP_W), row(4 * GROUP_W),
                   row(4 * GROUP_W), row(LANES)],
        out_shape=[S((n, GROUP_W), F32)] * 3 + [S((n, 4 * GROUP_W), F32)] * 3 + [S((n, LANES), F32)],
        compiler_params=_cparams(("parallel",)),
        name="inproj_prompt",
    )(x, nw, w, wgh, wgl)


def _inproj_decode_kernel(x_ref, nw_ref, w_ref, wgh_ref, wgl_ref, sbq_ref, gdraw_ref, pt_ref):
    h = _norm_rows(x_ref[...], nw_ref[...])
    hb = h.astype(BF16)
    for j in range(N_MAIN // LANES):
        p = _dot(hb, w_ref[:, j * LANES:(j + 1) * LANES])
        if j < GROUP_W // LANES:
            sbq_ref[:, j * LANES:(j + 1) * LANES] = p
        c0 = j * LANES - COL_GD
        if 0 <= c0 < 3 * GROUP_W:
            gdraw_ref[:, c0:c0 + LANES] = p
        pt_ref[j * LANES:(j + 1) * LANES, :] = p.T
    pt_ref[N_MAIN:N_PROJ, :] = _gates_dot(h, wgh_ref, wgl_ref).T


def _inproj_decode(x, nw, w, wgh, wgl):
    S = jax.ShapeDtypeStruct
    return pl.pallas_call(
        _inproj_decode_kernel,
        out_shape=[S((DEC_BATCH, GROUP_W), F32), S((DEC_BATCH, 3 * GROUP_W), F32), S((N_PROJ, DEC_BATCH), F32)],
        compiler_params=_cparams(None),
        name="inproj_decode",
    )(x, nw, w, wgh, wgl)


def _sb_prompt_kernel(bias_ref, q_ref, k_ref, v_ref, tri_ref, o_ref, acc_s, carry_s):
    tb = SB_BLOCK
    qi = pl.program_id(1)
    tri = tri_ref[...]
    r = lax.broadcasted_iota(I32, (tb, tb), 0)
    c = lax.broadcasted_iota(I32, (tb, tb), 1)
    dmask = c < r
    acc_s[...] = jnp.zeros_like(acc_s)
    carry_s[...] = jnp.zeros_like(carry_s)
    q = q_ref[...] * SCALE

    def block(j, mask):
        start = pl.multiple_of(j * tb, tb)
        kj = k_ref[pl.ds(start, tb), :]
        vj = v_ref[pl.ds(start, tb), :]
        heads = range(N_HEADS)
        sls = [slice(h * HEAD_DIM, (h + 1) * HEAD_DIM) for h in heads]
        z = [_dot_nt(q[:, sls[h]], kj[:, sls[h]]) + bias_ref[h] for h in heads]
        lsz = [_log_sigmoid(z[h]) for h in heads]
        lk = [lsz[h] - z[h] for h in heads]
        if mask is not None:
            lk = [jnp.where(mask, x, 0.0) for x in lk]
        lw_all = _dot(jnp.concatenate([x.astype(BF16) for x in lk], axis=0), tri)
        lw = [lw_all[h * tb:(h + 1) * tb, :] for h in heads]
        a = [jnp.exp(lsz[h] + lw[h] + carry_s[:, h:h + 1]) for h in heads]
        if mask is not None:
            a = [jnp.where(mask, x, 0.0) for x in a]
        av = [_dot(a[h], vj[:, sls[h]]) for h in heads]
        for h in heads:
            acc_s[:, sls[h]] += av[h]
            carry_s[:, h:h + 1] += lw[h][:, 0:1] + lk[h][:, 0:1]

    block(qi, dmask)

    def body(it, _):
        block(qi - 1 - it, None)
        return 0

    lax.fori_loop(0, qi, body, 0)
    for h in range(N_HEADS):
        sl = slice(h * HEAD_DIM, (h + 1) * HEAD_DIM)
        o_ref[:, sl] = _head_rms(acc_s[:, sl])


def _sb_prompt(q, k, v, bias, tri):
    tb = SB_BLOCK
    nq = SEQ // tb
    return pl.pallas_call(
        _sb_prompt_kernel,
        grid_spec=pltpu.PrefetchScalarGridSpec(
            num_scalar_prefetch=0,
            grid=(BATCH, nq),
            in_specs=[pl.BlockSpec(memory_space=pltpu.SMEM),
                      pl.BlockSpec((tb, GROUP_W), lambda b, i: (b * nq + i, 0)),
                      pl.BlockSpec((SEQ, GROUP_W), lambda b, i: (b, 0)),
                      pl.BlockSpec((SEQ, GROUP_W), lambda b, i: (b, 0)),
                      _const_spec((tb, tb))],
            out_specs=pl.BlockSpec((tb, GROUP_W), lambda b, i: (b * nq + i, 0)),
            scratch_shapes=[pltpu.VMEM((tb, GROUP_W), F32), pltpu.VMEM((tb, LANES), F32)]),
        out_shape=jax.ShapeDtypeStruct((N_PROMPT, GROUP_W), F32),
        compiler_params=_cparams(("parallel", "parallel")),
        name="sb_prompt",
    )(bias, q, k, v, tri)


def _sb_decode_kernel(pt_ref, bias_ref, q_ref, *rest):
    k_refs = rest[:N_PAGES]
    v_refs = rest[N_PAGES:2 * N_PAGES]
    tri_ref, pg_ref, o_ref = rest[2 * N_PAGES:]
    n_rows = 8 * N_PAGES
    b = pl.program_id(0)
    r = b % 8
    qrow = q_ref[pl.ds(r, 1), :] * SCALE
    rowi = lax.broadcasted_iota(I32, (8, GROUP_W), 0)
    lanei = lax.broadcasted_iota(I32, (8, GROUP_W), 1)
    head_of_lane = jnp.right_shift(lanei, 6)
    qbd = jnp.where(head_of_lane == rowi, qrow, 0.0).astype(BF16)
    row8 = lax.broadcasted_iota(I32, (8, 1), 0)
    bias = jnp.zeros((8, 1), F32)
    for h in range(N_HEADS):
        bias = jnp.where(row8 == h, bias_ref[h], bias)
    z = jnp.concatenate([_dot(qbd, k_refs[j][0, 0]) + bias for j in range(N_PAGES)], axis=0)
    lsz = _log_sigmoid(z)
    lk = lsz - z
    lw = _dot(lk, tri_ref[...])
    tot = jnp.broadcast_to(lw[:, 0:1] + lk[:, 0:1], (n_rows, LANES))
    later_pages = _tri_dot(pg_ref[...], tot)
    a = jnp.exp(lsz + lw + later_pages)
    acc = jnp.zeros((8, GROUP_W), F32)
    for j in range(N_PAGES):
        acc = acc + _dot_nt(a[j * 8:(j + 1) * 8, :], v_refs[j][0, 0])
    own = jnp.where(head_of_lane == rowi, acc, 0.0)
    orow = jnp.sum(own, axis=0, keepdims=True)
    pieces = [_head_rms(orow[:, h * HEAD_DIM:(h + 1) * HEAD_DIM]) for h in range(N_HEADS)]
    o_ref[pl.ds(r, 1), :] = jnp.concatenate(pieces, axis=1)


def _sb_decode(q, cache_kt, cache_vt, page_table, bias, tri, later_pages, layer):
    def page_spec(j):
        return pl.BlockSpec((1, 1, GROUP_W, PAGE_SIZE), lambda b, pt: (layer, pt[b, j], 0, 0))

    in_specs = ([pl.BlockSpec(memory_space=pltpu.SMEM),
                 pl.BlockSpec((8, GROUP_W), lambda b, pt: (b // 8, 0))]
                + [page_spec(j) for j in range(N_PAGES)] * 2
                + [pl.BlockSpec((PAGE_SIZE, PAGE_SIZE), lambda b, pt: (0, 0)),
                   pl.BlockSpec((8 * N_PAGES, 8 * N_PAGES), lambda b, pt: (0, 0))])
    return pl.pallas_call(
        _sb_decode_kernel,
        grid_spec=pltpu.PrefetchScalarGridSpec(
            num_scalar_prefetch=1,
            grid=(DEC_BATCH,),
            in_specs=in_specs,
            out_specs=pl.BlockSpec((8, GROUP_W), lambda b, pt: (b // 8, 0))),
        out_shape=jax.ShapeDtypeStruct((DEC_BATCH, GROUP_W), F32),
        compiler_params=_cparams(("arbitrary",)),
        name="sb_decode",
    )(page_table, bias, q, *([cache_kt] * N_PAGES), *([cache_vt] * N_PAGES), tri, later_pages)


def _chunk_masks(L):
    r = lax.broadcasted_iota(I32, (L, L), 0)
    c = lax.broadcasted_iota(I32, (L, L), 1)
    return c <= r, c < r


def _head_cols(blk, h):
    return slice(blk * GROUP_W + h * HEAD_DIM, blk * GROUP_W + (h + 1) * HEAD_DIM)


def _mlstm_prompt_kernel(x_ref, g_ref, gb_ref, tril_ref, o_ref, c1_ref, n1_ref, m1_ref, c_s, n_s, m_s):
    L = CHUNK
    R = x_ref.shape[0]
    ci = pl.program_id(1)

    @pl.when(ci == 0)
    def _():
        c_s[...] = jnp.zeros_like(c_s)
        n_s[...] = jnp.zeros_like(n_s)
        m_s[...] = jnp.zeros_like(m_s)

    g = g_ref[...] + gb_ref[...]
    lane = lax.broadcasted_iota(I32, (R, LANES), 1)
    lf = jnp.where((lane >= N_HEADS) & (lane < 2 * N_HEADS), _log_sigmoid(g), 0.0)
    cum = _tri_dot(tril_ref[...], lf)
    xt = jnp.where(lane < N_HEADS, g, cum).T
    causal, _ = _chunk_masks(L)
    P = [(s_i, h) for s_i in range(R // L) for h in range(N_HEADS)]
    rows = [slice(s_i * L, (s_i + 1) * L) for s_i, _ in P]
    hs = [h for _, h in P]
    np_ = range(len(P))
    bc = [cum[rows[p], N_HEADS + hs[p]:N_HEADS + hs[p] + 1] for p in np_]
    li = [g[rows[p], hs[p]:hs[p] + 1] for p in np_]
    log_d = [jnp.where(causal, bc[p] - xt[N_HEADS + hs[p]:N_HEADS + hs[p] + 1, rows[p]] + xt[hs[p]:hs[p] + 1, rows[p]],
                       -jnp.inf) for p in np_]
    b_last = [bc[p][L - 1:L, :] for p in np_]
    log_w = [b_last[p] - bc[p] + li[p] for p in np_]
    d_max = [jnp.max(log_d[p], axis=1, keepdims=True) for p in np_]
    w_max = [jnp.max(log_w[p], axis=0, keepdims=True) for p in np_]
    m = [m_s[0:1, h:h + 1] for h in range(N_HEADS)]
    m_in, m_out = [], []
    for p in np_:
        m_in.append(m[hs[p]])
        m[hs[p]] = jnp.maximum(b_last[p] + m[hs[p]], w_max[p])
        m_out.append(m[hs[p]])
    log_inter = [bc[p] + m_in[p] for p in np_]
    m_row = [jnp.maximum(log_inter[p], d_max[p]) for p in np_]
    dexp = [jnp.exp(log_d[p] - m_row[p]) for p in np_]
    w_inter = [jnp.exp(log_inter[p] - m_row[p]) for p in np_]
    decay = [jnp.exp(b_last[p] + m_in[p] - m_out[p]) for p in np_]
    q = [x_ref[rows[p], _head_cols(0, hs[p])] for p in np_]
    k = [x_ref[rows[p], _head_cols(1, hs[p])] * SCALE for p in np_]
    v = [x_ref[rows[p], _head_cols(2, hs[p])] for p in np_]
    kw = [k[p] * jnp.exp(log_w[p] - m_out[p]) for p in np_]
    s_mat = [_dot_nt(q[p], k[p]) * dexp[p] for p in np_]
    sv = [_dot(s_mat[p], v[p]) for p in np_]
    kv = [_dot_tn(kw[p], v[p]) for p in np_]
    kw_sum = [jnp.sum(kw[p], axis=0, keepdims=True) for p in np_]
    c = [c_s[h] for h in range(N_HEADS)]
    n = [n_s[h:h + 1, :] for h in range(N_HEADS)]
    c_in, n_in = [], []
    for p in np_:
        c_in.append(c[hs[p]])
        n_in.append(n[hs[p]])
        c[hs[p]] = decay[p] * c[hs[p]] + kv[p]
        n[hs[p]] = decay[p] * n[hs[p]] + kw_sum[p]
    qc = [_dot(q[p], c_in[p]) for p in np_]
    s_sum = [jnp.sum(s_mat[p], axis=1, keepdims=True) for p in np_]
    qn = [jnp.sum(q[p] * n_in[p], axis=1, keepdims=True) for p in np_]
    hh = [(sv[p] + w_inter[p] * qc[p]) / jnp.maximum(jnp.abs(s_sum[p] + w_inter[p] * qn[p]), jnp.exp(-m_row[p]))
          for p in np_]
    ms = [jnp.mean(hh[p] * hh[p], axis=1, keepdims=True) for p in np_]
    for p in np_:
        og = x_ref[rows[p], _head_cols(3, hs[p])]
        o_ref[rows[p], hs[p] * HEAD_DIM:(hs[p] + 1) * HEAD_DIM] = hh[p] * lax.rsqrt(ms[p] + EPS) * _sigmoid(og)
    for h in range(N_HEADS):
        c_s[h] = c[h]
        n_s[h:h + 1, :] = n[h]
        m_s[0:1, h:h + 1] = m[h]

    @pl.when(ci == pl.num_programs(1) - 1)
    def _():
        c1_ref[0] = c_s[...]
        n1_ref[0] = n_s[...]
        m1_ref[0] = m_s[...]


def _mlstm_prompt(ml, gates, gate_bias_row, tril):
    L = MLSTM_STEP
    nc = SEQ // L
    S = jax.ShapeDtypeStruct
    return pl.pallas_call(
        _mlstm_prompt_kernel,
        grid=(BATCH, nc),
        in_specs=[pl.BlockSpec((L, 4 * GROUP_W), lambda b, c: (b * nc + c, 0)),
                  pl.BlockSpec((L, LANES), lambda b, c: (b * nc + c, 0)),
                  _const_spec((1, LANES)), _const_spec((L, L))],
        out_specs=[pl.BlockSpec((L, GROUP_W), lambda b, c: (b * nc + c, 0)),
                   pl.BlockSpec((1, N_HEADS, HEAD_DIM, HEAD_DIM), lambda b, c: (b, 0, 0, 0)),
                   pl.BlockSpec((1, N_HEADS, HEAD_DIM), lambda b, c: (b, 0, 0)),
                   pl.BlockSpec((1, 1, LANES), lambda b, c: (b, 0, 0))],
        out_shape=[S((N_PROMPT, GROUP_W), F32), S((BATCH, N_HEADS, HEAD_DIM, HEAD_DIM), F32),
                   S((BATCH, N_HEADS, HEAD_DIM), F32), S((BATCH, 1, LANES), F32)],
        scratch_shapes=[pltpu.VMEM((N_HEADS, HEAD_DIM, HEAD_DIM), F32), pltpu.VMEM((N_HEADS, HEAD_DIM), F32),
                        pltpu.VMEM((1, LANES), F32)],
        compiler_params=_cparams(("parallel", "arbitrary")),
        name="mlstm_prompt",
    )(ml, gates, gate_bias_row, tril)


def _unit_lower_inverse(a, L):
    r = lax.broadcasted_iota(I32, (L, L), 0)
    c = lax.broadcasted_iota(I32, (L, L), 1)
    p = jnp.where(r == c, 1.0, 0.0) - a
    x = a
    power = 1
    while 2 * power < L:
        x = _dot(x, x)
        p = p + _dot(p, x)
        power *= 2
    return p


def _gdn_prompt_kernel(x_ref, g_ref, gb_ref, al_ref, cw_ref, tril_ref, o_ref, s1_ref, cv_ref, s_s, xe_s):
    L = GDN_CHUNK
    R = x_ref.shape[0]
    nq = 3 * GROUP_W
    ci = pl.program_id(1)

    @pl.when(ci == 0)
    def _():
        s_s[...] = jnp.zeros_like(s_s)
        xe_s[0:8, :] = jnp.zeros((8, nq), F32)

    raw = x_ref[:, 0:nq]
    xe_s[8:8 + R, :] = raw
    conv = (cw_ref[3:4, :] * raw + cw_ref[2:3, :] * xe_s[7:7 + R, :]
            + cw_ref[1:2, :] * xe_s[6:6 + R, :] + cw_ref[0:1, :] * xe_s[5:5 + R, :])
    xe_s[0:8, :] = raw[R - 8:R, :]
    qkv = conv * _sigmoid(conv)

    g = g_ref[...] + gb_ref[...]
    lane = lax.broadcasted_iota(I32, (R, LANES), 1)
    beta_all = _sigmoid(g)
    gd = jnp.where((lane >= 3 * N_HEADS) & (lane < 4 * N_HEADS), -jnp.exp(al_ref[...]) * _softplus(g), 0.0)
    gcum = _tri_dot(tril_ref[...], gd)
    xt = gcum.T
    incl, strict = _chunk_masks(L)
    prob = [(s_i, h) for s_i in range(R // L) for h in range(N_HEADS)]
    loc = []
    for s_i, h in prob:
        rows = slice(s_i * L, (s_i + 1) * L)
        q = qkv[rows, _head_cols(0, h)]
        k = qkv[rows, _head_cols(1, h)]
        v = qkv[rows, _head_cols(2, h)]
        q = q * lax.rsqrt(jnp.sum(q * q, axis=1, keepdims=True) + EPS) * SCALE
        k = k * lax.rsqrt(jnp.sum(k * k, axis=1, keepdims=True) + EPS)
        b = beta_all[rows, 2 * N_HEADS + h:2 * N_HEADS + h + 1]
        gc = gcum[rows, 3 * N_HEADS + h:3 * N_HEADS + h + 1]
        gc_row = xt[3 * N_HEADS + h:3 * N_HEADS + h + 1, rows]
        decay = jnp.exp(jnp.where(incl, gc - gc_row, -jnp.inf))
        kb = k * b
        egc = jnp.exp(gc)
        gl = gc[L - 1:L, :]
        loc.append(dict(q=q, k=k, kb=kb, decay=decay, vb=v * b, kbe=kb * egc, q_dec=q * egc,
                        k_dec=k * jnp.exp(gl - gc), g_last=jnp.exp(gl)))
    a_mat = [jnp.where(strict, _dot_nt(d["kb"], d["k"]) * d["decay"], 0.0) for d in loc]
    qk_mat = [_dot_nt(d["q"], d["k"]) * d["decay"] for d in loc]
    eye = jnp.where(incl & jnp.logical_not(strict), 1.0, 0.0)
    pw = a_mat
    inv = [eye - a for a in a_mat]
    power = 1
    while 2 * power < L:
        pw = [_dot(x, x) for x in pw]
        inv = [p + _dot(p, x) for p, x in zip(inv, pw)]
        power *= 2
    us = [_dot(t, d["vb"]) for t, d in zip(inv, loc)]
    ws = [_dot(t, d["kbe"]) for t, d in zip(inv, loc)]
    s = [s_s[h] for h in range(N_HEADS)]
    heads = range(N_HEADS)
    for s_i in range(R // L):
        rows = slice(s_i * L, (s_i + 1) * L)
        p0 = s_i * N_HEADS
        w_s = [_dot(ws[p0 + h], s[h]) for h in heads]
        q_s = [_dot(loc[p0 + h]["q_dec"], s[h]) for h in heads]
        v_new = [us[p0 + h] - w_s[h] for h in heads]
        o = [q_s[h] + _dot(qk_mat[p0 + h], v_new[h]) for h in heads]
        s = [s[h] * loc[p0 + h]["g_last"] + _dot_tn(loc[p0 + h]["k_dec"], v_new[h]) for h in heads]
        for h in heads:
            gate = x_ref[rows, _head_cols(3, h)]
            o_ref[rows, h * HEAD_DIM:(h + 1) * HEAD_DIM] = _head_rms(o[h]) * (gate * _sigmoid(gate))
    for h in range(N_HEADS):
        s_s[h] = s[h]

    @pl.when(ci == pl.num_programs(1) - 1)
    def _():
        s1_ref[0] = s_s[...]
        cv_ref[0] = xe_s[8 - (CONV_W - 1):8, :]


def _gdn_prompt(gd, gates, gate_bias_row, a_log_row, conv_w, tril):
    L = GDN_STEP
    nc = SEQ // L
    S = jax.ShapeDtypeStruct
    return pl.pallas_call(
        _gdn_prompt_kernel,
        grid=(BATCH, nc),
        in_specs=[pl.BlockSpec((L, 4 * GROUP_W), lambda b, c: (b * nc + c, 0)),
                  pl.BlockSpec((L, LANES), lambda b, c: (b * nc + c, 0)),
                  _const_spec((1, LANES)), _const_spec((1, LANES)), _const_spec((CONV_W, 3 * GROUP_W)),
                  _const_spec((L, L))],
        out_specs=[pl.BlockSpec((L, GROUP_W), lambda b, c: (b * nc + c, 0)),
                   pl.BlockSpec((1, N_HEADS, HEAD_DIM, HEAD_DIM), lambda b, c: (b, 0, 0, 0)),
                   pl.BlockSpec((1, CONV_W - 1, 3 * GROUP_W), lambda b, c: (b, 0, 0))],
        out_shape=[S((N_PROMPT, GROUP_W), F32), S((BATCH, N_HEADS, HEAD_DIM, HEAD_DIM), F32),
                   S((BATCH, CONV_W - 1, 3 * GROUP_W), F32)],
        scratch_shapes=[pltpu.VMEM((N_HEADS, HEAD_DIM, HEAD_DIM), F32), pltpu.VMEM((8 + L, 3 * GROUP_W), F32)],
        compiler_params=_cparams(("parallel", "arbitrary")),
        name="gdn_prompt",
    )(gd, gates, gate_bias_row, a_log_row, conv_w, tril)


def _rope_rows(x, cos, sin_signed):
    lane = lax.broadcasted_iota(I32, x.shape, 1)
    first = jnp.bitwise_and(lane, HEAD_DIM - 1) < HEAD_DIM // 2
    w = x.shape[1]
    swapped = jnp.where(first, pltpu.roll(x, w - HEAD_DIM // 2, 1), pltpu.roll(x, HEAD_DIM // 2, 1))
    return x * cos + swapped * sin_signed


def _ret_prompt_kernel(x_ref, cos_ref, sin_ref, dm_ref, xz_ref, gch_ref, o_ref, s1_ref, s_s):
    ci = pl.program_id(1)

    @pl.when(ci == 0)
    def _():
        s_s[...] = jnp.zeros_like(s_s)

    L = CHUNK
    cos = cos_ref[...]
    sin = sin_ref[...]
    qr = _rope_rows(x_ref[:, 0:GROUP_W], cos, sin)
    kr = _rope_rows(x_ref[:, GROUP_W:2 * GROUP_W], cos, sin) * SCALE
    prob = []
    for s_i in range(x_ref.shape[0] // L):
        rows = slice(s_i * L, (s_i + 1) * L)
        for h in range(N_HEADS):
            sl = slice(h * HEAD_DIM, (h + 1) * HEAD_DIM)
            prob.append(dict(rows=rows, h=h, sl=sl, q=qr[rows, sl], k=kr[rows, sl], v=x_ref[rows, _head_cols(2, h)]))
    qk = [_dot_nt(p["q"], p["k"]) * dm_ref[p["h"]] for p in prob]
    intra = [_dot(a, p["v"]) for a, p in zip(qk, prob)]
    kv = [_dot_tn(p["k"] * xz_ref[:, N_HEADS + p["h"]:N_HEADS + p["h"] + 1], p["v"]) for p in prob]
    s = [s_s[h] for h in range(N_HEADS)]
    s_in = []
    for p, kv_p in zip(prob, kv):
        s_in.append(s[p["h"]])
        s[p["h"]] = s[p["h"]] * gch_ref[p["h"]] + kv_p
    inter = [_dot(p["q"], s0) * xz_ref[:, p["h"]:p["h"] + 1] for p, s0 in zip(prob, s_in)]
    for p, a, b in zip(prob, intra, inter):
        gate = x_ref[p["rows"], _head_cols(3, p["h"])]
        o_ref[p["rows"], p["sl"]] = _head_rms(a + b) * (gate * _sigmoid(gate))
    for h in range(N_HEADS):
        s_s[h] = s[h]

    @pl.when(ci == pl.num_programs(1) - 1)
    def _():
        s1_ref[0] = s_s[...]


def _ret_prompt(rt, cos, sin, dmat, xz, gch):
    L = RET_STEP
    nc = SEQ // L
    S = jax.ShapeDtypeStruct
    return pl.pallas_call(
        _ret_prompt_kernel,
        grid=(BATCH, nc),
        in_specs=[pl.BlockSpec((L, 4 * GROUP_W), lambda b, c: (b * nc + c, 0)),
                  pl.BlockSpec((L, GROUP_W), lambda b, c: (c, 0)),
                  pl.BlockSpec((L, GROUP_W), lambda b, c: (c, 0)),
                  _const_spec((N_HEADS, CHUNK, CHUNK)), _const_spec((CHUNK, LANES)),
                  pl.BlockSpec(memory_space=pltpu.SMEM)],
        out_specs=[pl.BlockSpec((L, GROUP_W), lambda b, c: (b * nc + c, 0)),
                   pl.BlockSpec((1, N_HEADS, HEAD_DIM, HEAD_DIM), lambda b, c: (b, 0, 0, 0))],
        out_shape=[S((N_PROMPT, GROUP_W), F32), S((BATCH, N_HEADS, HEAD_DIM, HEAD_DIM), F32)],
        scratch_shapes=[pltpu.VMEM((N_HEADS, HEAD_DIM, HEAD_DIM), F32)],
        compiler_params=_cparams(("parallel", "arbitrary")),
        name="ret_prompt",
    )(rt, cos, sin, dmat, xz, gch)


def _gdn_decode_conv_kernel(raw_ref, c0_ref, cw_ref, qkvt_ref, cv_ref):
    raw = raw_ref[...]
    conv = (cw_ref[3:4, :] * raw + cw_ref[2:3, :] * c0_ref[2] + cw_ref[1:2, :] * c0_ref[1]
            + cw_ref[0:1, :] * c0_ref[0])
    qkv = conv * _sigmoid(conv)
    cv_ref[0] = c0_ref[1]
    cv_ref[1] = c0_ref[2]
    cv_ref[2] = raw
    for blk in range(3):
        for h in range(N_HEADS):
            lo = blk * GROUP_W + h * HEAD_DIM
            x = qkv[:, lo:lo + HEAD_DIM]
            if blk == 0:
                x = x * lax.rsqrt(jnp.sum(x * x, axis=1, keepdims=True) + EPS) * SCALE
            elif blk == 1:
                x = x * lax.rsqrt(jnp.sum(x * x, axis=1, keepdims=True) + EPS)
            if h % 2 == 0:
                pair = x
            else:
                qkvt_ref[lo - HEAD_DIM:lo + HEAD_DIM, :] = jnp.concatenate([pair, x], axis=1).T


def _gdn_decode_conv(raw, conv0, conv_w):
    S = jax.ShapeDtypeStruct
    return pl.pallas_call(
        _gdn_decode_conv_kernel,
        out_shape=[S((3 * GROUP_W, DEC_BATCH), F32), S((CONV_W - 1, DEC_BATCH, 3 * GROUP_W), F32)],
        compiler_params=_cparams(None),
        name="gdn_decode_conv",
    )(raw, conv0, conv_w)


def _decode_rec_kernel(sc_ref, mlq_ref, mlk_ref, mlv_ref, mlo_ref, gq_ref, gk_ref, gv_ref, gg_ref,
                       rq_ref, rk_ref, rv_ref, rg_ref, gt_ref, cos_ref, sin_ref,
                       c0_ref, n0_ref, m0_ref, sg0_ref, sr0_ref,
                       oml_ref, ogd_ref, ort_ref, c1_ref, n1_ref, m1_ref, sg1_ref, sr1_ref, va_s, vb_s):
    h = pl.program_id(0)
    D = HEAD_DIM
    sum0 = lambda x: jnp.sum(x, axis=0, keepdims=True)
    rms0 = lambda x: x * lax.rsqrt(sum0(x * x) * (1.0 / D) + EPS)

    li = gt_ref[pl.ds(h, 1), :] + sc_ref[0, h]
    lf = _log_sigmoid(gt_ref[pl.ds(N_HEADS + h, 1), :] + sc_ref[1, h])
    m0 = m0_ref[0]
    q = mlq_ref[...]
    k = mlk_ref[...] * SCALE
    v = mlv_ref[...]
    log_inter = lf + m0
    m_row = jnp.maximum(log_inter, li)
    s = sum0(q * k) * jnp.exp(li - m_row)
    w_inter = jnp.exp(log_inter - m_row)
    decay = jnp.exp(lf + m0 - m_row)
    kw = k * jnp.exp(li - m_row)

    va_s[...] = kw

    def ml_body(d, qc):
        c_d = c0_ref[0, d]
        c1_ref[0, d] = decay * c_d + va_s[pl.ds(d, 1), :] * v
        return qc + mlq_ref[pl.ds(d, 1), :] * c_d

    n0 = n0_ref[0]
    n1_ref[0] = decay * n0 + kw
    qc = lax.fori_loop(0, D, ml_body, jnp.zeros((D, LANES), F32))
    num = s * v + w_inter * qc
    den = s + w_inter * sum0(q * n0)
    hh = num / jnp.maximum(jnp.abs(den), jnp.exp(-m_row))
    m1_ref[0] = m_row
    oml_ref[...] = rms0(hh) * _sigmoid(mlo_ref[...])

    beta = _sigmoid(gt_ref[pl.ds(2 * N_HEADS + h, 1), :])
    gdec = -sc_ref[3, h] * _softplus(gt_ref[pl.ds(3 * N_HEADS + h, 1), :] + sc_ref[2, h])
    eg = jnp.exp(gdec)
    gv = gv_ref[...]

    def ks_body(d, acc):
        return acc + gk_ref[pl.ds(d, 1), :] * sg0_ref[0, d]

    ks = lax.fori_loop(0, D, ks_body, jnp.zeros((D, LANES), F32))
    v_new = beta * gv - (beta * eg) * ks

    def gd_body(d, acc):
        s_new = eg * sg0_ref[0, d] + gk_ref[pl.ds(d, 1), :] * v_new
        sg1_ref[0, d] = s_new
        return acc + gq_ref[pl.ds(d, 1), :] * s_new

    og = lax.fori_loop(0, D, gd_body, jnp.zeros((D, LANES), F32))
    gate = gg_ref[...]
    ogd_ref[...] = rms0(og) * (gate * _sigmoid(gate))

    half = D // 2

    def rope(ref):
        x = ref[...]
        sw = jnp.concatenate([x[half:, :], x[:half, :]], axis=0)
        return x * cos_ref[...] + sw * sin_ref[...]

    va_s[...] = rope(rq_ref)
    vb_s[...] = rope(rk_ref) * SCALE
    rv = rv_ref[...]
    gamma = sc_ref[4, h]

    def rt_body(d, acc):
        s_new = gamma * sr0_ref[0, d] + vb_s[pl.ds(d, 1), :] * rv
        sr1_ref[0, d] = s_new
        return acc + va_s[pl.ds(d, 1), :] * s_new

    ort = lax.fori_loop(0, D, rt_body, jnp.zeros((D, LANES), F32))
    gate = rg_ref[...]
    ort_ref[...] = rms0(ort) * (gate * _sigmoid(gate))


def _decode_rec(scalars, pt, qkvt, cos_t, sin_t, c0, n0, m0, sg0, sr0):
    D = HEAD_DIM
    S = jax.ShapeDtypeStruct
    prow = lambda col, blk: pl.BlockSpec((D, LANES), lambda h: ((col + blk * GROUP_W) // D + h, 0))
    vec = lambda blk: pl.BlockSpec((D, LANES), lambda h: (blk * N_HEADS + h, 0))
    st4 = pl.BlockSpec((1, D, D, LANES), lambda h: (h, 0, 0, 0))
    st3 = pl.BlockSpec((1, D, LANES), lambda h: (h, 0, 0))
    st2 = pl.BlockSpec((1, 1, LANES), lambda h: (h, 0, 0))
    in_specs = ([pl.BlockSpec(memory_space=pltpu.SMEM)]
                + [prow(COL_ML, blk) for blk in range(4)]
                + [vec(0), vec(1), vec(2), prow(COL_GD, 3)]
                + [prow(COL_RT, blk) for blk in range(4)]
                + [pl.BlockSpec((LANES, LANES), lambda h: (COL_GATES // LANES, 0)),
                   _const_spec((D, LANES)), _const_spec((D, LANES)), st4, st3, st2, st4, st4])
    return pl.pallas_call(
        _decode_rec_kernel,
        grid=(N_HEADS,),
        in_specs=in_specs,
        out_specs=[vec(0), vec(0), vec(0), st4, st3, st2, st4, st4],
        out_shape=[S((GROUP_W, LANES), F32)] * 3
        + [S((N_HEADS, D, D, LANES), F32), S((N_HEADS, D, LANES), F32), S((N_HEADS, 1, LANES), F32),
           S((N_HEADS, D, D, LANES), F32), S((N_HEADS, D, D, LANES), F32)],
        scratch_shapes=[pltpu.VMEM((D, LANES), F32), pltpu.VMEM((D, LANES), F32)],
        compiler_params=_cparams(("parallel",)),
        name="decode_rec",
    )(scalars, pt, pt, pt, pt, qkvt, qkvt, qkvt, pt, pt, pt, pt, pt, pt, cos_t, sin_t, c0, n0, m0, sg0, sr0)


def _outproj_kernel(x_ref, osb_ref, oml_ref, ogd_ref, ort_ref, gain_ref, wo_ref, nw_ref, rwh_ref, rwl_ref,
                    x1_ref, h2_ref, lg_ref, *, transposed):
    parts = [osb_ref[...]]
    for ref in (oml_ref, ogd_ref, ort_ref):
        parts.append(ref[...].T if transposed else ref[...])
    y = None
    for g, p in enumerate(parts):
        cols = slice(g * GROUP_W, (g + 1) * GROUP_W)
        t = _dot(p * gain_ref[:, cols], wo_ref[cols, :])
        y = t if y is None else y + t
    x1 = x_ref[...] + y
    h2 = _norm_rows(x1, nw_ref[...])
    x1_ref[...] = x1
    h2_ref[...] = h2.astype(BF16)
    lg_ref[...] = _gates_dot(h2, rwh_ref, rwl_ref)


def _outproj(x, osb, oml, ogd, ort, gain, wo, nw, rwh, rwl, transposed):
    n = x.shape[0]
    tm = min(TM_PROMPT, n)
    S = jax.ShapeDtypeStruct
    row = lambda c: pl.BlockSpec((tm, c), lambda i: (i, 0))
    mix = _const_spec((GROUP_W, DEC_BATCH)) if transposed else row(GROUP_W)
    return pl.pallas_call(
        functools.partial(_outproj_kernel, transposed=transposed),
        grid=(n // tm,),
        in_specs=[row(D_MODEL), row(GROUP_W), mix, mix, mix, _const_spec((1, D_MODEL)),
                  _const_spec((D_MODEL, D_MODEL)), _const_spec((1, D_MODEL)),
                  _const_spec((D_MODEL, LANES)), _const_spec((D_MODEL, LANES))],
        out_specs=[row(D_MODEL), row(D_MODEL), row(LANES)],
        out_shape=[S((n, D_MODEL), F32), S((n, D_MODEL), BF16), S((n, LANES), F32)],
        compiler_params=_cparams(("parallel",)),
        name="outproj",
    )(x, osb, oml, ogd, ort, gain, wo, nw, rwh, rwl)


def _route_kernel(lg_ref, rb_ref, tri_ref, lt_ref, cin_ref, pr_ref, rf_ref, ti_ref, cnt_ref):
    @pl.when(pl.program_id(0) == 0)
    def _():
        cnt_ref[...] = cin_ref[...]

    lg = lg_ref[...] + rb_ref[...]
    lane = lax.broadcasted_iota(I32, lg.shape, 1)
    ninf = -jnp.inf
    big = LANES - 1
    rmax = lambda x: jnp.max(x, axis=1, keepdims=True)
    rmin = lambda x: jnp.min(x, axis=1, keepdims=True)
    gl = jnp.where(lane < N_GROUPS, lg, ninf)
    gmax = rmax(gl)
    g_sel = rmin(jnp.where(gl == gmax, lane, big))
    g_prob = 1.0 / jnp.sum(jnp.exp(gl - gmax), axis=1, keepdims=True)
    e_lane = (lane >= N_GROUPS) & (lane < N_GROUPS + N_EXPERTS)
    em = jnp.where(e_lane & (jnp.right_shift(lane - N_GROUPS, 3) == g_sel), lg, ninf)
    v1 = rmax(em)
    i1 = rmin(jnp.where(em == v1, lane, big))
    em2 = jnp.where(lane == i1, ninf, em)
    v2 = rmax(em2)
    i2 = rmin(jnp.where(em2 == v2, lane, big))
    t = jnp.exp(v2 - v1)
    gate1 = g_prob / (1.0 + t)
    gate2 = g_prob * t / (1.0 + t)
    e1 = i1 - N_GROUPS
    e2 = i2 - N_GROUPS
    onehot = jnp.where((lane == e1) | (lane == e2), 1.0, 0.0)
    before = _dot(tri_ref[...], onehot)
    tile_cnt = jnp.sum(onehot, axis=0, keepdims=True)
    padded = jnp.floor((tile_cnt + (RUN_ALIGN - 1)) * (1.0 / RUN_ALIGN)) * RUN_ALIGN
    run_start = _dot(jnp.broadcast_to(padded, (8, LANES)), lt_ref[...])[0:1, :]
    pos = before + run_start
    p1 = jnp.sum(jnp.where(lane == e1, pos, 0.0), axis=1, keepdims=True)
    p2 = jnp.sum(jnp.where(lane == e2, pos, 0.0), axis=1, keepdims=True)
    pr = jnp.where(lane == 0, p1, jnp.where(lane == 1, p2, 0.0)).astype(I32)
    pr_ref[...] = pr.T[0:8, :]
    rf_ref[...] = jnp.where(lane == 0, gate1, jnp.where(lane == 1, gate2, jnp.where(lane == 2, p1,
                            jnp.where(lane == 3, p2, 0.0))))
    row8 = lax.broadcasted_iota(I32, (8, LANES), 0)
    ti_ref[...] = jnp.where(row8 == 0, padded, jnp.where(row8 == 1, cnt_ref[...],
                            jnp.where(row8 == 2, run_start, 0.0))).astype(I32)
    cnt_ref[...] += padded


def _route(logits, rb, tri, lanes_lt, cnt_in):
    n = logits.shape[0]
    tm = tri.shape[0]
    S = jax.ShapeDtypeStruct
    row = pl.BlockSpec((tm, LANES), lambda i: (i, 0))
    return pl.pallas_call(
        _route_kernel,
        grid=(n // tm,),
        in_specs=[row, _const_spec((1, LANES)), _const_spec((tm, tm)), _const_spec((LANES, LANES)),
                  _const_spec((1, LANES))],
        out_specs=[pl.BlockSpec((8, tm), lambda i: (0, i)), row, pl.BlockSpec((8, LANES), lambda i: (i, 0)),
                   _const_spec((1, LANES))],
        out_shape=[S((8, n), I32), S((n, LANES), F32), S((8 * (n // tm), LANES), I32), S((1, LANES), F32)],
        compiler_params=_cparams(("arbitrary",)),
        name="route",
    )(logits, rb, tri, lanes_lt, cnt_in)


def _packed_rows(tm):
    return 2 * tm + N_EXPERTS * RUN_ALIGN


def _for_each_run_chunk(plan_refs, fn):
    offs_ref, len_ref, before_ref, start_ref = plan_refs
    base = pl.program_id(0) * N_EXPERTS

    def per_expert(e, _):
        n_chunks = len_ref[base + e] // RUN_ALIGN
        packed0 = start_ref[base + e]
        slot0 = offs_ref[e] + before_ref[base + e]

        def per_chunk(c, _):
            fn(pl.multiple_of(packed0 + c * RUN_ALIGN, RUN_ALIGN), pl.multiple_of(slot0 + c * RUN_ALIGN, RUN_ALIGN))
            return 0

        lax.fori_loop(0, n_chunks, per_chunk, 0)
        return 0

    lax.fori_loop(0, N_EXPERTS, per_expert, 0)


def _dispatch_kernel(offs_ref, len_ref, before_ref, start_ref, h_ref, pr_ref, xs_in_ref, xs_ref, pk_s, sem):
    del xs_in_ref
    plan_refs = (offs_ref, len_ref, before_ref, start_ref)
    tm = h_ref.shape[0]
    slot = lax.broadcasted_iota(I32, (_packed_rows(tm), tm), 0)
    sel = jnp.where((slot == pr_ref[0:1, :]) | (slot == pr_ref[1:2, :]), 1.0, 0.0)
    pk_s[...] = _dot(sel, h_ref[...])

    def copy(packed_row, slot_row):
        return pltpu.make_async_copy(pk_s.at[pl.ds(packed_row, RUN_ALIGN), :],
                                     xs_ref.at[pl.ds(slot_row, RUN_ALIGN), :], sem.at[0])

    _for_each_run_chunk(plan_refs, lambda p, s: copy(p, s).start())
    _for_each_run_chunk(plan_refs, lambda p, s: copy(p, s).wait())


def _dispatch(plan, h2, pr, xs):
    n = h2.shape[0]
    tm = min(TM_ROUTE, n)
    return pl.pallas_call(
        _dispatch_kernel,
        grid_spec=pltpu.PrefetchScalarGridSpec(
            num_scalar_prefetch=4,
            grid=(n // tm,),
            in_specs=[pl.BlockSpec((tm, D_MODEL), lambda i, *_: (i, 0)),
                      pl.BlockSpec((8, tm), lambda i, *_: (0, i)),
                      pl.BlockSpec(memory_space=pl.ANY)],
            out_specs=pl.BlockSpec(memory_space=pl.ANY),
            scratch_shapes=[pltpu.VMEM((_packed_rows(tm), D_MODEL), F32), pltpu.SemaphoreType.DMA((1,))]),
        out_shape=jax.ShapeDtypeStruct(xs.shape, xs.dtype),
        input_output_aliases={6: 0},
        compiler_params=_cparams(("arbitrary",)),
        name="dispatch",
    )(*plan, h2, pr, xs)


def _experts_kernel(te_ref, ts_ref, xs_ref, wg_ref, wu_ref, wd_ref, ys_ref, wg_s, wu_s, wd_s):
    i = pl.program_id(0)
    valid = ts_ref[i] == i
    fresh = (i == 0) | (te_ref[i] != te_ref[jnp.maximum(i - 1, 0)])

    @pl.when(valid & fresh)
    def _():
        wg_s[...] = wg_ref[0].astype(BF16)
        wu_s[...] = wu_ref[0].astype(BF16)
        wd_s[...] = wd_ref[0].astype(BF16)

    @pl.when(valid)
    def _():
        x = xs_ref[...].astype(BF16)
        a = _dot(x, wg_s[...])
        u = _dot(x, wu_s[...])
        act = a * _sigmoid(a) * u
        ys_ref[...] = _dot(act, wd_s[...])

    @pl.when(jnp.logical_not(valid))
    def _():
        ys_ref[...] = jnp.zeros_like(ys_ref)


def _experts(tile_expert, tile_src, xs, wg, wu, wd, layer):
    tm = TM_EXPERT
    wspec = lambda a, b: pl.BlockSpec((None, 1, a, b), lambda i, te, tv: (layer, te[i], 0, 0))
    return pl.pallas_call(
        _experts_kernel,
        grid_spec=pltpu.PrefetchScalarGridSpec(
            num_scalar_prefetch=2,
            grid=(N_TILES,),
            in_specs=[pl.BlockSpec((tm, D_MODEL), lambda i, te, ts: (ts[i], 0)),
                      wspec(D_MODEL, D_EXPERT), wspec(D_MODEL, D_EXPERT), wspec(D_EXPERT, D_MODEL)],
            out_specs=pl.BlockSpec((tm, D_MODEL), lambda i, te, tv: (i, 0)),
            scratch_shapes=[pltpu.VMEM((D_MODEL, D_EXPERT), BF16), pltpu.VMEM((D_MODEL, D_EXPERT), BF16),
                            pltpu.VMEM((D_EXPERT, D_MODEL), BF16)]),
        out_shape=jax.ShapeDtypeStruct((N_SLOTS, D_MODEL), F32),
        compiler_params=_cparams(("arbitrary",)),
        name="experts",
    )(tile_expert, tile_src, xs, wg, wu, wd)


def _combine_kernel(offs_ref, len_ref, before_ref, start_ref, x1_ref, rf_ref, fw_ref, ys_ref, out_ref,
                    pk_s, sem, *, final):
    plan_refs = (offs_ref, len_ref, before_ref, start_ref)
    tm = x1_ref.shape[0]

    @pl.when(pl.program_id(0) == 0)
    def _():
        pk_s[...] = jnp.zeros_like(pk_s)

    def copy(packed_row, slot_row):
        return pltpu.make_async_copy(ys_ref.at[pl.ds(slot_row, RUN_ALIGN), :],
                                     pk_s.at[pl.ds(packed_row, RUN_ALIGN), :], sem.at[0])

    _for_each_run_chunk(plan_refs, lambda p, s: copy(p, s).start())
    _for_each_run_chunk(plan_refs, lambda p, s: copy(p, s).wait())
    slot = lax.broadcasted_iota(I32, (tm, _packed_rows(tm)), 1)
    packed = pk_s[...].astype(BF16)
    x2 = x1_ref[...]
    for k in range(2):
        sel = jnp.where(slot == rf_ref[:, 2 + k:3 + k].astype(I32), 1.0, 0.0)
        x2 = x2 + rf_ref[:, k:k + 1] * _dot(sel, packed)
    out_ref[...] = _norm_rows(x2, fw_ref[...]) if final else x2


def _combine(plan, x1, rf, fw, ys, final):
    n = x1.shape[0]
    tm = min(TM_ROUTE, n)
    row = lambda c: pl.BlockSpec((tm, c), lambda i, *_: (i, 0))
    return pl.pallas_call(
        functools.partial(_combine_kernel, final=final),
        grid_spec=pltpu.PrefetchScalarGridSpec(
            num_scalar_prefetch=4,
            grid=(n // tm,),
            in_specs=[row(D_MODEL), row(LANES), pl.BlockSpec((1, D_MODEL), lambda i, *_: (0, 0)),
                      pl.BlockSpec(memory_space=pl.ANY)],
            out_specs=row(D_MODEL),
            scratch_shapes=[pltpu.VMEM((_packed_rows(tm), D_MODEL), F32), pltpu.SemaphoreType.DMA((1,))]),
        out_shape=jax.ShapeDtypeStruct((n, D_MODEL), F32),
        compiler_params=_cparams(("arbitrary",)),
        name="combine",
    )(*plan, x1, rf, fw, ys)


def _tri(n, kind, chunk=1):
    r = np.arange(n)[:, None]
    c = np.arange(n)[None, :]
    m = {"chunk_lower_incl": (c <= r) & (r // chunk == c // chunk),
         "lower_strict": c < r,
         "row_gt_col": r > c,
         "row_lt_col": r < c,
         "later_page": (r % 8 == c % 8) & (c // 8 > r // 8)}[kind]
    return jnp.asarray(m.astype(np.float32), dtype=BF16)


def _rope_tables(pos):
    half = HEAD_DIM // 2
    inv = ROPE_BASE ** (-np.arange(half, dtype=np.float64) / half)
    ang = np.asarray(pos, np.float64)[:, None] * inv[None, :]
    cos = np.concatenate([np.cos(ang), np.cos(ang)], axis=1)
    sin = np.concatenate([-np.sin(ang), np.sin(ang)], axis=1)
    return cos.astype(np.float32), sin.astype(np.float32)


def _retention_consts(L):
    log_gamma = np.log1p(-np.exp2(-5.0 - np.arange(N_HEADS, dtype=np.float64)))
    idx = np.arange(L, dtype=np.float64)
    diff = np.maximum(idx[:, None] - idx[None, :], 0.0)
    dmat = np.where(idx[None, :] <= idx[:, None], np.exp(log_gamma[:, None, None] * diff), 0.0)
    xi = np.exp(log_gamma[:, None] * (idx + 1.0))
    zeta = np.exp(log_gamma[:, None] * (L - 1.0 - idx))
    xz = np.zeros((L, LANES), np.float64)
    xz[:, 0:N_HEADS] = xi.T
    xz[:, N_HEADS:2 * N_HEADS] = zeta.T
    return (jnp.asarray(dmat, F32), jnp.asarray(xz, F32), jnp.asarray(np.exp(log_gamma * L), F32),
            np.exp(log_gamma))


def _hi_lo(w):
    hi = w.astype(BF16)
    return hi, (w - hi.astype(F32)).astype(BF16)


def _pad_lanes(w):
    return jnp.pad(w, ((0, 0), (0, LANES - w.shape[1])))


def _prep_w_in(w):
    ml0 = 3 * GROUP_W
    mlg = ml0 + 4 * GROUP_W
    gd0 = mlg + 2 * N_HEADS
    gdg = gd0 + 4 * GROUP_W
    rt0 = gdg + 2 * N_HEADS
    main = jnp.concatenate([w[:, 0:ml0], w[:, ml0:mlg], w[:, gd0:gdg], w[:, rt0:]], axis=1).astype(BF16)
    gates = _pad_lanes(jnp.concatenate([w[:, mlg:gd0], w[:, gdg:rt0]], axis=1))
    return (main,) + _hi_lo(gates)


def _tile_plan(tile_info):
    t = tile_info.reshape(-1, 8, LANES)[:, :, :N_EXPERTS]
    return t[:, 0].reshape(-1), t[:, 1].reshape(-1), t[:, 2].reshape(-1)


def _dispatch_plan(cnt):
    ntile = (cnt + TM_EXPERT - 1) // TM_EXPERT
    tile_end = jnp.cumsum(ntile)
    offs = (tile_end - ntile) * TM_EXPERT
    tid = jnp.arange(N_TILES, dtype=I32)
    te = jnp.minimum(jnp.sum(tid[:, None] >= tile_end[None, :], axis=1), N_EXPERTS - 1).astype(I32)
    tile_src = jnp.minimum(tid, tile_end[-1] - 1).astype(I32)
    return offs.astype(I32), te, tile_src


def kernel(x_prompt, x_sample, cache_sb_k, cache_sb_v, state_mlstm_c, state_mlstm_n, state_mlstm_m, state_gdn_s, state_gdn_conv, state_ret_s, page_table, norm_attn_w, w_in, sb_logit_bias, mlstm_gate_bias, gdn_conv_w, gdn_a_log, gdn_dt_bias, head_norm_w, w_out, norm_ffn_w, router_group_w, router_group_b, router_expert_w, router_expert_b, expert_w_gate, expert_w_up, expert_w_down, final_norm_w):
    D = HEAD_DIM
    xp = x_prompt.reshape(N_PROMPT, D_MODEL)
    xd = x_sample.reshape(DEC_BATCH, D_MODEL)
    n_pool = cache_sb_k.shape[1]
    cache_kt = cache_sb_k.transpose(0, 1, 3, 4, 2).reshape(DEPTH, n_pool, GROUP_W, PAGE_SIZE)
    cache_vt = cache_sb_v.transpose(0, 1, 3, 4, 2).reshape(DEPTH, n_pool, GROUP_W, PAGE_SIZE)

    tril_ml = _tri(MLSTM_STEP, "chunk_lower_incl", CHUNK)
    tril_gd = _tri(GDN_STEP, "chunk_lower_incl", GDN_CHUNK)
    tri_sb = _tri(SB_BLOCK, "row_gt_col")
    tri_page = _tri(PAGE_SIZE, "row_gt_col")
    later_pages = _tri(8 * N_PAGES, "later_page")
    lanes_lt = _tri(LANES, "row_lt_col")
    tri_rp = _tri(TM_ROUTE, "lower_strict")
    tri_rd = _tri(DEC_BATCH, "lower_strict")
    cos_p, sin_p = _rope_tables(np.arange(SEQ))
    cos_p = jnp.asarray(np.tile(cos_p, (1, N_HEADS)))
    sin_p = jnp.asarray(np.tile(sin_p, (1, N_HEADS)))
    cos_d, sin_d = _rope_tables([PAST_LEN])
    cos_d = jnp.asarray(np.tile(cos_d.T, (1, LANES)))
    sin_d = jnp.asarray(np.tile(sin_d.T, (1, LANES)))
    dmat, xz, gch, gamma = _retention_consts(CHUNK)
    zeros4 = jnp.zeros((N_HEADS,), F32)

    outs = {k: [] for k in ("kp", "vp", "ks", "vs", "cp", "np", "mp", "cs", "ns", "ms", "gp", "gcp", "gs", "gcs",
                            "rp", "rs")}
    yp = yd = None
    for l in range(DEPTH):
        w_main, wg_hi, wg_lo = _prep_w_in(w_in[l])
        nw = norm_attn_w[l][None, :]
        gate_bias_row = _pad_lanes(jnp.concatenate([mlstm_gate_bias[l], zeros4, gdn_dt_bias[l]])[None, :])
        a_log_row = _pad_lanes(jnp.concatenate([zeros4, zeros4, zeros4, gdn_a_log[l]])[None, :])
        bias = sb_logit_bias[l]

        sbq, sbk, sbv, ml, gd, rt, gt = _inproj_prompt(xp, nw, w_main, wg_hi, wg_lo)
        osb_p = _sb_prompt(sbq, sbk, sbv, bias, tri_sb)
        oml_p, c1p, n1p, m1p = _mlstm_prompt(ml, gt, gate_bias_row, tril_ml)
        ogd_p, s1p, cv1p = _gdn_prompt(gd, gt, gate_bias_row, a_log_row, gdn_conv_w[l], tril_gd)
        ort_p, r1p = _ret_prompt(rt, cos_p, sin_p, dmat, xz, gch)

        sbq_d, gdraw_d, pt = _inproj_decode(xd, nw, w_main, wg_hi, wg_lo)
        osb_d = _sb_decode(sbq_d, cache_kt, cache_vt, page_table, bias, tri_page, later_pages, l)
        qkvt, cv1d = _gdn_decode_conv(gdraw_d, state_gdn_conv[l].transpose(1, 0, 2), gdn_conv_w[l])
        scalars = jnp.stack([mlstm_gate_bias[l][:N_HEADS], mlstm_gate_bias[l][N_HEADS:], gdn_dt_bias[l],
                             jnp.exp(gdn_a_log[l]), jnp.asarray(gamma, F32), zeros4, zeros4, zeros4])
        oml_d, ogd_d, ort_d, c1d, n1d, m1d, s1d, r1d = _decode_rec(
            scalars, pt, qkvt, cos_d, sin_d,
            state_mlstm_c[l].transpose(1, 2, 3, 0), state_mlstm_n[l].transpose(1, 2, 0),
            state_mlstm_m[l].T.reshape(N_HEADS, 1, DEC_BATCH),
            state_gdn_s[l].transpose(1, 2, 3, 0), state_ret_s[l].transpose(1, 2, 3, 0))

        gain = head_norm_w[l][None, :]
        wo = w_out[l].astype(BF16)
        nfw = norm_ffn_w[l][None, :]
        rw_hi, rw_lo = _hi_lo(_pad_lanes(jnp.concatenate([router_group_w[l], router_expert_w[l]], axis=1)))
        rb = _pad_lanes(jnp.concatenate([router_group_b[l], router_expert_b[l]])[None, :])
        x1p, h2p, lgp = _outproj(xp, osb_p, oml_p, ogd_p, ort_p, gain, wo, nfw, rw_hi, rw_lo, False)
        x1d, h2d, lgd = _outproj(xd, osb_d, oml_d, ogd_d, ort_d, gain, wo, nfw, rw_hi, rw_lo, True)
        prp, rfp, tip, cnt_p = _route(lgp, rb, tri_rp, lanes_lt, jnp.zeros((1, LANES), F32))
        prd, rfd, tid, cnt = _route(lgd, rb, tri_rd, lanes_lt, cnt_p)

        offs, te, tile_src = _dispatch_plan(cnt[0, :N_EXPERTS].astype(I32))
        plan_p = (offs,) + _tile_plan(tip)
        plan_d = (offs,) + _tile_plan(tid)
        xs = _dispatch(plan_p, h2p, prp, jnp.zeros((N_SLOTS, D_MODEL), F32))
        xs = _dispatch(plan_d, h2d, prd, xs)
        ys = _experts(te, tile_src, xs, expert_w_gate, expert_w_up, expert_w_down, l)
        final = l == DEPTH - 1
        fw = final_norm_w[None, :]
        xp_next = _combine(plan_p, x1p, rfp, fw, ys, final)
        xd_next = _combine(plan_d, x1d, rfd, fw, ys, final)
        if final:
            yp, yd = xp_next, xd_next
        else:
            xp, xd = xp_next, xd_next

        heads_p = lambda a: a.reshape(BATCH, SEQ, N_HEADS, D)
        heads_t = lambda a: a.reshape(N_HEADS, D, DEC_BATCH).transpose(2, 0, 1)[:, None]
        outs["kp"].append(heads_p(sbk))
        outs["vp"].append(heads_p(sbv))
        outs["ks"].append(heads_t(pt[GROUP_W:2 * GROUP_W]))
        outs["vs"].append(heads_t(pt[2 * GROUP_W:3 * GROUP_W]))
        outs["cp"].append(c1p)
        outs["np"].append(n1p)
        outs["mp"].append(m1p[:, 0, :N_HEADS])
        outs["cs"].append(c1d.transpose(3, 0, 1, 2))
        outs["ns"].append(n1d.transpose(2, 0, 1))
        outs["ms"].append(m1d[:, 0, :].T)
        outs["gp"].append(s1p)
        outs["gcp"].append(cv1p)
        outs["gs"].append(s1d.transpose(3, 0, 1, 2))
        outs["gcs"].append(cv1d.transpose(1, 0, 2))
        outs["rp"].append(r1p)
        outs["rs"].append(r1d.transpose(3, 0, 1, 2))

    st = lambda k: jnp.stack(outs[k], axis=0)
    return (yp.reshape(BATCH, SEQ, D_MODEL), yd.reshape(DEC_BATCH, 1, D_MODEL),
            st("kp"), st("vp"), st("ks"), st("vs"),
            st("cp"), st("np"), st("mp"), st("cs"), st("ns"), st("ms"),
            st("gp"), st("gcp"), st("gs"), st("gcs"), st("rp"), st("rs"))
```

```python
import functools
import math

import numpy as np
import jax
import jax.numpy as jnp
from jax import lax
from jax.experimental import pallas as pl
from jax.experimental.pallas import tpu as pltpu

F32 = jnp.float32
BF16 = jnp.bfloat16
I32 = jnp.int32

D_MODEL = 1024
BATCH = 8
SEQ = 2048
DEPTH = 2
DEC_BATCH = 128
PAST_LEN = 2048
PAGE_SIZE = 128
N_PAGES = PAST_LEN // PAGE_SIZE
HEAD_DIM = 64
N_HEADS = 4
GROUP_W = N_HEADS * HEAD_DIM
CHUNK = 128
GDN_CHUNK = 64
CONV_W = 4
N_GROUPS = 4
EXPERTS_PER_GROUP = 8
N_EXPERTS = N_GROUPS * EXPERTS_PER_GROUP
D_EXPERT = D_MODEL // 2
ROPE_BASE = 10000.0
EPS = 1e-6
SCALE = HEAD_DIM ** -0.5

N_PROMPT = BATCH * SEQ
LANES = 128
N_MAIN = 3 * GROUP_W + 3 * 4 * GROUP_W
N_PROJ = N_MAIN + LANES
COL_ML = 3 * GROUP_W
COL_GD = COL_ML + 4 * GROUP_W
COL_RT = COL_GD + 4 * GROUP_W
COL_GATES = N_MAIN

MLSTM_STEP = 2 * CHUNK
GDN_STEP = 4 * GDN_CHUNK
RET_STEP = 4 * CHUNK
SB_BLOCK = 256
SB_DEC_SEQS = 4
TM_PROMPT = 512
TM_EXPERT = 512
N_ASSIGN = 2 * (N_PROMPT + DEC_BATCH)
TM_ROUTE = 512
RUN_ALIGN = 8
N_TOKEN_TILES = N_PROMPT // TM_ROUTE + 1
N_TILES = -(-(N_ASSIGN + N_TOKEN_TILES * N_EXPERTS * (RUN_ALIGN - 1)) // TM_EXPERT) + N_EXPERTS
N_SLOTS = N_TILES * TM_EXPERT
VMEM_LIMIT = 48 * 1024 * 1024

_NT = (((1,), (1,)), ((), ()))
_TN = (((0,), (0,)), ((), ()))


_NN = (((1,), (0,)), ((), ()))


def _mm(a, b, dims):
    return lax.dot_general(a.astype(BF16), b.astype(BF16), dims, preferred_element_type=F32)


def _dot(a, b):
    return _mm(a, b, _NN)


def _dot_nt(a, b):
    return _mm(a, b, _NT)


def _dot_tn(a, b):
    return _mm(a, b, _TN)


def _split3(x):
    x1 = x.astype(BF16)
    r1 = x - x1.astype(F32)
    x2 = r1.astype(BF16)
    x3 = (r1 - x2.astype(F32)).astype(BF16)
    return x1, x2, x3


def _tri_dot(tri, x):
    x1, x2, x3 = _split3(x)
    return _dot(tri, x1) + _dot(tri, x2) + _dot(tri, x3)


def _log_sigmoid(z):
    return jnp.minimum(z, 0.0) - jnp.log(1.0 + jnp.exp(-jnp.abs(z)))


def _softplus(z):
    return jnp.maximum(z, 0.0) + jnp.log(1.0 + jnp.exp(-jnp.abs(z)))


def _sigmoid(z):
    return 1.0 / (1.0 + jnp.exp(-z))


def _head_rms(x):
    return x * lax.rsqrt(jnp.mean(x * x, axis=-1, keepdims=True) + EPS)


def _cparams(sem):
    return pltpu.CompilerParams(dimension_semantics=sem, vmem_limit_bytes=VMEM_LIMIT)


def _const_spec(shape):
    nd = len(shape)
    return pl.BlockSpec(shape, lambda *_: (0,) * nd)


def _norm_rows(x, w):
    return x * lax.rsqrt(jnp.mean(x * x, axis=-1, keepdims=True) + EPS) * w


def _gates_dot(h, wgh_ref, wgl_ref):
    hb = h.astype(BF16)
    hl = (h - hb.astype(F32)).astype(BF16)
    return _dot(hb, wgh_ref[...]) + _dot(hl, wgh_ref[...]) + _dot(hb, wgl_ref[...])


def _inproj_prompt_kernel(x_ref, nw_ref, w_ref, wgh_ref, wgl_ref,
                          sbq_ref, sbk_ref, sbv_ref, ml_ref, gd_ref, rt_ref, gt_ref):
    h = _norm_rows(x_ref[...], nw_ref[...])
    hb = h.astype(BF16)
    seg = lambda a, b: _dot(hb, w_ref[:, a:b])
    sbq_ref[...] = seg(0, GROUP_W)
    sbk_ref[...] = seg(GROUP_W, 2 * GROUP_W)
    sbv_ref[...] = seg(2 * GROUP_W, 3 * GROUP_W)
    ml_ref[...] = seg(COL_ML, COL_GD)
    gd_ref[...] = seg(COL_GD, COL_RT)
    rt_ref[...] = seg(COL_RT, N_MAIN)
    gt_ref[...] = _gates_dot(h, wgh_ref, wgl_ref)


def _inproj_prompt(x, nw, w, wgh, wgl):
    n = x.shape[0]
    tm = TM_PROMPT
    row = lambda c: pl.BlockSpec((tm, c), lambda i: (i, 0))
    S = jax.ShapeDtypeStruct
    return pl.pallas_call(
        _inproj_prompt_kernel,
        grid=(n // tm,),
        in_specs=[row(D_MODEL), _const_spec((1, D_MODEL)), _const_spec((D_MODEL, N_MAIN)),
                  _const_spec((D_MODEL, LANES)), _const_spec((D_MODEL, LANES))],
        out_specs=[row(GROUP_W), row(GROUP_W), row(GROUP_W), row(4 * GROUP_W), row(4 * GROUP_W),
                   row(4 * GROUP_W), row(LANES)],
        out_shape=[S((n, GROUP_W), F32)] * 3 + [S((n, 4 * GROUP_W), F32)] * 3 + [S((n, LANES), F32)],
        compiler_params=_cparams(("parallel",)),
        name="inproj_prompt",
    )(x, nw, w, wgh, wgl)


def _inproj_decode_kernel(x_ref, nw_ref, w_ref, wgh_ref, wgl_ref, sbq_ref, gdraw_ref, pt_ref):
    h = _norm_rows(x_ref[...], nw_ref[...])
    hb = h.astype(BF16)
    for j in range(N_MAIN // LANES):
        p = _dot(hb, w_ref[:, j * LANES:(j + 1) * LANES])
        if j < GROUP_W // LANES:
            sbq_ref[:, j * LANES:(j + 1) * LANES] = p
        c0 = j * LANES - COL_GD
        if 0 <= c0 < 3 * GROUP_W:
            gdraw_ref[:, c0:c0 + LANES] = p
        pt_ref[j * LANES:(j + 1) * LANES, :] = p.T
    pt_ref[N_MAIN:N_PROJ, :] = _gates_dot(h, wgh_ref, wgl_ref).T


def _inproj_decode(x, nw, w, wgh, wgl):
    S = jax.ShapeDtypeStruct
    return pl.pallas_call(
        _inproj_decode_kernel,
        out_shape=[S((DEC_BATCH, GROUP_W), F32), S((DEC_BATCH, 3 * GROUP_W), F32), S((N_PROJ, DEC_BATCH), F32)],
        compiler_params=_cparams(None),
        name="inproj_decode",
    )(x, nw, w, wgh, wgl)


def _sb_prompt_kernel(bias_ref, q_ref, k_ref, v_ref, tri_ref, o_ref, acc_s, carry_s):
    tb = SB_BLOCK
    qi = pl.program_id(1)
    tri = tri_ref[...]
    r = lax.broadcasted_iota(I32, (tb, tb), 0)
    c = lax.broadcasted_iota(I32, (tb, tb), 1)
    dmask = c < r
    acc_s[...] = jnp.zeros_like(acc_s)
    carry_s[...] = jnp.zeros_like(carry_s)
    q = q_ref[...] * SCALE

    def block(j, mask):
        start = pl.multiple_of(j * tb, tb)
        kj = k_ref[pl.ds(start, tb), :]
        vj = v_ref[pl.ds(start, tb), :]
        heads = range(N_HEADS)
        sls = [slice(h * HEAD_DIM, (h + 1) * HEAD_DIM) for h in heads]
        z = [_dot_nt(q[:, sls[h]], kj[:, sls[h]]) + bias_ref[h] for h in heads]
        lsz = [_log_sigmoid(z[h]) for h in heads]
        lk = [lsz[h] - z[h] for h in heads]
        if mask is not None:
            lk = [jnp.where(mask, x, 0.0) for x in lk]
        lw_all = _dot(jnp.concatenate([x.astype(BF16) for x in lk], axis=0), tri)
        lw = [lw_all[h * tb:(h + 1) * tb, :] for h in heads]
        a = [jnp.exp(lsz[h] + lw[h] + carry_s[:, h:h + 1]) for h in heads]
        if mask is not None:
            a = [jnp.where(mask, x, 0.0) for x in a]
        av = [_dot(a[h], vj[:, sls[h]]) for h in heads]
        for h in heads:
            acc_s[:, sls[h]] += av[h]
            carry_s[:, h:h + 1] += lw[h][:, 0:1] + lk[h][:, 0:1]

    block(qi, dmask)

    def body(it, _):
        block(qi - 1 - it, None)
        return 0

    lax.fori_loop(0, qi, body, 0)
    for h in range(N_HEADS):
        sl = slice(h * HEAD_DIM, (h + 1) * HEAD_DIM)
        o_ref[:, sl] = _head_rms(acc_s[:, sl])


def _sb_prompt(q, k, v, bias, tri):
    tb = SB_BLOCK
    nq = SEQ // tb
    return pl.pallas_call(
        _sb_prompt_kernel,
        grid_spec=pltpu.PrefetchScalarGridSpec(
            num_scalar_prefetch=0,
            grid=(BATCH, nq),
            in_specs=[pl.BlockSpec(memory_space=pltpu.SMEM),
                      pl.BlockSpec((tb, GROUP_W), lambda b, i: (b * nq + i, 0)),
                      pl.BlockSpec((SEQ, GROUP_W), lambda b, i: (b, 0)),
                      pl.BlockSpec((SEQ, GROUP_W), lambda b, i: (b, 0)),
                      _const_spec((tb, tb))],
            out_specs=pl.BlockSpec((tb, GROUP_W), lambda b, i: (b * nq + i, 0)),
            scratch_shapes=[pltpu.VMEM((tb, GROUP_W), F32), pltpu.VMEM((tb, LANES), F32)]),
        out_shape=jax.ShapeDtypeStruct((N_PROMPT, GROUP_W), F32),
        compiler_params=_cparams(("parallel", "parallel")),
        name="sb_prompt",
    )(bias, q, k, v, tri)


def _sb_decode_kernel(pt_ref, bias_ref, q_ref, *rest):
    S = SB_DEC_SEQS
    n_in = S * N_PAGES
    k_refs = rest[:n_in]
    v_refs = rest[n_in:2 * n_in]
    tri_ref, pg_ref, o_ref = rest[2 * n_in:]
    n_rows = 8 * N_PAGES
    r0 = (pl.program_id(0) * S) % 8
    seqs = range(S)
    rowi = lax.broadcasted_iota(I32, (8, GROUP_W), 0)
    lanei = lax.broadcasted_iota(I32, (8, GROUP_W), 1)
    head_of_lane = jnp.right_shift(lanei, 6)
    row8 = lax.broadcasted_iota(I32, (8, 1), 0)
    bias = jnp.zeros((8, 1), F32)
    for h in range(N_HEADS):
        bias = jnp.where(row8 == h, bias_ref[h], bias)
    qbd = [jnp.where(head_of_lane == rowi, q_ref[pl.ds(r0 + s, 1), :] * SCALE, 0.0).astype(BF16) for s in seqs]
    z = [jnp.concatenate([_dot(qbd[s], k_refs[s * N_PAGES + j][0, 0]) + bias for j in range(N_PAGES)], axis=0)
         for s in seqs]
    lsz = [_log_sigmoid(z[s]) for s in seqs]
    lk = [lsz[s] - z[s] for s in seqs]
    lw = [_dot(lk[s], tri_ref[...]) for s in seqs]
    tot = [jnp.broadcast_to(lw[s][:, 0:1] + lk[s][:, 0:1], (n_rows, LANES)) for s in seqs]
    later_pages = [_tri_dot(pg_ref[...], tot[s]) for s in seqs]
    a = [jnp.exp(lsz[s] + lw[s] + later_pages[s]) for s in seqs]
    for s in seqs:
        acc = jnp.zeros((8, GROUP_W), F32)
        for j in range(N_PAGES):
            acc = acc + _dot_nt(a[s][j * 8:(j + 1) * 8, :], v_refs[s * N_PAGES + j][0, 0])
        orow = jnp.sum(jnp.where(head_of_lane == rowi, acc, 0.0), axis=0, keepdims=True)
        pieces = [_head_rms(orow[:, h * HEAD_DIM:(h + 1) * HEAD_DIM]) for h in range(N_HEADS)]
        o_ref[pl.ds(r0 + s, 1), :] = jnp.concatenate(pieces, axis=1)


def _sb_decode(q, cache_kt, cache_vt, page_table, bias, tri, later_pages, layer):
    S = SB_DEC_SEQS

    def page_spec(s, j):
        return pl.BlockSpec((1, 1, GROUP_W, PAGE_SIZE), lambda i, pt: (layer, pt[i * S + s, j], 0, 0))

    pages = [page_spec(s, j) for s in range(S) for j in range(N_PAGES)]
    in_specs = ([pl.BlockSpec(memory_space=pltpu.SMEM),
                 pl.BlockSpec((8, GROUP_W), lambda i, pt: (i * S // 8, 0))]
                + pages * 2
                + [pl.BlockSpec((PAGE_SIZE, PAGE_SIZE), lambda i, pt: (0, 0)),
                   pl.BlockSpec((8 * N_PAGES, 8 * N_PAGES), lambda i, pt: (0, 0))])
    return pl.pallas_call(
        _sb_decode_kernel,
        grid_spec=pltpu.PrefetchScalarGridSpec(
            num_scalar_prefetch=1,
            grid=(DEC_BATCH // S,),
            in_specs=in_specs,
            out_specs=pl.BlockSpec((8, GROUP_W), lambda i, pt: (i * S // 8, 0))),
        out_shape=jax.ShapeDtypeStruct((DEC_BATCH, GROUP_W), F32),
        compiler_params=_cparams(("arbitrary",)),
        name="sb_decode",
    )(page_table, bias, q, *([cache_kt] * (S * N_PAGES)), *([cache_vt] * (S * N_PAGES)), tri, later_pages)


def _chunk_masks(L):
    r = lax.broadcasted_iota(I32, (L, L), 0)
    c = lax.broadcasted_iota(I32, (L, L), 1)
    return c <= r, c < r


def _head_cols(blk, h):
    return slice(blk * GROUP_W + h * HEAD_DIM, blk * GROUP_W + (h + 1) * HEAD_DIM)


def _mlstm_prompt_kernel(x_ref, g_ref, gb_ref, tril_ref, o_ref, c1_ref, n1_ref, m1_ref, c_s, n_s, m_s):
    L = CHUNK
    R = x_ref.shape[0]
    ci = pl.program_id(1)

    @pl.when(ci == 0)
    def _():
        c_s[...] = jnp.zeros_like(c_s)
        n_s[...] = jnp.zeros_like(n_s)
        m_s[...] = jnp.zeros_like(m_s)

    g = g_ref[...] + gb_ref[...]
    lane = lax.broadcasted_iota(I32, (R, LANES), 1)
    lf = jnp.where((lane >= N_HEADS) & (lane < 2 * N_HEADS), _log_sigmoid(g), 0.0)
    cum = _tri_dot(tril_ref[...], lf)
    xt = jnp.where(lane < N_HEADS, g, cum).T
    causal, _ = _chunk_masks(L)
    P = [(s_i, h) for s_i in range(R // L) for h in range(N_HEADS)]
    rows = [slice(s_i * L, (s_i + 1) * L) for s_i, _ in P]
    hs = [h for _, h in P]
    np_ = range(len(P))
    bc = [cum[rows[p], N_HEADS + hs[p]:N_HEADS + hs[p] + 1] for p in np_]
    li = [g[rows[p], hs[p]:hs[p] + 1] for p in np_]
    log_d = [jnp.where(causal, bc[p] - xt[N_HEADS + hs[p]:N_HEADS + hs[p] + 1, rows[p]] + xt[hs[p]:hs[p] + 1, rows[p]],
                       -jnp.inf) for p in np_]
    b_last = [bc[p][L - 1:L, :] for p in np_]
    log_w = [b_last[p] - bc[p] + li[p] for p in np_]
    d_max = [jnp.max(log_d[p], axis=1, keepdims=True) for p in np_]
    w_max = [jnp.max(log_w[p], axis=0, keepdims=True) for p in np_]
    m = [m_s[0:1, h:h + 1] for h in range(N_HEADS)]
    m_in, m_out = [], []
    for p in np_:
        m_in.append(m[hs[p]])
        m[hs[p]] = jnp.maximum(b_last[p] + m[hs[p]], w_max[p])
        m_out.append(m[hs[p]])
    log_inter = [bc[p] + m_in[p] for p in np_]
    m_row = [jnp.maximum(log_inter[p], d_max[p]) for p in np_]
    dexp = [jnp.exp(log_d[p] - m_row[p]) for p in np_]
    w_inter = [jnp.exp(log_inter[p] - m_row[p]) for p in np_]
    decay = [jnp.exp(b_last[p] + m_in[p] - m_out[p]) for p in np_]
    q = [x_ref[rows[p], _head_cols(0, hs[p])] for p in np_]
    k = [x_ref[rows[p], _head_cols(1, hs[p])] * SCALE for p in np_]
    v = [x_ref[rows[p], _head_cols(2, hs[p])] for p in np_]
    kw = [k[p] * jnp.exp(log_w[p] - m_out[p]) for p in np_]
    s_mat = [_dot_nt(q[p], k[p]) * dexp[p] for p in np_]
    sv = [_dot(s_mat[p], v[p]) for p in np_]
    kv = [_dot_tn(kw[p], v[p]) for p in np_]
    kw_sum = [jnp.sum(kw[p], axis=0, keepdims=True) for p in np_]
    c = [c_s[h] for h in range(N_HEADS)]
    n = [n_s[h:h + 1, :] for h in range(N_HEADS)]
    c_in, n_in = [], []
    for p in np_:
        c_in.append(c[hs[p]])
        n_in.append(n[hs[p]])
        c[hs[p]] = decay[p] * c[hs[p]] + kv[p]
        n[hs[p]] = decay[p] * n[hs[p]] + kw_sum[p]
    qc = [_dot(q[p], c_in[p]) for p in np_]
    s_sum = [jnp.sum(s_mat[p], axis=1, keepdims=True) for p in np_]
    qn = [jnp.sum(q[p] * n_in[p], axis=1, keepdims=True) for p in np_]
    hh = [(sv[p] + w_inter[p] * qc[p]) / jnp.maximum(jnp.abs(s_sum[p] + w_inter[p] * qn[p]), jnp.exp(-m_row[p]))
          for p in np_]
    ms = [jnp.mean(hh[p] * hh[p], axis=1, keepdims=True) for p in np_]
    for p in np_:
        og = x_ref[rows[p], _head_cols(3, hs[p])]
        o_ref[rows[p], hs[p] * HEAD_DIM:(hs[p] + 1) * HEAD_DIM] = hh[p] * lax.rsqrt(ms[p] + EPS) * _sigmoid(og)
    for h in range(N_HEADS):
        c_s[h] = c[h]
        n_s[h:h + 1, :] = n[h]
        m_s[0:1, h:h + 1] = m[h]

    @pl.when(ci == pl.num_programs(1) - 1)
    def _():
        c1_ref[0] = c_s[...]
        n1_ref[0] = n_s[...]
        m1_ref[0] = m_s[...]


def _mlstm_prompt(ml, gates, gate_bias_row, tril):
    L = MLSTM_STEP
    nc = SEQ // L
    S = jax.ShapeDtypeStruct
    return pl.pallas_call(
        _mlstm_prompt_kernel,
        grid=(BATCH, nc),
        in_specs=[pl.BlockSpec((L, 4 * GROUP_W), lambda b, c: (b * nc + c, 0)),
                  pl.BlockSpec((L, LANES), lambda b, c: (b * nc + c, 0)),
                  _const_spec((1, LANES)), _const_spec((L, L))],
        out_specs=[pl.BlockSpec((L, GROUP_W), lambda b, c: (b * nc + c, 0)),
                   pl.BlockSpec((1, N_HEADS, HEAD_DIM, HEAD_DIM), lambda b, c: (b, 0, 0, 0)),
                   pl.BlockSpec((1, N_HEADS, HEAD_DIM), lambda b, c: (b, 0, 0)),
                   pl.BlockSpec((1, 1, LANES), lambda b, c: (b, 0, 0))],
        out_shape=[S((N_PROMPT, GROUP_W), F32), S((BATCH, N_HEADS, HEAD_DIM, HEAD_DIM), F32),
                   S((BATCH, N_HEADS, HEAD_DIM), F32), S((BATCH, 1, LANES), F32)],
        scratch_shapes=[pltpu.VMEM((N_HEADS, HEAD_DIM, HEAD_DIM), F32), pltpu.VMEM((N_HEADS, HEAD_DIM), F32),
                        pltpu.VMEM((1, LANES), F32)],
        compiler_params=_cparams(("parallel", "arbitrary")),
        name="mlstm_prompt",
    )(ml, gates, gate_bias_row, tril)


def _unit_lower_inverse(a, L):
    r = lax.broadcasted_iota(I32, (L, L), 0)
    c = lax.broadcasted_iota(I32, (L, L), 1)
    p = jnp.where(r == c, 1.0, 0.0) - a
    x = a
    power = 1
    while 2 * power < L:
        x = _dot(x, x)
        p = p + _dot(p, x)
        power *= 2
    return p


def _gdn_prompt_kernel(x_ref, g_ref, gb_ref, al_ref, cw_ref, tril_ref, o_ref, s1_ref, cv_ref, s_s, xe_s):
    L = GDN_CHUNK
    R = x_ref.shape[0]
    nq = 3 * GROUP_W
    ci = pl.program_id(1)

    @pl.when(ci == 0)
    def _():
        s_s[...] = jnp.zeros_like(s_s)
        xe_s[0:8, :] = jnp.zeros((8, nq), F32)

    raw = x_ref[:, 0:nq]
    xe_s[8:8 + R, :] = raw
    conv = (cw_ref[3:4, :] * raw + cw_ref[2:3, :] * xe_s[7:7 + R, :]
            + cw_ref[1:2, :] * xe_s[6:6 + R, :] + cw_ref[0:1, :] * xe_s[5:5 + R, :])
    xe_s[0:8, :] = raw[R - 8:R, :]
    qkv = conv * _sigmoid(conv)

    g = g_ref[...] + gb_ref[...]
    lane = lax.broadcasted_iota(I32, (R, LANES), 1)
    beta_all = _sigmoid(g)
    gd = jnp.where((lane >= 3 * N_HEADS) & (lane < 4 * N_HEADS), -jnp.exp(al_ref[...]) * _softplus(g), 0.0)
    gcum = _tri_dot(tril_ref[...], gd)
    xt = gcum.T
    incl, strict = _chunk_masks(L)
    prob = [(s_i, h) for s_i in range(R // L) for h in range(N_HEADS)]
    loc = []
    for s_i, h in prob:
        rows = slice(s_i * L, (s_i + 1) * L)
        q = qkv[rows, _head_cols(0, h)]
        k = qkv[rows, _head_cols(1, h)]
        v = qkv[rows, _head_cols(2, h)]
        q = q * lax.rsqrt(jnp.sum(q * q, axis=1, keepdims=True) + EPS) * SCALE
        k = k * lax.rsqrt(jnp.sum(k * k, axis=1, keepdims=True) + EPS)
        b = beta_all[rows, 2 * N_HEADS + h:2 * N_HEADS + h + 1]
        gc = gcum[rows, 3 * N_HEADS + h:3 * N_HEADS + h + 1]
        gc_row = xt[3 * N_HEADS + h:3 * N_HEADS + h + 1, rows]
        decay = jnp.exp(jnp.where(incl, gc - gc_row, -jnp.inf))
        kb = k * b
        egc = jnp.exp(gc)
        gl = gc[L - 1:L, :]
        loc.append(dict(q=q, k=k, kb=kb, decay=decay, vb=v * b, kbe=kb * egc, q_dec=q * egc,
                        k_dec=k * jnp.exp(gl - gc), g_last=jnp.exp(gl)))
    a_mat = [jnp.where(strict, _dot_nt(d["kb"], d["k"]) * d["decay"], 0.0) for d in loc]
    qk_mat = [_dot_nt(d["q"], d["k"]) * d["decay"] for d in loc]
    eye = jnp.where(incl & jnp.logical_not(strict), 1.0, 0.0)
    pw = a_mat
    inv = [eye - a for a in a_mat]
    power = 1
    while 2 * power < L:
        pw = [_dot(x, x) for x in pw]
        inv = [p + _dot(p, x) for p, x in zip(inv, pw)]
        power *= 2
    us = [_dot(t, d["vb"]) for t, d in zip(inv, loc)]
    ws = [_dot(t, d["kbe"]) for t, d in zip(inv, loc)]
    s = [s_s[h] for h in range(N_HEADS)]
    heads = range(N_HEADS)
    for s_i in range(R // L):
        rows = slice(s_i * L, (s_i + 1) * L)
        p0 = s_i * N_HEADS
        w_s = [_dot(ws[p0 + h], s[h]) for h in heads]
        q_s = [_dot(loc[p0 + h]["q_dec"], s[h]) for h in heads]
        v_new = [us[p0 + h] - w_s[h] for h in heads]
        o = [q_s[h] + _dot(qk_mat[p0 + h], v_new[h]) for h in heads]
        s = [s[h] * loc[p0 + h]["g_last"] + _dot_tn(loc[p0 + h]["k_dec"], v_new[h]) for h in heads]
        for h in heads:
            gate = x_ref[rows, _head_cols(3, h)]
            o_ref[rows, h * HEAD_DIM:(h + 1) * HEAD_DIM] = _head_rms(o[h]) * (gate * _sigmoid(gate))
    for h in range(N_HEADS):
        s_s[h] = s[h]

    @pl.when(ci == pl.num_programs(1) - 1)
    def _():
        s1_ref[0] = s_s[...]
        cv_ref[0] = xe_s[8 - (CONV_W - 1):8, :]


def _gdn_prompt(gd, gates, gate_bias_row, a_log_row, conv_w, tril):
    L = GDN_STEP
    nc = SEQ // L
    S = jax.ShapeDtypeStruct
    return pl.pallas_call(
        _gdn_prompt_kernel,
        grid=(BATCH, nc),
        in_specs=[pl.BlockSpec((L, 4 * GROUP_W), lambda b, c: (b * nc + c, 0)),
                  pl.BlockSpec((L, LANES), lambda b, c: (b * nc + c, 0)),
                  _const_spec((1, LANES)), _const_spec((1, LANES)), _const_spec((CONV_W, 3 * GROUP_W)),
                  _const_spec((L, L))],
        out_specs=[pl.BlockSpec((L, GROUP_W), lambda b, c: (b * nc + c, 0)),
                   pl.BlockSpec((1, N_HEADS, HEAD_DIM, HEAD_DIM), lambda b, c: (b, 0, 0, 0)),
                   pl.BlockSpec((1, CONV_W - 1, 3 * GROUP_W), lambda b, c: (b, 0, 0))],
        out_shape=[S((N_PROMPT, GROUP_W), F32), S((BATCH, N_HEADS, HEAD_DIM, HEAD_DIM), F32),
                   S((BATCH, CONV_W - 1, 3 * GROUP_W), F32)],
        scratch_shapes=[pltpu.VMEM((N_HEADS, HEAD_DIM, HEAD_DIM), F32), pltpu.VMEM((8 + L, 3 * GROUP_W), F32)],
        compiler_params=_cparams(("parallel", "arbitrary")),
        name="gdn_prompt",
    )(gd, gates, gate_bias_row, a_log_row, conv_w, tril)


def _rope_rows(x, cos, sin_signed):
    lane = lax.broadcasted_iota(I32, x.shape, 1)
    first = jnp.bitwise_and(lane, HEAD_DIM - 1) < HEAD_DIM // 2
    w = x.shape[1]
    swapped = jnp.where(first, pltpu.roll(x, w - HEAD_DIM // 2, 1), pltpu.roll(x, HEAD_DIM // 2, 1))
    return x * cos + swapped * sin_signed


def _ret_prompt_kernel(x_ref, cos_ref, sin_ref, dm_ref, xz_ref, gch_ref, o_ref, s1_ref, s_s):
    ci = pl.program_id(1)

    @pl.when(ci == 0)
    def _():
        s_s[...] = jnp.zeros_like(s_s)

    L = CHUNK
    cos = cos_ref[...]
    sin = sin_ref[...]
    qr = _rope_rows(x_ref[:, 0:GROUP_W], cos, sin)
    kr = _rope_rows(x_ref[:, GROUP_W:2 * GROUP_W], cos, sin) * SCALE
    prob = []
    for s_i in range(x_ref.shape[0] // L):
        rows = slice(s_i * L, (s_i + 1) * L)
        for h in range(N_HEADS):
            sl = slice(h * HEAD_DIM, (h + 1) * HEAD_DIM)
            prob.append(dict(rows=rows, h=h, sl=sl, q=qr[rows, sl], k=kr[rows, sl], v=x_ref[rows, _head_cols(2, h)]))
    qk = [_dot_nt(p["q"], p["k"]) * dm_ref[p["h"]] for p in prob]
    intra = [_dot(a, p["v"]) for a, p in zip(qk, prob)]
    kv = [_dot_tn(p["k"] * xz_ref[:, N_HEADS + p["h"]:N_HEADS + p["h"] + 1], p["v"]) for p in prob]
    s = [s_s[h] for h in range(N_HEADS)]
    s_in = []
    for p, kv_p in zip(prob, kv):
        s_in.append(s[p["h"]])
        s[p["h"]] = s[p["h"]] * gch_ref[p["h"]] + kv_p
    inter = [_dot(p["q"], s0) * xz_ref[:, p["h"]:p["h"] + 1] for p, s0 in zip(prob, s_in)]
    for p, a, b in zip(prob, intra, inter):
        gate = x_ref[p["rows"], _head_cols(3, p["h"])]
        o_ref[p["rows"], p["sl"]] = _head_rms(a + b) * (gate * _sigmoid(gate))
    for h in range(N_HEADS):
        s_s[h] = s[h]

    @pl.when(ci == pl.num_programs(1) - 1)
    def _():
        s1_ref[0] = s_s[...]


def _ret_prompt(rt, cos, sin, dmat, xz, gch):
    L = RET_STEP
    nc = SEQ // L
    S = jax.ShapeDtypeStruct
    return pl.pallas_call(
        _ret_prompt_kernel,
        grid=(BATCH, nc),
        in_specs=[pl.BlockSpec((L, 4 * GROUP_W), lambda b, c: (b * nc + c, 0)),
                  pl.BlockSpec((L, GROUP_W), lambda b, c: (c, 0)),
                  pl.BlockSpec((L, GROUP_W), lambda b, c: (c, 0)),
                  _const_spec((N_HEADS, CHUNK, CHUNK)), _const_spec((CHUNK, LANES)),
                  pl.BlockSpec(memory_space=pltpu.SMEM)],
        out_specs=[pl.BlockSpec((L, GROUP_W), lambda b, c: (b * nc + c, 0)),
                   pl.BlockSpec((1, N_HEADS, HEAD_DIM, HEAD_DIM), lambda b, c: (b, 0, 0, 0))],
        out_shape=[S((N_PROMPT, GROUP_W), F32), S((BATCH, N_HEADS, HEAD_DIM, HEAD_DIM), F32)],
        scratch_shapes=[pltpu.VMEM((N_HEADS, HEAD_DIM, HEAD_DIM), F32)],
        compiler_params=_cparams(("parallel", "arbitrary")),
        name="ret_prompt",
    )(rt, cos, sin, dmat, xz, gch)


def _gdn_decode_conv_kernel(raw_ref, c0_ref, cw_ref, qkvt_ref, cv_ref):
    raw = raw_ref[...]
    conv = (cw_ref[3:4, :] * raw + cw_ref[2:3, :] * c0_ref[2] + cw_ref[1:2, :] * c0_ref[1]
            + cw_ref[0:1, :] * c0_ref[0])
    qkv = conv * _sigmoid(conv)
    cv_ref[0] = c0_ref[1]
    cv_ref[1] = c0_ref[2]
    cv_ref[2] = raw
    for blk in range(3):
        for h in range(N_HEADS):
            lo = blk * GROUP_W + h * HEAD_DIM
            x = qkv[:, lo:lo + HEAD_DIM]
            if blk == 0:
                x = x * lax.rsqrt(jnp.sum(x * x, axis=1, keepdims=True) + EPS) * SCALE
            elif blk == 1:
                x = x * lax.rsqrt(jnp.sum(x * x, axis=1, keepdims=True) + EPS)
            if h % 2 == 0:
                pair = x
            else:
                qkvt_ref[lo - HEAD_DIM:lo + HEAD_DIM, :] = jnp.concatenate([pair, x], axis=1).T


def _gdn_decode_conv(raw, conv0, conv_w):
    S = jax.ShapeDtypeStruct
    return pl.pallas_call(
        _gdn_decode_conv_kernel,
        out_shape=[S((3 * GROUP_W, DEC_BATCH), F32), S((CONV_W - 1, DEC_BATCH, 3 * GROUP_W), F32)],
        compiler_params=_cparams(None),
        name="gdn_decode_conv",
    )(raw, conv0, conv_w)


def _decode_rec_kernel(sc_ref, mlq_ref, mlk_ref, mlv_ref, mlo_ref, gq_ref, gk_ref, gv_ref, gg_ref,
                       rq_ref, rk_ref, rv_ref, rg_ref, gt_ref, cos_ref, sin_ref,
                       c0_ref, n0_ref, m0_ref, sg0_ref, sr0_ref,
                       oml_ref, ogd_ref, ort_ref, c1_ref, n1_ref, m1_ref, sg1_ref, sr1_ref, va_s, vb_s):
    h = pl.program_id(0)
    D = HEAD_DIM
    sum0 = lambda x: jnp.sum(x, axis=0, keepdims=True)
    rms0 = lambda x: x * lax.rsqrt(sum0(x * x) * (1.0 / D) + EPS)

    li = gt_ref[pl.ds(h, 1), :] + sc_ref[0, h]
    lf = _log_sigmoid(gt_ref[pl.ds(N_HEADS + h, 1), :] + sc_ref[1, h])
    m0 = m0_ref[0]
    q = mlq_ref[...]
    k = mlk_ref[...] * SCALE
    v = mlv_ref[...]
    log_inter = lf + m0
    m_row = jnp.maximum(log_inter, li)
    s = sum0(q * k) * jnp.exp(li - m_row)
    w_inter = jnp.exp(log_inter - m_row)
    decay = jnp.exp(lf + m0 - m_row)
    kw = k * jnp.exp(li - m_row)

    va_s[...] = kw

    def ml_body(d, qc):
        c_d = c0_ref[0, d]
        c1_ref[0, d] = decay * c_d + va_s[pl.ds(d, 1), :] * v
        return qc + mlq_ref[pl.ds(d, 1), :] * c_d

    n0 = n0_ref[0]
    n1_ref[0] = decay * n0 + kw
    qc = lax.fori_loop(0, D, ml_body, jnp.zeros((D, LANES), F32))
    num = s * v + w_inter * qc
    den = s + w_inter * sum0(q * n0)
    hh = num / jnp.maximum(jnp.abs(den), jnp.exp(-m_row))
    m1_ref[0] = m_row
    oml_ref[...] = rms0(hh) * _sigmoid(mlo_ref[...])

    beta = _sigmoid(gt_ref[pl.ds(2 * N_HEADS + h, 1), :])
    gdec = -sc_ref[3, h] * _softplus(gt_ref[pl.ds(3 * N_HEADS + h, 1), :] + sc_ref[2, h])
    eg = jnp.exp(gdec)
    gv = gv_ref[...]

    def ks_body(d, acc):
        return acc + gk_ref[pl.ds(d, 1), :] * sg0_ref[0, d]

    ks = lax.fori_loop(0, D, ks_body, jnp.zeros((D, LANES), F32))
    v_new = beta * gv - (beta * eg) * ks

    def gd_body(d, acc):
        s_new = eg * sg0_ref[0, d] + gk_ref[pl.ds(d, 1), :] * v_new
        sg1_ref[0, d] = s_new
        return acc + gq_ref[pl.ds(d, 1), :] * s_new

    og = lax.fori_loop(0, D, gd_body, jnp.zeros((D, LANES), F32))
    gate = gg_ref[...]
    ogd_ref[...] = rms0(og) * (gate * _sigmoid(gate))

    half = D // 2

    def rope(ref):
        x = ref[...]
        sw = jnp.concatenate([x[half:, :], x[:half, :]], axis=0)
        return x * cos_ref[...] + sw * sin_ref[...]

    va_s[...] = rope(rq_ref)
    vb_s[...] = rope(rk_ref) * SCALE
    rv = rv_ref[...]
    gamma = sc_ref[4, h]

    def rt_body(d, acc):
        s_new = gamma * sr0_ref[0, d] + vb_s[pl.ds(d, 1), :] * rv
        sr1_ref[0, d] = s_new
        return acc + va_s[pl.ds(d, 1), :] * s_new

    ort = lax.fori_loop(0, D, rt_body, jnp.zeros((D, LANES), F32))
    gate = rg_ref[...]
    ort_ref[...] = rms0(ort) * (gate * _sigmoid(gate))


def _decode_rec(scalars, pt, qkvt, cos_t, sin_t, c0, n0, m0, sg0, sr0):
    D = HEAD_DIM
    S = jax.ShapeDtypeStruct
    prow = lambda col, blk: pl.BlockSpec((D, LANES), lambda h: ((col + blk * GROUP_W) // D + h, 0))
    vec = lambda blk: pl.BlockSpec((D, LANES), lambda h: (blk * N_HEADS + h, 0))
    st4 = pl.BlockSpec((1, D, D, LANES), lambda h: (h, 0, 0, 0))
    st3 = pl.BlockSpec((1, D, LANES), lambda h: (h, 0, 0))
    st2 = pl.BlockSpec((1, 1, LANES), lambda h: (h, 0, 0))
    in_specs = ([pl.BlockSpec(memory_space=pltpu.SMEM)]
                + [prow(COL_ML, blk) for blk in range(4)]
                + [vec(0), vec(1), vec(2), prow(COL_GD, 3)]
                + [prow(COL_RT, blk) for blk in range(4)]
                + [pl.BlockSpec((LANES, LANES), lambda h: (COL_GATES // LANES, 0)),
                   _const_spec((D, LANES)), _const_spec((D, LANES)), st4, st3, st2, st4, st4])
    return pl.pallas_call(
        _decode_rec_kernel,
        grid=(N_HEADS,),
        in_specs=in_specs,
        out_specs=[vec(0), vec(0), vec(0), st4, st3, st2, st4, st4],
        out_shape=[S((GROUP_W, LANES), F32)] * 3
        + [S((N_HEADS, D, D, LANES), F32), S((N_HEADS, D, LANES), F32), S((N_HEADS, 1, LANES), F32),
           S((N_HEADS, D, D, LANES), F32), S((N_HEADS, D, D, LANES), F32)],
        scratch_shapes=[pltpu.VMEM((D, LANES), F32), pltpu.VMEM((D, LANES), F32)],
        compiler_params=_cparams(("parallel",)),
        name="decode_rec",
    )(scalars, pt, pt, pt, pt, qkvt, qkvt, qkvt, pt, pt, pt, pt, pt, pt, cos_t, sin_t, c0, n0, m0, sg0, sr0)


def _outproj_kernel(x_ref, osb_ref, oml_ref, ogd_ref, ort_ref, gain_ref, wo_ref, nw_ref, rwh_ref, rwl_ref,
                    x1_ref, h2_ref, lg_ref, *, transposed):
    parts = [osb_ref[...]]
    for ref in (oml_ref, ogd_ref, ort_ref):
        parts.append(ref[...].T if transposed else ref[...])
    y = None
    for g, p in enumerate(parts):
        cols = slice(g * GROUP_W, (g + 1) * GROUP_W)
        t = _dot(p * gain_ref[:, cols], wo_ref[cols, :])
        y = t if y is None else y + t
    x1 = x_ref[...] + y
    h2 = _norm_rows(x1, nw_ref[...])
    x1_ref[...] = x1
    h2_ref[...] = h2.astype(BF16)
    lg_ref[...] = _gates_dot(h2, rwh_ref, rwl_ref)


def _outproj(x, osb, oml, ogd, ort, gain, wo, nw, rwh, rwl, transposed):
    n = x.shape[0]
    tm = min(TM_PROMPT, n)
    S = jax.ShapeDtypeStruct
    row = lambda c: pl.BlockSpec((tm, c), lambda i: (i, 0))
    mix = _const_spec((GROUP_W, DEC_BATCH)) if transposed else row(GROUP_W)
    return pl.pallas_call(
        functools.partial(_outproj_kernel, transposed=transposed),
        grid=(n // tm,),
        in_specs=[row(D_MODEL), row(GROUP_W), mix, mix, mix, _const_spec((1, D_MODEL)),
                  _const_spec((D_MODEL, D_MODEL)), _const_spec((1, D_MODEL)),
                  _const_spec((D_MODEL, LANES)), _const_spec((D_MODEL, LANES))],
        out_specs=[row(D_MODEL), row(D_MODEL), row(LANES)],
        out_shape=[S((n, D_MODEL), F32), S((n, D_MODEL), BF16), S((n, LANES), F32)],
        compiler_params=_cparams(("parallel",)),
        name="outproj",
    )(x, osb, oml, ogd, ort, gain, wo, nw, rwh, rwl)


def _route_kernel(lg_ref, rb_ref, tri_ref, lt_ref, cin_ref, pr_ref, rf_ref, ti_ref, cnt_ref):
    @pl.when(pl.program_id(0) == 0)
    def _():
        cnt_ref[...] = cin_ref[...]

    lg = lg_ref[...] + rb_ref[...]
    lane = lax.broadcasted_iota(I32, lg.shape, 1)
    ninf = -jnp.inf
    big = LANES - 1
    rmax = lambda x: jnp.max(x, axis=1, keepdims=True)
    rmin = lambda x: jnp.min(x, axis=1, keepdims=True)
    gl = jnp.where(lane < N_GROUPS, lg, ninf)
    gmax = rmax(gl)
    g_sel = rmin(jnp.where(gl == gmax, lane, big))
    g_prob = 1.0 / jnp.sum(jnp.exp(gl - gmax), axis=1, keepdims=True)
    e_lane = (lane >= N_GROUPS) & (lane < N_GROUPS + N_EXPERTS)
    em = jnp.where(e_lane & (jnp.right_shift(lane - N_GROUPS, 3) == g_sel), lg, ninf)
    v1 = rmax(em)
    i1 = rmin(jnp.where(em == v1, lane, big))
    em2 = jnp.where(lane == i1, ninf, em)
    v2 = rmax(em2)
    i2 = rmin(jnp.where(em2 == v2, lane, big))
    t = jnp.exp(v2 - v1)
    gate1 = g_prob / (1.0 + t)
    gate2 = g_prob * t / (1.0 + t)
    e1 = i1 - N_GROUPS
    e2 = i2 - N_GROUPS
    onehot = jnp.where((lane == e1) | (lane == e2), 1.0, 0.0)
    before = _dot(tri_ref[...], onehot)
    tile_cnt = jnp.sum(onehot, axis=0, keepdims=True)
    padded = jnp.floor((tile_cnt + (RUN_ALIGN - 1)) * (1.0 / RUN_ALIGN)) * RUN_ALIGN
    run_start = _dot(jnp.broadcast_to(padded, (8, LANES)), lt_ref[...])[0:1, :]
    pos = before + run_start
    p1 = jnp.sum(jnp.where(lane == e1, pos, 0.0), axis=1, keepdims=True)
    p2 = jnp.sum(jnp.where(lane == e2, pos, 0.0), axis=1, keepdims=True)
    pr = jnp.where(lane == 0, p1, jnp.where(lane == 1, p2, 0.0)).astype(I32)
    pr_ref[...] = pr.T[0:8, :]
    rf_ref[...] = jnp.where(lane == 0, gate1, jnp.where(lane == 1, gate2, jnp.where(lane == 2, p1,
                            jnp.where(lane == 3, p2, 0.0))))
    row8 = lax.broadcasted_iota(I32, (8, LANES), 0)
    ti_ref[...] = jnp.where(row8 == 0, padded, jnp.where(row8 == 1, cnt_ref[...],
                            jnp.where(row8 == 2, run_start, 0.0))).astype(I32)
    cnt_ref[...] += padded


def _route(logits, rb, tri, lanes_lt, cnt_in):
    n = logits.shape[0]
    tm = tri.shape[0]
    S = jax.ShapeDtypeStruct
    row = pl.BlockSpec((tm, LANES), lambda i: (i, 0))
    return pl.pallas_call(
        _route_kernel,
        grid=(n // tm,),
        in_specs=[row, _const_spec((1, LANES)), _const_spec((tm, tm)), _const_spec((LANES, LANES)),
                  _const_spec((1, LANES))],
        out_specs=[pl.BlockSpec((8, tm), lambda i: (0, i)), row, pl.BlockSpec((8, LANES), lambda i: (i, 0)),
                   _const_spec((1, LANES))],
        out_shape=[S((8, n), I32), S((n, LANES), F32), S((8 * (n // tm), LANES), I32), S((1, LANES), F32)],
        compiler_params=_cparams(("arbitrary",)),
        name="route",
    )(logits, rb, tri, lanes_lt, cnt_in)


def _packed_rows(tm):
    return 2 * tm + N_EXPERTS * RUN_ALIGN


def _for_each_run_chunk(plan_refs, tile, fn):
    offs_ref, len_ref, before_ref, start_ref = plan_refs
    base = tile * N_EXPERTS

    def per_expert(e, _):
        n_chunks = len_ref[base + e] // RUN_ALIGN
        packed0 = start_ref[base + e]
        slot0 = offs_ref[e] + before_ref[base + e]

        def per_chunk(c, _):
            fn(pl.multiple_of(packed0 + c * RUN_ALIGN, RUN_ALIGN), pl.multiple_of(slot0 + c * RUN_ALIGN, RUN_ALIGN))
            return 0

        lax.fori_loop(0, n_chunks, per_chunk, 0)
        return 0

    lax.fori_loop(0, N_EXPERTS, per_expert, 0)


def _dispatch_kernel(offs_ref, len_ref, before_ref, start_ref, h_ref, pr_ref, xs_in_ref, xs_ref, pk_s, sem):
    del xs_in_ref
    plan_refs = (offs_ref, len_ref, before_ref, start_ref)
    tm = h_ref.shape[0]
    i = pl.program_id(0)
    buf = i % 2
    slot = lax.broadcasted_iota(I32, (_packed_rows(tm), tm), 0)
    sel = jnp.where((slot == pr_ref[0:1, :]) | (slot == pr_ref[1:2, :]), 1.0, 0.0)
    pk_s[buf] = _dot(sel, h_ref[...])

    def copy(b, packed_row, slot_row):
        return pltpu.make_async_copy(pk_s.at[b, pl.ds(packed_row, RUN_ALIGN), :],
                                     xs_ref.at[pl.ds(slot_row, RUN_ALIGN), :], sem.at[b])

    _for_each_run_chunk(plan_refs, i, lambda p, s: copy(buf, p, s).start())

    @pl.when(i > 0)
    def _():
        _for_each_run_chunk(plan_refs, i - 1, lambda p, s: copy(1 - buf, p, s).wait())

    @pl.when(i == pl.num_programs(0) - 1)
    def _():
        _for_each_run_chunk(plan_refs, i, lambda p, s: copy(buf, p, s).wait())


def _dispatch(plan, h2, pr, xs):
    n = h2.shape[0]
    tm = min(TM_ROUTE, n)
    return pl.pallas_call(
        _dispatch_kernel,
        grid_spec=pltpu.PrefetchScalarGridSpec(
            num_scalar_prefetch=4,
            grid=(n // tm,),
            in_specs=[pl.BlockSpec((tm, D_MODEL), lambda i, *_: (i, 0)),
                      pl.BlockSpec((8, tm), lambda i, *_: (0, i)),
                      pl.BlockSpec(memory_space=pl.ANY)],
            out_specs=pl.BlockSpec(memory_space=pl.ANY),
            scratch_shapes=[pltpu.VMEM((2, _packed_rows(tm), D_MODEL), F32), pltpu.SemaphoreType.DMA((2,))]),
        out_shape=jax.ShapeDtypeStruct(xs.shape, xs.dtype),
        input_output_aliases={6: 0},
        compiler_params=_cparams(("arbitrary",)),
        name="dispatch",
    )(*plan, h2, pr, xs)


def _experts_kernel(te_ref, ts_ref, xs_ref, wg_ref, wu_ref, wd_ref, ys_ref, wg_s, wu_s, wd_s):
    i = pl.program_id(0)
    valid = ts_ref[i] == i
    fresh = (i == 0) | (te_ref[i] != te_ref[jnp.maximum(i - 1, 0)])

    @pl.when(valid & fresh)
    def _():
        wg_s[...] = wg_ref[0].astype(BF16)
        wu_s[...] = wu_ref[0].astype(BF16)
        wd_s[...] = wd_ref[0].astype(BF16)

    @pl.when(valid)
    def _():
        x = xs_ref[...].astype(BF16)
        a = _dot(x, wg_s[...])
        u = _dot(x, wu_s[...])
        act = a * _sigmoid(a) * u
        ys_ref[...] = _dot(act, wd_s[...])

    @pl.when(jnp.logical_not(valid))
    def _():
        ys_ref[...] = jnp.zeros_like(ys_ref)


def _experts(tile_expert, tile_src, xs, wg, wu, wd, layer):
    tm = TM_EXPERT
    wspec = lambda a, b: pl.BlockSpec((None, 1, a, b), lambda i, te, tv: (layer, te[i], 0, 0))
    return pl.pallas_call(
        _experts_kernel,
        grid_spec=pltpu.PrefetchScalarGridSpec(
            num_scalar_prefetch=2,
            grid=(N_TILES,),
            in_specs=[pl.BlockSpec((tm, D_MODEL), lambda i, te, ts: (ts[i], 0)),
                      wspec(D_MODEL, D_EXPERT), wspec(D_MODEL, D_EXPERT), wspec(D_EXPERT, D_MODEL)],
            out_specs=pl.BlockSpec((tm, D_MODEL), lambda i, te, tv: (i, 0)),
            scratch_shapes=[pltpu.VMEM((D_MODEL, D_EXPERT), BF16), pltpu.VMEM((D_MODEL, D_EXPERT), BF16),
                            pltpu.VMEM((D_EXPERT, D_MODEL), BF16)]),
        out_shape=jax.ShapeDtypeStruct((N_SLOTS, D_MODEL), F32),
        compiler_params=_cparams(("arbitrary",)),
        name="experts",
    )(tile_expert, tile_src, xs, wg, wu, wd)


def _combine_kernel(offs_ref, len_ref, before_ref, start_ref, x1_ref, rf_ref, fw_ref, ys_ref, out_ref,
                    pk_s, sem, *, final):
    plan_refs = (offs_ref, len_ref, before_ref, start_ref)
    tm = x1_ref.shape[0]
    i = pl.program_id(0)
    buf = i % 2

    def copy(b, packed_row, slot_row):
        return pltpu.make_async_copy(ys_ref.at[pl.ds(slot_row, RUN_ALIGN), :],
                                     pk_s.at[b, pl.ds(packed_row, RUN_ALIGN), :], sem.at[b])

    @pl.when(i == 0)
    def _():
        pk_s[...] = jnp.zeros_like(pk_s)
        _for_each_run_chunk(plan_refs, i, lambda p, s: copy(buf, p, s).start())

    @pl.when(i + 1 < pl.num_programs(0))
    def _():
        _for_each_run_chunk(plan_refs, i + 1, lambda p, s: copy(1 - buf, p, s).start())

    _for_each_run_chunk(plan_refs, i, lambda p, s: copy(buf, p, s).wait())
    slot = lax.broadcasted_iota(I32, (tm, _packed_rows(tm)), 1)
    packed = pk_s[buf].astype(BF16)
    x2 = x1_ref[...]
    for k in range(2):
        sel = jnp.where(slot == rf_ref[:, 2 + k:3 + k].astype(I32), 1.0, 0.0)
        x2 = x2 + rf_ref[:, k:k + 1] * _dot(sel, packed)
    out_ref[...] = _norm_rows(x2, fw_ref[...]) if final else x2


def _combine(plan, x1, rf, fw, ys, final):
    n = x1.shape[0]
    tm = min(TM_ROUTE, n)
    row = lambda c: pl.BlockSpec((tm, c), lambda i, *_: (i, 0))
    return pl.pallas_call(
        functools.partial(_combine_kernel, final=final),
        grid_spec=pltpu.PrefetchScalarGridSpec(
            num_scalar_prefetch=4,
            grid=(n // tm,),
            in_specs=[row(D_MODEL), row(LANES), pl.BlockSpec((1, D_MODEL), lambda i, *_: (0, 0)),
                      pl.BlockSpec(memory_space=pl.ANY)],
            out_specs=row(D_MODEL),
            scratch_shapes=[pltpu.VMEM((2, _packed_rows(tm), D_MODEL), F32), pltpu.SemaphoreType.DMA((2,))]),
        out_shape=jax.ShapeDtypeStruct((n, D_MODEL), F32),
        compiler_params=_cparams(("arbitrary",)),
        name="combine",
    )(*plan, x1, rf, fw, ys)


def _tri(n, kind, chunk=1):
    r = np.arange(n)[:, None]
    c = np.arange(n)[None, :]
    m = {"chunk_lower_incl": (c <= r) & (r // chunk == c // chunk),
         "lower_strict": c < r,
         "row_gt_col": r > c,
         "row_lt_col": r < c,
         "later_page": (r % 8 == c % 8) & (c // 8 > r // 8)}[kind]
    return jnp.asarray(m.astype(np.float32), dtype=BF16)


def _rope_tables(pos):
    half = HEAD_DIM // 2
    inv = ROPE_BASE ** (-np.arange(half, dtype=np.float64) / half)
    ang = np.asarray(pos, np.float64)[:, None] * inv[None, :]
    cos = np.concatenate([np.cos(ang), np.cos(ang)], axis=1)
    sin = np.concatenate([-np.sin(ang), np.sin(ang)], axis=1)
    return cos.astype(np.float32), sin.astype(np.float32)


def _retention_consts(L):
    log_gamma = np.log1p(-np.exp2(-5.0 - np.arange(N_HEADS, dtype=np.float64)))
    idx = np.arange(L, dtype=np.float64)
    diff = np.maximum(idx[:, None] - idx[None, :], 0.0)
    dmat = np.where(idx[None, :] <= idx[:, None], np.exp(log_gamma[:, None, None] * diff), 0.0)
    xi = np.exp(log_gamma[:, None] * (idx + 1.0))
    zeta = np.exp(log_gamma[:, None] * (L - 1.0 - idx))
    xz = np.zeros((L, LANES), np.float64)
    xz[:, 0:N_HEADS] = xi.T
    xz[:, N_HEADS:2 * N_HEADS] = zeta.T
    return (jnp.asarray(dmat, F32), jnp.asarray(xz, F32), jnp.asarray(np.exp(log_gamma * L), F32),
            np.exp(log_gamma))


def _hi_lo(w):
    hi = w.astype(BF16)
    return hi, (w - hi.astype(F32)).astype(BF16)


def _pad_lanes(w):
    return jnp.pad(w, ((0, 0), (0, LANES - w.shape[1])))


def _prep_w_in(w):
    ml0 = 3 * GROUP_W
    mlg = ml0 + 4 * GROUP_W
    gd0 = mlg + 2 * N_HEADS
    gdg = gd0 + 4 * GROUP_W
    rt0 = gdg + 2 * N_HEADS
    main = jnp.concatenate([w[:, 0:ml0], w[:, ml0:mlg], w[:, gd0:gdg], w[:, rt0:]], axis=1).astype(BF16)
    gates = _pad_lanes(jnp.concatenate([w[:, mlg:gd0], w[:, gdg:rt0]], axis=1))
    return (main,) + _hi_lo(gates)


def _tile_plan(tile_info):
    t = tile_info.reshape(-1, 8, LANES)[:, :, :N_EXPERTS]
    return t[:, 0].reshape(-1), t[:, 1].reshape(-1), t[:, 2].reshape(-1)


def _dispatch_plan(cnt):
    ntile = (cnt + TM_EXPERT - 1) // TM_EXPERT
    tile_end = jnp.cumsum(ntile)
    offs = (tile_end - ntile) * TM_EXPERT
    tid = jnp.arange(N_TILES, dtype=I32)
    te = jnp.minimum(jnp.sum(tid[:, None] >= tile_end[None, :], axis=1), N_EXPERTS - 1).astype(I32)
    tile_src = jnp.minimum(tid, tile_end[-1] - 1).astype(I32)
    return offs.astype(I32), te, tile_src


def kernel(x_prompt, x_sample, cache_sb_k, cache_sb_v, state_mlstm_c, state_mlstm_n, state_mlstm_m, state_gdn_s, state_gdn_conv, state_ret_s, page_table, norm_attn_w, w_in, sb_logit_bias, mlstm_gate_bias, gdn_conv_w, gdn_a_log, gdn_dt_bias, head_norm_w, w_out, norm_ffn_w, router_group_w, router_group_b, router_expert_w, router_expert_b, expert_w_gate, expert_w_up, expert_w_down, final_norm_w):
    D = HEAD_DIM
    xp = x_prompt.reshape(N_PROMPT, D_MODEL)
    xd = x_sample.reshape(DEC_BATCH, D_MODEL)
    n_pool = cache_sb_k.shape[1]
    cache_kt = cache_sb_k.transpose(0, 1, 3, 4, 2).reshape(DEPTH, n_pool, GROUP_W, PAGE_SIZE)
    cache_vt = cache_sb_v.transpose(0, 1, 3, 4, 2).reshape(DEPTH, n_pool, GROUP_W, PAGE_SIZE)

    tril_ml = _tri(MLSTM_STEP, "chunk_lower_incl", CHUNK)
    tril_gd = _tri(GDN_STEP, "chunk_lower_incl", GDN_CHUNK)
    tri_sb = _tri(SB_BLOCK, "row_gt_col")
    tri_page = _tri(PAGE_SIZE, "row_gt_col")
    later_pages = _tri(8 * N_PAGES, "later_page")
    lanes_lt = _tri(LANES, "row_lt_col")
    tri_rp = _tri(TM_ROUTE, "lower_strict")
    tri_rd = _tri(DEC_BATCH, "lower_strict")
    cos_p, sin_p = _rope_tables(np.arange(SEQ))
    cos_p = jnp.asarray(np.tile(cos_p, (1, N_HEADS)))
    sin_p = jnp.asarray(np.tile(sin_p, (1, N_HEADS)))
    cos_d, sin_d = _rope_tables([PAST_LEN])
    cos_d = jnp.asarray(np.tile(cos_d.T, (1, LANES)))
    sin_d = jnp.asarray(np.tile(sin_d.T, (1, LANES)))
    dmat, xz, gch, gamma = _retention_consts(CHUNK)
    zeros4 = jnp.zeros((N_HEADS,), F32)

    outs = {k: [] for k in ("kp", "vp", "ks", "vs", "cp", "np", "mp", "cs", "ns", "ms", "gp", "gcp", "gs", "gcs",
                            "rp", "rs")}
    yp = yd = None
    for l in range(DEPTH):
        w_main, wg_hi, wg_lo = _prep_w_in(w_in[l])
        nw = norm_attn_w[l][None, :]
        gate_bias_row = _pad_lanes(jnp.concatenate([mlstm_gate_bias[l], zeros4, gdn_dt_bias[l]])[None, :])
        a_log_row = _pad_lanes(jnp.concatenate([zeros4, zeros4, zeros4, gdn_a_log[l]])[None, :])
        bias = sb_logit_bias[l]

        sbq, sbk, sbv, ml, gd, rt, gt = _inproj_prompt(xp, nw, w_main, wg_hi, wg_lo)
        osb_p = _sb_prompt(sbq, sbk, sbv, bias, tri_sb)
        oml_p, c1p, n1p, m1p = _mlstm_prompt(ml, gt, gate_bias_row, tril_ml)
        ogd_p, s1p, cv1p = _gdn_prompt(gd, gt, gate_bias_row, a_log_row, gdn_conv_w[l], tril_gd)
        ort_p, r1p = _ret_prompt(rt, cos_p, sin_p, dmat, xz, gch)

        sbq_d, gdraw_d, pt = _inproj_decode(xd, nw, w_main, wg_hi, wg_lo)
        osb_d = _sb_decode(sbq_d, cache_kt, cache_vt, page_table, bias, tri_page, later_pages, l)
        qkvt, cv1d = _gdn_decode_conv(gdraw_d, state_gdn_conv[l].transpose(1, 0, 2), gdn_conv_w[l])
        scalars = jnp.stack([mlstm_gate_bias[l][:N_HEADS], mlstm_gate_bias[l][N_HEADS:], gdn_dt_bias[l],
                             jnp.exp(gdn_a_log[l]), jnp.asarray(gamma, F32), zeros4, zeros4, zeros4])
        oml_d, ogd_d, ort_d, c1d, n1d, m1d, s1d, r1d = _decode_rec(
            scalars, pt, qkvt, cos_d, sin_d,
            state_mlstm_c[l].transpose(1, 2, 3, 0), state_mlstm_n[l].transpose(1, 2, 0),
            state_mlstm_m[l].T.reshape(N_HEADS, 1, DEC_BATCH),
            state_gdn_s[l].transpose(1, 2, 3, 0), state_ret_s[l].transpose(1, 2, 3, 0))

        gain = head_norm_w[l][None, :]
        wo = w_out[l].astype(BF16)
        nfw = norm_ffn_w[l][None, :]
        rw_hi, rw_lo = _hi_lo(_pad_lanes(jnp.concatenate([router_group_w[l], router_expert_w[l]], axis=1)))
        rb = _pad_lanes(jnp.concatenate([router_group_b[l], router_expert_b[l]])[None, :])
        x1p, h2p, lgp = _outproj(xp, osb_p, oml_p, ogd_p, ort_p, gain, wo, nfw, rw_hi, rw_lo, False)
        x1d, h2d, lgd = _outproj(xd, osb_d, oml_d, ogd_d, ort_d, gain, wo, nfw, rw_hi, rw_lo, True)
        prp, rfp, tip, cnt_p = _route(lgp, rb, tri_rp, lanes_lt, jnp.zeros((1, LANES), F32))
        prd, rfd, tid, cnt = _route(lgd, rb, tri_rd, lanes_lt, cnt_p)

        offs, te, tile_src = _dispatch_plan(cnt[0, :N_EXPERTS].astype(I32))
        plan_p = (offs,) + _tile_plan(tip)
        plan_d = (offs,) + _tile_plan(tid)
        xs = _dispatch(plan_p, h2p, prp, jnp.zeros((N_SLOTS, D_MODEL), F32))
        xs = _dispatch(plan_d, h2d, prd, xs)
        ys = _experts(te, tile_src, xs, expert_w_gate, expert_w_up, expert_w_down, l)
        final = l == DEPTH - 1
        fw = final_norm_w[None, :]
        xp_next = _combine(plan_p, x1p, rfp, fw, ys, final)
        xd_next = _combine(plan_d, x1d, rfd, fw, ys, final)
        if final:
            yp, yd = xp_next, xd_next
        else:
            xp, xd = xp_next, xd_next

        heads_p = lambda a: a.reshape(BATCH, SEQ, N_HEADS, D)
        heads_t = lambda a: a.reshape(N_HEADS, D, DEC_BATCH).transpose(2, 0, 1)[:, None]
        outs["kp"].append(heads_p(sbk))
        outs["vp"].append(heads_p(sbv))
        outs["ks"].append(heads_t(pt[GROUP_W:2 * GROUP_W]))
        outs["vs"].append(heads_t(pt[2 * GROUP_W:3 * GROUP_W]))
        outs["cp"].append(c1p)
        outs["np"].append(n1p)
        outs["mp"].append(m1p[:, 0, :N_HEADS])
        outs["cs"].append(c1d.transpose(3, 0, 1, 2))
        outs["ns"].append(n1d.transpose(2, 0, 1))
        outs["ms"].append(m1d[:, 0, :].T)
        outs["gp"].append(s1p)
        outs["gcp"].append(cv1p)
        outs["gs"].append(s1d.transpose(3, 0, 1, 2))
        outs["gcs"].append(cv1d.transpose(1, 0, 2))
        outs["rp"].append(r1p)
        outs["rs"].append(r1d.transpose(3, 0, 1, 2))

    st = lambda k: jnp.stack(outs[k], axis=0)
    return (yp.reshape(BATCH, SEQ, D_MODEL), yd.reshape(DEC_BATCH, 1, D_MODEL),
            st("kp"), st("vp"), st("ks"), st("vs"),
            st("cp"), st("np"), st("mp"), st("cs"), st("ns"), st("ms"),
            st("gp"), st("gcp"), st("gs"), st("gcs"), st("rp"), st("rs"))
```

```python
import functools
import math

import numpy as np
import jax
import jax.numpy as jnp
from jax import lax
from jax.experimental import pallas as pl
from jax.experimental.pallas import tpu as pltpu

F32 = jnp.float32
BF16 = jnp.bfloat16
I32 = jnp.int32

D_MODEL = 1024
BATCH = 8
SEQ = 2048
DEPTH = 2
DEC_BATCH = 128
PAST_LEN = 2048
PAGE_SIZE = 128
N_PAGES = PAST_LEN // PAGE_SIZE
HEAD_DIM = 64
N_HEADS = 4
GROUP_W = N_HEADS * HEAD_DIM
CHUNK = 128
GDN_CHUNK = 64
CONV_W = 4
N_GROUPS = 4
EXPERTS_PER_GROUP = 8
N_EXPERTS = N_GROUPS * EXPERTS_PER_GROUP
D_EXPERT = D_MODEL // 2
ROPE_BASE = 10000.0
EPS = 1e-6
SCALE = HEAD_DIM ** -0.5

N_PROMPT = BATCH * SEQ
LANES = 128
N_MAIN = 3 * GROUP_W + 3 * 4 * GROUP_W
N_PROJ = N_MAIN + LANES
COL_ML = 3 * GROUP_W
COL_GD = COL_ML + 4 * GROUP_W
COL_RT = COL_GD + 4 * GROUP_W
COL_GATES = N_MAIN

MLSTM_STEP = 2 * CHUNK
GDN_STEP = 4 * GDN_CHUNK
RET_STEP = 4 * CHUNK
SB_BLOCK = 256
SB_DEC_SEQS = 4
TM_PROMPT = 512
TM_OUTPROJ = 256
TM_EXPERT = 512
N_ASSIGN = 2 * (N_PROMPT + DEC_BATCH)
TM_ROUTE = 512
RUN_ALIGN = 8
RUN_CHUNK = 4 * RUN_ALIGN
N_TOKEN_TILES = N_PROMPT // TM_ROUTE + 1
N_TILES = -(-(N_ASSIGN + N_TOKEN_TILES * N_EXPERTS * (RUN_ALIGN - 1)) // TM_EXPERT) + N_EXPERTS
N_SLOTS = N_TILES * TM_EXPERT
VMEM_LIMIT = 48 * 1024 * 1024

_NT = (((1,), (1,)), ((), ()))
_TN = (((0,), (0,)), ((), ()))


_NN = (((1,), (0,)), ((), ()))


def _mm(a, b, dims):
    return lax.dot_general(a.astype(BF16), b.astype(BF16), dims, preferred_element_type=F32)


def _dot(a, b):
    return _mm(a, b, _NN)


def _dot_nt(a, b):
    return _mm(a, b, _NT)


def _dot_tn(a, b):
    return _mm(a, b, _TN)


def _split3(x):
    x1 = x.astype(BF16)
    r1 = x - x1.astype(F32)
    x2 = r1.astype(BF16)
    x3 = (r1 - x2.astype(F32)).astype(BF16)
    return x1, x2, x3


def _tri_dot(tri, x):
    x1, x2, x3 = _split3(x)
    return _dot(tri, x1) + _dot(tri, x2) + _dot(tri, x3)


def _log_sigmoid(z):
    return jnp.minimum(z, 0.0) - jnp.log(1.0 + jnp.exp(-jnp.abs(z)))


def _softplus(z):
    return jnp.maximum(z, 0.0) + jnp.log(1.0 + jnp.exp(-jnp.abs(z)))


def _sigmoid(z):
    return 1.0 / (1.0 + jnp.exp(-z))


def _head_rms(x):
    return x * lax.rsqrt(jnp.mean(x * x, axis=-1, keepdims=True) + EPS)


def _cparams(sem):
    return pltpu.CompilerParams(dimension_semantics=sem, vmem_limit_bytes=VMEM_LIMIT)


def _const_spec(shape):
    nd = len(shape)
    return pl.BlockSpec(shape, lambda *_: (0,) * nd)


def _norm_rows(x, w):
    return x * lax.rsqrt(jnp.mean(x * x, axis=-1, keepdims=True) + EPS) * w


def _gates_dot(h, wgh_ref, wgl_ref):
    hb = h.astype(BF16)
    hl = (h - hb.astype(F32)).astype(BF16)
    return _dot(hb, wgh_ref[...]) + _dot(hl, wgh_ref[...]) + _dot(hb, wgl_ref[...])


def _inproj_prompt_kernel(x_ref, nw_ref, w_ref, wgh_ref, wgl_ref,
                          sbq_ref, sbk_ref, sbv_ref, ml_ref, gd_ref, rt_ref, gt_ref):
    h = _norm_rows(x_ref[...], nw_ref[...])
    hb = h.astype(BF16)
    seg = lambda a, b: _dot(hb, w_ref[:, a:b])
    sbq_ref[...] = seg(0, GROUP_W)
    sbk_ref[...] = seg(GROUP_W, 2 * GROUP_W)
    sbv_ref[...] = seg(2 * GROUP_W, 3 * GROUP_W)
    ml_ref[...] = seg(COL_ML, COL_GD)
    gd_ref[...] = seg(COL_GD, COL_RT)
    rt_ref[...] = seg(COL_RT, N_MAIN)
    gt_ref[...] = _gates_dot(h, wgh_ref, wgl_ref)


def _inproj_prompt(x, nw, w, wgh, wgl):
    n = x.shape[0]
    tm = TM_PROMPT
    row = lambda c: pl.BlockSpec((tm, c), lambda i: (i, 0))
    S = jax.ShapeDtypeStruct
    return pl.pallas_call(
        _inproj_prompt_kernel,
        grid=(n // tm,),
        in_specs=[row(D_MODEL), _const_spec((1, D_MODEL)), _const_spec((D_MODEL, N_MAIN)),
                  _const_spec((D_MODEL, LANES)), _const_spec((D_MODEL, LANES))],
        out_specs=[row(GROUP_W), row(GROUP_W), row(GROUP_W), row(4 * GROUP_W), row(4 * GROUP_W),
                   row(4 * GROUP_W), row(LANES)],
        out_shape=[S((n, GROUP_W), F32)] * 3 + [S((n, 4 * GROUP_W), F32)] * 3 + [S((n, LANES), F32)],
        compiler_params=_cparams(("parallel",)),
        name="inproj_prompt",
    )(x, nw, w, wgh, wgl)


def _inproj_decode_kernel(x_ref, nw_ref, w_ref, wgh_ref, wgl_ref, sbq_ref, gdraw_ref, pt_ref):
    h = _norm_rows(x_ref[...], nw_ref[...])
    hb = h.astype(BF16)
    for j in range(N_MAIN // LANES):
        p = _dot(hb, w_ref[:, j * LANES:(j + 1) * LANES])
        if j < GROUP_W // LANES:
            sbq_ref[:, j * LANES:(j + 1) * LANES] = p
        c0 = j * LANES - COL_GD
        if 0 <= c0 < 3 * GROUP_W:
            gdraw_ref[:, c0:c0 + LANES] = p
        pt_ref[j * LANES:(j + 1) * LANES, :] = p.T
    pt_ref[N_MAIN:N_PROJ, :] = _gates_dot(h, wgh_ref, wgl_ref).T


def _inproj_decode(x, nw, w, wgh, wgl):
    S = jax.ShapeDtypeStruct
    return pl.pallas_call(
        _inproj_decode_kernel,
        out_shape=[S((DEC_BATCH, GROUP_W), F32), S((DEC_BATCH, 3 * GROUP_W), F32), S((N_PROJ, DEC_BATCH), F32)],
        compiler_params=_cparams(None),
        name="inproj_decode",
    )(x, nw, w, wgh, wgl)


def _sb_prompt_kernel(bias_ref, q_ref, k_ref, v_ref, tri_ref, o_ref, acc_s, carry_s):
    tb = SB_BLOCK
    qi = pl.program_id(1)
    tri = tri_ref[...]
    r = lax.broadcasted_iota(I32, (tb, tb), 0)
    c = lax.broadcasted_iota(I32, (tb, tb), 1)
    dmask = c < r
    acc_s[...] = jnp.zeros_like(acc_s)
    carry_s[...] = jnp.zeros_like(carry_s)
    q = q_ref[...] * SCALE

    def block(j, mask):
        start = pl.multiple_of(j * tb, tb)
        kj = k_ref[pl.ds(start, tb), :]
        vj = v_ref[pl.ds(start, tb), :]
        heads = range(N_HEADS)
        sls = [slice(h * HEAD_DIM, (h + 1) * HEAD_DIM) for h in heads]
        z = [_dot_nt(q[:, sls[h]], kj[:, sls[h]]) + bias_ref[h] for h in heads]
        lsz = [_log_sigmoid(z[h]) for h in heads]
        lk = [lsz[h] - z[h] for h in heads]
        if mask is not None:
            lk = [jnp.where(mask, x, 0.0) for x in lk]
        lw_all = _dot(jnp.concatenate([x.astype(BF16) for x in lk], axis=0), tri)
        lw = [lw_all[h * tb:(h + 1) * tb, :] for h in heads]
        a = [jnp.exp(lsz[h] + lw[h] + carry_s[:, h:h + 1]) for h in heads]
        if mask is not None:
            a = [jnp.where(mask, x, 0.0) for x in a]
        av = [_dot(a[h], vj[:, sls[h]]) for h in heads]
        for h in heads:
            acc_s[:, sls[h]] += av[h]
            carry_s[:, h:h + 1] += lw[h][:, 0:1] + lk[h][:, 0:1]

    block(qi, dmask)

    def body(it, _):
        block(qi - 1 - it, None)
        return 0

    lax.fori_loop(0, qi, body, 0)
    for h in range(N_HEADS):
        sl = slice(h * HEAD_DIM, (h + 1) * HEAD_DIM)
        o_ref[:, sl] = _head_rms(acc_s[:, sl])


def _sb_prompt(q, k, v, bias, tri):
    tb = SB_BLOCK
    nq = SEQ // tb
    return pl.pallas_call(
        _sb_prompt_kernel,
        grid_spec=pltpu.PrefetchScalarGridSpec(
            num_scalar_prefetch=0,
            grid=(BATCH, nq),
            in_specs=[pl.BlockSpec(memory_space=pltpu.SMEM),
                      pl.BlockSpec((tb, GROUP_W), lambda b, i: (b * nq + i, 0)),
                      pl.BlockSpec((SEQ, GROUP_W), lambda b, i: (b, 0)),
                      pl.BlockSpec((SEQ, GROUP_W), lambda b, i: (b, 0)),
                      _const_spec((tb, tb))],
            out_specs=pl.BlockSpec((tb, GROUP_W), lambda b, i: (b * nq + i, 0)),
            scratch_shapes=[pltpu.VMEM((tb, GROUP_W), F32), pltpu.VMEM((tb, LANES), F32)]),
        out_shape=jax.ShapeDtypeStruct((N_PROMPT, GROUP_W), F32),
        compiler_params=_cparams(("parallel", "parallel")),
        name="sb_prompt",
    )(bias, q, k, v, tri)


def _sb_decode_kernel(pt_ref, bias_ref, q_ref, *rest):
    S = SB_DEC_SEQS
    n_in = S * N_PAGES
    k_refs = rest[:n_in]
    v_refs = rest[n_in:2 * n_in]
    tri_ref, pg_ref, o_ref = rest[2 * n_in:]
    n_rows = 8 * N_PAGES
    r0 = (pl.program_id(0) * S) % 8
    seqs = range(S)
    rowi = lax.broadcasted_iota(I32, (8, GROUP_W), 0)
    lanei = lax.broadcasted_iota(I32, (8, GROUP_W), 1)
    head_of_lane = jnp.right_shift(lanei, 6)
    row8 = lax.broadcasted_iota(I32, (8, 1), 0)
    bias = jnp.zeros((8, 1), F32)
    for h in range(N_HEADS):
        bias = jnp.where(row8 == h, bias_ref[h], bias)
    qbd = [jnp.where(head_of_lane == rowi, q_ref[pl.ds(r0 + s, 1), :] * SCALE, 0.0).astype(BF16) for s in seqs]
    z = [jnp.concatenate([_dot(qbd[s], k_refs[s * N_PAGES + j][0, 0]) + bias for j in range(N_PAGES)], axis=0)
         for s in seqs]
    lsz = [_log_sigmoid(z[s]) for s in seqs]
    lk = [lsz[s] - z[s] for s in seqs]
    lw = [_dot(lk[s], tri_ref[...]) for s in seqs]
    tot = [jnp.broadcast_to(lw[s][:, 0:1] + lk[s][:, 0:1], (n_rows, LANES)) for s in seqs]
    later_pages = [_tri_dot(pg_ref[...], tot[s]) for s in seqs]
    a = [jnp.exp(lsz[s] + lw[s] + later_pages[s]) for s in seqs]
    for s in seqs:
        acc = jnp.zeros((8, GROUP_W), F32)
        for j in range(N_PAGES):
            acc = acc + _dot_nt(a[s][j * 8:(j + 1) * 8, :], v_refs[s * N_PAGES + j][0, 0])
        orow = jnp.sum(jnp.where(head_of_lane == rowi, acc, 0.0), axis=0, keepdims=True)
        pieces = [_head_rms(orow[:, h * HEAD_DIM:(h + 1) * HEAD_DIM]) for h in range(N_HEADS)]
        o_ref[pl.ds(r0 + s, 1), :] = jnp.concatenate(pieces, axis=1)


def _sb_decode(q, cache_kt, cache_vt, page_table, bias, tri, later_pages, layer):
    S = SB_DEC_SEQS

    def page_spec(s, j):
        return pl.BlockSpec((1, 1, GROUP_W, PAGE_SIZE), lambda i, pt: (layer, pt[i * S + s, j], 0, 0))

    pages = [page_spec(s, j) for s in range(S) for j in range(N_PAGES)]
    in_specs = ([pl.BlockSpec(memory_space=pltpu.SMEM),
                 pl.BlockSpec((8, GROUP_W), lambda i, pt: (i * S // 8, 0))]
                + pages * 2
                + [pl.BlockSpec((PAGE_SIZE, PAGE_SIZE), lambda i, pt: (0, 0)),
                   pl.BlockSpec((8 * N_PAGES, 8 * N_PAGES), lambda i, pt: (0, 0))])
    return pl.pallas_call(
        _sb_decode_kernel,
        grid_spec=pltpu.PrefetchScalarGridSpec(
            num_scalar_prefetch=1,
            grid=(DEC_BATCH // S,),
            in_specs=in_specs,
            out_specs=pl.BlockSpec((8, GROUP_W), lambda i, pt: (i * S // 8, 0))),
        out_shape=jax.ShapeDtypeStruct((DEC_BATCH, GROUP_W), F32),
        compiler_params=_cparams(("arbitrary",)),
        name="sb_decode",
    )(page_table, bias, q, *([cache_kt] * (S * N_PAGES)), *([cache_vt] * (S * N_PAGES)), tri, later_pages)


def _chunk_masks(L):
    r = lax.broadcasted_iota(I32, (L, L), 0)
    c = lax.broadcasted_iota(I32, (L, L), 1)
    return c <= r, c < r


def _head_cols(blk, h):
    return slice(blk * GROUP_W + h * HEAD_DIM, blk * GROUP_W + (h + 1) * HEAD_DIM)


def _mlstm_prompt_kernel(x_ref, g_ref, gb_ref, tril_ref, o_ref, c1_ref, n1_ref, m1_ref, c_s, n_s, m_s):
    L = CHUNK
    R = x_ref.shape[0]
    ci = pl.program_id(1)

    @pl.when(ci == 0)
    def _():
        c_s[...] = jnp.zeros_like(c_s)
        n_s[...] = jnp.zeros_like(n_s)
        m_s[...] = jnp.zeros_like(m_s)

    g = g_ref[...] + gb_ref[...]
    lane = lax.broadcasted_iota(I32, (R, LANES), 1)
    lf = jnp.where((lane >= N_HEADS) & (lane < 2 * N_HEADS), _log_sigmoid(g), 0.0)
    cum = _tri_dot(tril_ref[...], lf)
    xt = jnp.where(lane < N_HEADS, g, cum).T
    causal, _ = _chunk_masks(L)
    P = [(s_i, h) for s_i in range(R // L) for h in range(N_HEADS)]
    rows = [slice(s_i * L, (s_i + 1) * L) for s_i, _ in P]
    hs = [h for _, h in P]
    np_ = range(len(P))
    bc = [cum[rows[p], N_HEADS + hs[p]:N_HEADS + hs[p] + 1] for p in np_]
    li = [g[rows[p], hs[p]:hs[p] + 1] for p in np_]
    log_d = [jnp.where(causal, bc[p] - xt[N_HEADS + hs[p]:N_HEADS + hs[p] + 1, rows[p]] + xt[hs[p]:hs[p] + 1, rows[p]],
                       -jnp.inf) for p in np_]
    b_last = [bc[p][L - 1:L, :] for p in np_]
    log_w = [b_last[p] - bc[p] + li[p] for p in np_]
    d_max = [jnp.max(log_d[p], axis=1, keepdims=True) for p in np_]
    w_max = [jnp.max(log_w[p], axis=0, keepdims=True) for p in np_]
    m = [m_s[0:1, h:h + 1] for h in range(N_HEADS)]
    m_in, m_out = [], []
    for p in np_:
        m_in.append(m[hs[p]])
        m[hs[p]] = jnp.maximum(b_last[p] + m[hs[p]], w_max[p])
        m_out.append(m[hs[p]])
    log_inter = [bc[p] + m_in[p] for p in np_]
    m_row = [jnp.maximum(log_inter[p], d_max[p]) for p in np_]
    dexp = [jnp.exp(log_d[p] - m_row[p]) for p in np_]
    w_inter = [jnp.exp(log_inter[p] - m_row[p]) for p in np_]
    decay = [jnp.exp(b_last[p] + m_in[p] - m_out[p]) for p in np_]
    q = [x_ref[rows[p], _head_cols(0, hs[p])] for p in np_]
    k = [x_ref[rows[p], _head_cols(1, hs[p])] * SCALE for p in np_]
    v = [x_ref[rows[p], _head_cols(2, hs[p])] for p in np_]
    kw = [k[p] * jnp.exp(log_w[p] - m_out[p]) for p in np_]
    s_mat = [_dot_nt(q[p], k[p]) * dexp[p] for p in np_]
    sv = [_dot(s_mat[p], v[p]) for p in np_]
    kv = [_dot_tn(kw[p], v[p]) for p in np_]
    kw_sum = [jnp.sum(kw[p], axis=0, keepdims=True) for p in np_]
    c = [c_s[h] for h in range(N_HEADS)]
    n = [n_s[h:h + 1, :] for h in range(N_HEADS)]
    c_in, n_in = [], []
    for p in np_:
        c_in.append(c[hs[p]])
        n_in.append(n[hs[p]])
        c[hs[p]] = decay[p] * c[hs[p]] + kv[p]
        n[hs[p]] = decay[p] * n[hs[p]] + kw_sum[p]
    qc = [_dot(q[p], c_in[p]) for p in np_]
    s_sum = [jnp.sum(s_mat[p], axis=1, keepdims=True) for p in np_]
    qn = [jnp.sum(q[p] * n_in[p], axis=1, keepdims=True) for p in np_]
    hh = [(sv[p] + w_inter[p] * qc[p]) / jnp.maximum(jnp.abs(s_sum[p] + w_inter[p] * qn[p]), jnp.exp(-m_row[p]))
          for p in np_]
    ms = [jnp.mean(hh[p] * hh[p], axis=1, keepdims=True) for p in np_]
    for p in np_:
        og = x_ref[rows[p], _head_cols(3, hs[p])]
        o_ref[rows[p], hs[p] * HEAD_DIM:(hs[p] + 1) * HEAD_DIM] = hh[p] * lax.rsqrt(ms[p] + EPS) * _sigmoid(og)
    for h in range(N_HEADS):
        c_s[h] = c[h]
        n_s[h:h + 1, :] = n[h]
        m_s[0:1, h:h + 1] = m[h]

    @pl.when(ci == pl.num_programs(1) - 1)
    def _():
        c1_ref[0] = c_s[...]
        n1_ref[0] = n_s[...]
        m1_ref[0] = m_s[...]


def _mlstm_prompt(ml, gates, gate_bias_row, tril):
    L = MLSTM_STEP
    nc = SEQ // L
    S = jax.ShapeDtypeStruct
    return pl.pallas_call(
        _mlstm_prompt_kernel,
        grid=(BATCH, nc),
        in_specs=[pl.BlockSpec((L, 4 * GROUP_W), lambda b, c: (b * nc + c, 0)),
                  pl.BlockSpec((L, LANES), lambda b, c: (b * nc + c, 0)),
                  _const_spec((1, LANES)), _const_spec((L, L))],
        out_specs=[pl.BlockSpec((L, GROUP_W), lambda b, c: (b * nc + c, 0)),
                   pl.BlockSpec((1, N_HEADS, HEAD_DIM, HEAD_DIM), lambda b, c: (b, 0, 0, 0)),
                   pl.BlockSpec((1, N_HEADS, HEAD_DIM), lambda b, c: (b, 0, 0)),
                   pl.BlockSpec((1, 1, LANES), lambda b, c: (b, 0, 0))],
        out_shape=[S((N_PROMPT, GROUP_W), F32), S((BATCH, N_HEADS, HEAD_DIM, HEAD_DIM), F32),
                   S((BATCH, N_HEADS, HEAD_DIM), F32), S((BATCH, 1, LANES), F32)],
        scratch_shapes=[pltpu.VMEM((N_HEADS, HEAD_DIM, HEAD_DIM), F32), pltpu.VMEM((N_HEADS, HEAD_DIM), F32),
                        pltpu.VMEM((1, LANES), F32)],
        compiler_params=_cparams(("parallel", "arbitrary")),
        name="mlstm_prompt",
    )(ml, gates, gate_bias_row, tril)


def _unit_lower_inverse(a, L):
    r = lax.broadcasted_iota(I32, (L, L), 0)
    c = lax.broadcasted_iota(I32, (L, L), 1)
    p = jnp.where(r == c, 1.0, 0.0) - a
    x = a
    power = 1
    while 2 * power < L:
        x = _dot(x, x)
        p = p + _dot(p, x)
        power *= 2
    return p


def _gdn_prompt_kernel(x_ref, g_ref, gb_ref, al_ref, cw_ref, tril_ref, o_ref, s1_ref, cv_ref, s_s, xe_s):
    L = GDN_CHUNK
    R = x_ref.shape[0]
    nq = 3 * GROUP_W
    ci = pl.program_id(1)

    @pl.when(ci == 0)
    def _():
        s_s[...] = jnp.zeros_like(s_s)
        xe_s[0:8, :] = jnp.zeros((8, nq), F32)

    raw = x_ref[:, 0:nq]
    xe_s[8:8 + R, :] = raw
    conv = (cw_ref[3:4, :] * raw + cw_ref[2:3, :] * xe_s[7:7 + R, :]
            + cw_ref[1:2, :] * xe_s[6:6 + R, :] + cw_ref[0:1, :] * xe_s[5:5 + R, :])
    xe_s[0:8, :] = raw[R - 8:R, :]
    qkv = conv * _sigmoid(conv)

    g = g_ref[...] + gb_ref[...]
    lane = lax.broadcasted_iota(I32, (R, LANES), 1)
    beta_all = _sigmoid(g)
    gd = jnp.where((lane >= 3 * N_HEADS) & (lane < 4 * N_HEADS), -jnp.exp(al_ref[...]) * _softplus(g), 0.0)
    gcum = _tri_dot(tril_ref[...], gd)
    xt = gcum.T
    incl, strict = _chunk_masks(L)
    prob = [(s_i, h) for s_i in range(R // L) for h in range(N_HEADS)]
    loc = []
    for s_i, h in prob:
        rows = slice(s_i * L, (s_i + 1) * L)
        q = qkv[rows, _head_cols(0, h)]
        k = qkv[rows, _head_cols(1, h)]
        v = qkv[rows, _head_cols(2, h)]
        q = q * lax.rsqrt(jnp.sum(q * q, axis=1, keepdims=True) + EPS) * SCALE
        k = k * lax.rsqrt(jnp.sum(k * k, axis=1, keepdims=True) + EPS)
        b = beta_all[rows, 2 * N_HEADS + h:2 * N_HEADS + h + 1]
        gc = gcum[rows, 3 * N_HEADS + h:3 * N_HEADS + h + 1]
        gc_row = xt[3 * N_HEADS + h:3 * N_HEADS + h + 1, rows]
        decay = jnp.exp(jnp.where(incl, gc - gc_row, -jnp.inf))
        kb = k * b
        egc = jnp.exp(gc)
        gl = gc[L - 1:L, :]
        loc.append(dict(q=q, k=k, kb=kb, decay=decay, vb=v * b, kbe=kb * egc, q_dec=q * egc,
                        k_dec=k * jnp.exp(gl - gc), g_last=jnp.exp(gl)))
    a_mat = [jnp.where(strict, _dot_nt(d["kb"], d["k"]) * d["decay"], 0.0) for d in loc]
    qk_mat = [_dot_nt(d["q"], d["k"]) * d["decay"] for d in loc]
    eye = jnp.where(incl & jnp.logical_not(strict), 1.0, 0.0)
    pw = a_mat
    inv = [eye - a for a in a_mat]
    power = 1
    while 2 * power < L:
        pw = [_dot(x, x) for x in pw]
        inv = [p + _dot(p, x) for p, x in zip(inv, pw)]
        power *= 2
    us = [_dot(t, d["vb"]) for t, d in zip(inv, loc)]
    ws = [_dot(t, d["kbe"]) for t, d in zip(inv, loc)]
    s = [s_s[h] for h in range(N_HEADS)]
    heads = range(N_HEADS)
    for s_i in range(R // L):
        rows = slice(s_i * L, (s_i + 1) * L)
        p0 = s_i * N_HEADS
        w_s = [_dot(ws[p0 + h], s[h]) for h in heads]
        q_s = [_dot(loc[p0 + h]["q_dec"], s[h]) for h in heads]
        v_new = [us[p0 + h] - w_s[h] for h in heads]
        o = [q_s[h] + _dot(qk_mat[p0 + h], v_new[h]) for h in heads]
        s = [s[h] * loc[p0 + h]["g_last"] + _dot_tn(loc[p0 + h]["k_dec"], v_new[h]) for h in heads]
        for h in heads:
            gate = x_ref[rows, _head_cols(3, h)]
            o_ref[rows, h * HEAD_DIM:(h + 1) * HEAD_DIM] = _head_rms(o[h]) * (gate * _sigmoid(gate))
    for h in range(N_HEADS):
        s_s[h] = s[h]

    @pl.when(ci == pl.num_programs(1) - 1)
    def _():
        s1_ref[0] = s_s[...]
        cv_ref[0] = xe_s[8 - (CONV_W - 1):8, :]


def _gdn_prompt(gd, gates, gate_bias_row, a_log_row, conv_w, tril):
    L = GDN_STEP
    nc = SEQ // L
    S = jax.ShapeDtypeStruct
    return pl.pallas_call(
        _gdn_prompt_kernel,
        grid=(BATCH, nc),
        in_specs=[pl.BlockSpec((L, 4 * GROUP_W), lambda b, c: (b * nc + c, 0)),
                  pl.BlockSpec((L, LANES), lambda b, c: (b * nc + c, 0)),
                  _const_spec((1, LANES)), _const_spec((1, LANES)), _const_spec((CONV_W, 3 * GROUP_W)),
                  _const_spec((L, L))],
        out_specs=[pl.BlockSpec((L, GROUP_W), lambda b, c: (b * nc + c, 0)),
                   pl.BlockSpec((1, N_HEADS, HEAD_DIM, HEAD_DIM), lambda b, c: (b, 0, 0, 0)),
                   pl.BlockSpec((1, CONV_W - 1, 3 * GROUP_W), lambda b, c: (b, 0, 0))],
        out_shape=[S((N_PROMPT, GROUP_W), F32), S((BATCH, N_HEADS, HEAD_DIM, HEAD_DIM), F32),
                   S((BATCH, CONV_W - 1, 3 * GROUP_W), F32)],
        scratch_shapes=[pltpu.VMEM((N_HEADS, HEAD_DIM, HEAD_DIM), F32), pltpu.VMEM((8 + L, 3 * GROUP_W), F32)],
        compiler_params=_cparams(("parallel", "arbitrary")),
        name="gdn_prompt",
    )(gd, gates, gate_bias_row, a_log_row, conv_w, tril)


def _rope_rows(x, cos, sin_signed):
    lane = lax.broadcasted_iota(I32, x.shape, 1)
    first = jnp.bitwise_and(lane, HEAD_DIM - 1) < HEAD_DIM // 2
    w = x.shape[1]
    swapped = jnp.where(first, pltpu.roll(x, w - HEAD_DIM // 2, 1), pltpu.roll(x, HEAD_DIM // 2, 1))
    return x * cos + swapped * sin_signed


def _ret_prompt_kernel(x_ref, cos_ref, sin_ref, dm_ref, xz_ref, gch_ref, o_ref, s1_ref, s_s):
    ci = pl.program_id(1)

    @pl.when(ci == 0)
    def _():
        s_s[...] = jnp.zeros_like(s_s)

    L = CHUNK
    cos = cos_ref[...]
    sin = sin_ref[...]
    qr = _rope_rows(x_ref[:, 0:GROUP_W], cos, sin)
    kr = _rope_rows(x_ref[:, GROUP_W:2 * GROUP_W], cos, sin) * SCALE
    prob = []
    for s_i in range(x_ref.shape[0] // L):
        rows = slice(s_i * L, (s_i + 1) * L)
        for h in range(N_HEADS):
            sl = slice(h * HEAD_DIM, (h + 1) * HEAD_DIM)
            prob.append(dict(rows=rows, h=h, sl=sl, q=qr[rows, sl], k=kr[rows, sl], v=x_ref[rows, _head_cols(2, h)]))
    qk = [_dot_nt(p["q"], p["k"]) * dm_ref[p["h"]] for p in prob]
    intra = [_dot(a, p["v"]) for a, p in zip(qk, prob)]
    kv = [_dot_tn(p["k"] * xz_ref[:, N_HEADS + p["h"]:N_HEADS + p["h"] + 1], p["v"]) for p in prob]
    s = [s_s[h] for h in range(N_HEADS)]
    s_in = []
    for p, kv_p in zip(prob, kv):
        s_in.append(s[p["h"]])
        s[p["h"]] = s[p["h"]] * gch_ref[p["h"]] + kv_p
    inter = [_dot(p["q"], s0) * xz_ref[:, p["h"]:p["h"] + 1] for p, s0 in zip(prob, s_in)]
    for p, a, b in zip(prob, intra, inter):
        gate = x_ref[p["rows"], _head_cols(3, p["h"])]
        o_ref[p["rows"], p["sl"]] = _head_rms(a + b) * (gate * _sigmoid(gate))
    for h in range(N_HEADS):
        s_s[h] = s[h]

    @pl.when(ci == pl.num_programs(1) - 1)
    def _():
        s1_ref[0] = s_s[...]


def _ret_prompt(rt, cos, sin, dmat, xz, gch):
    L = RET_STEP
    nc = SEQ // L
    S = jax.ShapeDtypeStruct
    return pl.pallas_call(
        _ret_prompt_kernel,
        grid=(BATCH, nc),
        in_specs=[pl.BlockSpec((L, 4 * GROUP_W), lambda b, c: (b * nc + c, 0)),
                  pl.BlockSpec((L, GROUP_W), lambda b, c: (c, 0)),
                  pl.BlockSpec((L, GROUP_W), lambda b, c: (c, 0)),
                  _const_spec((N_HEADS, CHUNK, CHUNK)), _const_spec((CHUNK, LANES)),
                  pl.BlockSpec(memory_space=pltpu.SMEM)],
        out_specs=[pl.BlockSpec((L, GROUP_W), lambda b, c: (b * nc + c, 0)),
                   pl.BlockSpec((1, N_HEADS, HEAD_DIM, HEAD_DIM), lambda b, c: (b, 0, 0, 0))],
        out_shape=[S((N_PROMPT, GROUP_W), F32), S((BATCH, N_HEADS, HEAD_DIM, HEAD_DIM), F32)],
        scratch_shapes=[pltpu.VMEM((N_HEADS, HEAD_DIM, HEAD_DIM), F32)],
        compiler_params=_cparams(("parallel", "arbitrary")),
        name="ret_prompt",
    )(rt, cos, sin, dmat, xz, gch)


def _gdn_decode_conv_kernel(raw_ref, c0_ref, cw_ref, qkvt_ref, cv_ref):
    raw = raw_ref[...]
    conv = (cw_ref[3:4, :] * raw + cw_ref[2:3, :] * c0_ref[2] + cw_ref[1:2, :] * c0_ref[1]
            + cw_ref[0:1, :] * c0_ref[0])
    qkv = conv * _sigmoid(conv)
    cv_ref[0] = c0_ref[1]
    cv_ref[1] = c0_ref[2]
    cv_ref[2] = raw
    for blk in range(3):
        for h in range(N_HEADS):
            lo = blk * GROUP_W + h * HEAD_DIM
            x = qkv[:, lo:lo + HEAD_DIM]
            if blk == 0:
                x = x * lax.rsqrt(jnp.sum(x * x, axis=1, keepdims=True) + EPS) * SCALE
            elif blk == 1:
                x = x * lax.rsqrt(jnp.sum(x * x, axis=1, keepdims=True) + EPS)
            if h % 2 == 0:
                pair = x
            else:
                qkvt_ref[lo - HEAD_DIM:lo + HEAD_DIM, :] = jnp.concatenate([pair, x], axis=1).T


def _gdn_decode_conv(raw, conv0, conv_w):
    S = jax.ShapeDtypeStruct
    return pl.pallas_call(
        _gdn_decode_conv_kernel,
        out_shape=[S((3 * GROUP_W, DEC_BATCH), F32), S((CONV_W - 1, DEC_BATCH, 3 * GROUP_W), F32)],
        compiler_params=_cparams(None),
        name="gdn_decode_conv",
    )(raw, conv0, conv_w)


def _decode_rec_kernel(sc_ref, mlq_ref, mlk_ref, mlv_ref, mlo_ref, gq_ref, gk_ref, gv_ref, gg_ref,
                       rq_ref, rk_ref, rv_ref, rg_ref, gt_ref, cos_ref, sin_ref,
                       c0_ref, n0_ref, m0_ref, sg0_ref, sr0_ref,
                       oml_ref, ogd_ref, ort_ref, c1_ref, n1_ref, m1_ref, sg1_ref, sr1_ref, va_s, vb_s):
    h = pl.program_id(0)
    D = HEAD_DIM
    sum0 = lambda x: jnp.sum(x, axis=0, keepdims=True)
    rms0 = lambda x: x * lax.rsqrt(sum0(x * x) * (1.0 / D) + EPS)

    li = gt_ref[pl.ds(h, 1), :] + sc_ref[0, h]
    lf = _log_sigmoid(gt_ref[pl.ds(N_HEADS + h, 1), :] + sc_ref[1, h])
    m0 = m0_ref[0]
    q = mlq_ref[...]
    k = mlk_ref[...] * SCALE
    v = mlv_ref[...]
    log_inter = lf + m0
    m_row = jnp.maximum(log_inter, li)
    s = sum0(q * k) * jnp.exp(li - m_row)
    w_inter = jnp.exp(log_inter - m_row)
    decay = jnp.exp(lf + m0 - m_row)
    kw = k * jnp.exp(li - m_row)

    va_s[...] = kw

    def ml_body(d, qc):
        c_d = c0_ref[0, d]
        c1_ref[0, d] = decay * c_d + va_s[pl.ds(d, 1), :] * v
        return qc + mlq_ref[pl.ds(d, 1), :] * c_d

    n0 = n0_ref[0]
    n1_ref[0] = decay * n0 + kw
    qc = lax.fori_loop(0, D, ml_body, jnp.zeros((D, LANES), F32))
    num = s * v + w_inter * qc
    den = s + w_inter * sum0(q * n0)
    hh = num / jnp.maximum(jnp.abs(den), jnp.exp(-m_row))
    m1_ref[0] = m_row
    oml_ref[...] = rms0(hh) * _sigmoid(mlo_ref[...])

    beta = _sigmoid(gt_ref[pl.ds(2 * N_HEADS + h, 1), :])
    gdec = -sc_ref[3, h] * _softplus(gt_ref[pl.ds(3 * N_HEADS + h, 1), :] + sc_ref[2, h])
    eg = jnp.exp(gdec)
    gv = gv_ref[...]

    def ks_body(d, acc):
        return acc + gk_ref[pl.ds(d, 1), :] * sg0_ref[0, d]

    ks = lax.fori_loop(0, D, ks_body, jnp.zeros((D, LANES), F32))
    v_new = beta * gv - (beta * eg) * ks

    def gd_body(d, acc):
        s_new = eg * sg0_ref[0, d] + gk_ref[pl.ds(d, 1), :] * v_new
        sg1_ref[0, d] = s_new
        return acc + gq_ref[pl.ds(d, 1), :] * s_new

    og = lax.fori_loop(0, D, gd_body, jnp.zeros((D, LANES), F32))
    gate = gg_ref[...]
    ogd_ref[...] = rms0(og) * (gate * _sigmoid(gate))

    half = D // 2

    def rope(ref):
        x = ref[...]
        sw = jnp.concatenate([x[half:, :], x[:half, :]], axis=0)
        return x * cos_ref[...] + sw * sin_ref[...]

    va_s[...] = rope(rq_ref)
    vb_s[...] = rope(rk_ref) * SCALE
    rv = rv_ref[...]
    gamma = sc_ref[4, h]

    def rt_body(d, acc):
        s_new = gamma * sr0_ref[0, d] + vb_s[pl.ds(d, 1), :] * rv
        sr1_ref[0, d] = s_new
        return acc + va_s[pl.ds(d, 1), :] * s_new

    ort = lax.fori_loop(0, D, rt_body, jnp.zeros((D, LANES), F32))
    gate = rg_ref[...]
    ort_ref[...] = rms0(ort) * (gate * _sigmoid(gate))


def _decode_rec(scalars, pt, qkvt, cos_t, sin_t, c0, n0, m0, sg0, sr0):
    D = HEAD_DIM
    S = jax.ShapeDtypeStruct
    prow = lambda col, blk: pl.BlockSpec((D, LANES), lambda h: ((col + blk * GROUP_W) // D + h, 0))
    vec = lambda blk: pl.BlockSpec((D, LANES), lambda h: (blk * N_HEADS + h, 0))
    st4 = pl.BlockSpec((1, D, D, LANES), lambda h: (h, 0, 0, 0))
    st3 = pl.BlockSpec((1, D, LANES), lambda h: (h, 0, 0))
    st2 = pl.BlockSpec((1, 1, LANES), lambda h: (h, 0, 0))
    in_specs = ([pl.BlockSpec(memory_space=pltpu.SMEM)]
                + [prow(COL_ML, blk) for blk in range(4)]
                + [vec(0), vec(1), vec(2), prow(COL_GD, 3)]
                + [prow(COL_RT, blk) for blk in range(4)]
                + [pl.BlockSpec((LANES, LANES), lambda h: (COL_GATES // LANES, 0)),
                   _const_spec((D, LANES)), _const_spec((D, LANES)), st4, st3, st2, st4, st4])
    return pl.pallas_call(
        _decode_rec_kernel,
        grid=(N_HEADS,),
        in_specs=in_specs,
        out_specs=[vec(0), vec(0), vec(0), st4, st3, st2, st4, st4],
        out_shape=[S((GROUP_W, LANES), F32)] * 3
        + [S((N_HEADS, D, D, LANES), F32), S((N_HEADS, D, LANES), F32), S((N_HEADS, 1, LANES), F32),
           S((N_HEADS, D, D, LANES), F32), S((N_HEADS, D, D, LANES), F32)],
        scratch_shapes=[pltpu.VMEM((D, LANES), F32), pltpu.VMEM((D, LANES), F32)],
        compiler_params=_cparams(("parallel",)),
        name="decode_rec",
    )(scalars, pt, pt, pt, pt, qkvt, qkvt, qkvt, pt, pt, pt, pt, pt, pt, cos_t, sin_t, c0, n0, m0, sg0, sr0)


def _outproj_kernel(x_ref, osb_ref, oml_ref, ogd_ref, ort_ref, gain_ref, wo_ref, nw_ref, rwh_ref, rwl_ref,
                    x1_ref, h2_ref, lg_ref, *, transposed):
    parts = [osb_ref[...]]
    for ref in (oml_ref, ogd_ref, ort_ref):
        parts.append(ref[...].T if transposed else ref[...])
    y = None
    for g, p in enumerate(parts):
        cols = slice(g * GROUP_W, (g + 1) * GROUP_W)
        t = _dot(p * gain_ref[:, cols], wo_ref[cols, :])
        y = t if y is None else y + t
    x1 = x_ref[...] + y
    h2 = _norm_rows(x1, nw_ref[...])
    x1_ref[...] = x1
    h2_ref[...] = h2.astype(BF16)
    lg_ref[...] = _gates_dot(h2, rwh_ref, rwl_ref)


def _outproj(x, osb, oml, ogd, ort, gain, wo, nw, rwh, rwl, transposed):
    n = x.shape[0]
    tm = min(TM_OUTPROJ, n)
    S = jax.ShapeDtypeStruct
    row = lambda c: pl.BlockSpec((tm, c), lambda i: (i, 0))
    mix = _const_spec((GROUP_W, DEC_BATCH)) if transposed else row(GROUP_W)
    return pl.pallas_call(
        functools.partial(_outproj_kernel, transposed=transposed),
        grid=(n // tm,),
        in_specs=[row(D_MODEL), row(GROUP_W), mix, mix, mix, _const_spec((1, D_MODEL)),
                  _const_spec((D_MODEL, D_MODEL)), _const_spec((1, D_MODEL)),
                  _const_spec((D_MODEL, LANES)), _const_spec((D_MODEL, LANES))],
        out_specs=[row(D_MODEL), row(D_MODEL), row(LANES)],
        out_shape=[S((n, D_MODEL), F32), S((n, D_MODEL), BF16), S((n, LANES), F32)],
        compiler_params=_cparams(("parallel",)),
        name="outproj",
    )(x, osb, oml, ogd, ort, gain, wo, nw, rwh, rwl)


def _route_kernel(lg_ref, rb_ref, tri_ref, lt_ref, cin_ref, pr_ref, rf_ref, ti_ref, cnt_ref):
    @pl.when(pl.program_id(0) == 0)
    def _():
        cnt_ref[...] = cin_ref[...]

    lg = lg_ref[...] + rb_ref[...]
    lane = lax.broadcasted_iota(I32, lg.shape, 1)
    ninf = -jnp.inf
    big = LANES - 1
    rmax = lambda x: jnp.max(x, axis=1, keepdims=True)
    rmin = lambda x: jnp.min(x, axis=1, keepdims=True)
    gl = jnp.where(lane < N_GROUPS, lg, ninf)
    gmax = rmax(gl)
    g_sel = rmin(jnp.where(gl == gmax, lane, big))
    g_prob = 1.0 / jnp.sum(jnp.exp(gl - gmax), axis=1, keepdims=True)
    e_lane = (lane >= N_GROUPS) & (lane < N_GROUPS + N_EXPERTS)
    em = jnp.where(e_lane & (jnp.right_shift(lane - N_GROUPS, 3) == g_sel), lg, ninf)
    v1 = rmax(em)
    i1 = rmin(jnp.where(em == v1, lane, big))
    em2 = jnp.where(lane == i1, ninf, em)
    v2 = rmax(em2)
    i2 = rmin(jnp.where(em2 == v2, lane, big))
    t = jnp.exp(v2 - v1)
    gate1 = g_prob / (1.0 + t)
    gate2 = g_prob * t / (1.0 + t)
    e1 = i1 - N_GROUPS
    e2 = i2 - N_GROUPS
    onehot = jnp.where((lane == e1) | (lane == e2), 1.0, 0.0)
    before = _dot(tri_ref[...], onehot)
    tile_cnt = jnp.sum(onehot, axis=0, keepdims=True)
    padded = jnp.floor((tile_cnt + (RUN_ALIGN - 1)) * (1.0 / RUN_ALIGN)) * RUN_ALIGN
    run_start = _dot(jnp.broadcast_to(padded, (8, LANES)), lt_ref[...])[0:1, :]
    pos = before + run_start
    p1 = jnp.sum(jnp.where(lane == e1, pos, 0.0), axis=1, keepdims=True)
    p2 = jnp.sum(jnp.where(lane == e2, pos, 0.0), axis=1, keepdims=True)
    pr = jnp.where(lane == 0, p1, jnp.where(lane == 1, p2, 0.0)).astype(I32)
    pr_ref[...] = pr.T[0:8, :]
    rf_ref[...] = jnp.where(lane == 0, gate1, jnp.where(lane == 1, gate2, jnp.where(lane == 2, p1,
                            jnp.where(lane == 3, p2, 0.0))))
    row8 = lax.broadcasted_iota(I32, (8, LANES), 0)
    ti_ref[...] = jnp.where(row8 == 0, padded, jnp.where(row8 == 1, cnt_ref[...],
                            jnp.where(row8 == 2, run_start, 0.0))).astype(I32)
    cnt_ref[...] += padded


def _route(logits, rb, tri, lanes_lt, cnt_in):
    n = logits.shape[0]
    tm = tri.shape[0]
    S = jax.ShapeDtypeStruct
    row = pl.BlockSpec((tm, LANES), lambda i: (i, 0))
    return pl.pallas_call(
        _route_kernel,
        grid=(n // tm,),
        in_specs=[row, _const_spec((1, LANES)), _const_spec((tm, tm)), _const_spec((LANES, LANES)),
                  _const_spec((1, LANES))],
        out_specs=[pl.BlockSpec((8, tm), lambda i: (0, i)), row, pl.BlockSpec((8, LANES), lambda i: (i, 0)),
                   _const_spec((1, LANES))],
        out_shape=[S((8, n), I32), S((n, LANES), F32), S((8 * (n // tm), LANES), I32), S((1, LANES), F32)],
        compiler_params=_cparams(("arbitrary",)),
        name="route",
    )(logits, rb, tri, lanes_lt, cnt_in)


def _packed_rows(tm):
    return 2 * tm + N_EXPERTS * RUN_ALIGN


def _for_each_run_chunk(plan_refs, tile, fn):
    offs_ref, len_ref, before_ref, start_ref = plan_refs
    base = tile * N_EXPERTS

    def per_expert(e, _):
        n_rows = len_ref[base + e]
        n_big = n_rows // RUN_CHUNK
        packed0 = start_ref[base + e]
        slot0 = offs_ref[e] + before_ref[base + e]

        def chunks(first, count, rows):
            def per_chunk(c, _):
                off = first + c * rows
                fn(pl.multiple_of(packed0 + off, RUN_ALIGN), pl.multiple_of(slot0 + off, RUN_ALIGN), rows)
                return 0

            lax.fori_loop(0, count, per_chunk, 0)

        chunks(0, n_big, RUN_CHUNK)
        chunks(n_big * RUN_CHUNK, (n_rows - n_big * RUN_CHUNK) // RUN_ALIGN, RUN_ALIGN)
        return 0

    lax.fori_loop(0, N_EXPERTS, per_expert, 0)


def _dispatch_kernel(offs_ref, len_ref, before_ref, start_ref, h_ref, pr_ref, xs_in_ref, xs_ref, pk_s, sem):
    del xs_in_ref
    plan_refs = (offs_ref, len_ref, before_ref, start_ref)
    tm = h_ref.shape[0]
    i = pl.program_id(0)
    slot = lax.broadcasted_iota(I32, (_packed_rows(tm), tm), 0)
    sel = jnp.where((slot == pr_ref[0:1, :]) | (slot == pr_ref[1:2, :]), 1.0, 0.0)
    pk_s[...] = _dot(sel, h_ref[...])

    def copy(packed_row, slot_row, rows):
        return pltpu.make_async_copy(pk_s.at[pl.ds(packed_row, rows), :],
                                     xs_ref.at[pl.ds(slot_row, rows), :], sem.at[0])

    _for_each_run_chunk(plan_refs, i, lambda p, s, n: copy(p, s, n).start())
    _for_each_run_chunk(plan_refs, i, lambda p, s, n: copy(p, s, n).wait())


def _dispatch(plan, h2, pr, xs):
    n = h2.shape[0]
    tm = min(TM_ROUTE, n)
    return pl.pallas_call(
        _dispatch_kernel,
        grid_spec=pltpu.PrefetchScalarGridSpec(
            num_scalar_prefetch=4,
            grid=(n // tm,),
            in_specs=[pl.BlockSpec((tm, D_MODEL), lambda i, *_: (i, 0)),
                      pl.BlockSpec((8, tm), lambda i, *_: (0, i)),
                      pl.BlockSpec(memory_space=pl.ANY)],
            out_specs=pl.BlockSpec(memory_space=pl.ANY),
            scratch_shapes=[pltpu.VMEM((_packed_rows(tm), D_MODEL), F32), pltpu.SemaphoreType.DMA((1,))]),
        out_shape=jax.ShapeDtypeStruct(xs.shape, xs.dtype),
        input_output_aliases={6: 0},
        compiler_params=_cparams(("arbitrary",)),
        name="dispatch",
    )(*plan, h2, pr, xs)


def _experts_kernel(te_ref, ts_ref, xs_ref, wg_ref, wu_ref, wd_ref, ys_ref, wg_s, wu_s, wd_s):
    i = pl.program_id(0)
    valid = ts_ref[i] == i
    fresh = (i == 0) | (te_ref[i] != te_ref[jnp.maximum(i - 1, 0)])

    @pl.when(valid & fresh)
    def _():
        wg_s[...] = wg_ref[0].astype(BF16)
        wu_s[...] = wu_ref[0].astype(BF16)
        wd_s[...] = wd_ref[0].astype(BF16)

    @pl.when(valid)
    def _():
        x = xs_ref[...].astype(BF16)
        a = _dot(x, wg_s[...])
        u = _dot(x, wu_s[...])
        act = a * _sigmoid(a) * u
        ys_ref[...] = _dot(act, wd_s[...])

    @pl.when(jnp.logical_not(valid))
    def _():
        ys_ref[...] = jnp.zeros_like(ys_ref)


def _experts(tile_expert, tile_src, xs, wg, wu, wd, layer):
    tm = TM_EXPERT
    wspec = lambda a, b: pl.BlockSpec((None, 1, a, b), lambda i, te, tv: (layer, te[i], 0, 0))
    return pl.pallas_call(
        _experts_kernel,
        grid_spec=pltpu.PrefetchScalarGridSpec(
            num_scalar_prefetch=2,
            grid=(N_TILES,),
            in_specs=[pl.BlockSpec((tm, D_MODEL), lambda i, te, ts: (ts[i], 0)),
                      wspec(D_MODEL, D_EXPERT), wspec(D_MODEL, D_EXPERT), wspec(D_EXPERT, D_MODEL)],
            out_specs=pl.BlockSpec((tm, D_MODEL), lambda i, te, tv: (i, 0)),
            scratch_shapes=[pltpu.VMEM((D_MODEL, D_EXPERT), BF16), pltpu.VMEM((D_MODEL, D_EXPERT), BF16),
                            pltpu.VMEM((D_EXPERT, D_MODEL), BF16)]),
        out_shape=jax.ShapeDtypeStruct((N_SLOTS, D_MODEL), F32),
        compiler_params=_cparams(("arbitrary",)),
        name="experts",
    )(tile_expert, tile_src, xs, wg, wu, wd)


def _combine_kernel(offs_ref, len_ref, before_ref, start_ref, x1_ref, rf_ref, fw_ref, ys_ref, out_ref,
                    pk_s, sem, *, final):
    plan_refs = (offs_ref, len_ref, before_ref, start_ref)
    tm = x1_ref.shape[0]
    i = pl.program_id(0)
    buf = i % 2

    def copy(b, packed_row, slot_row, rows):
        return pltpu.make_async_copy(ys_ref.at[pl.ds(slot_row, rows), :],
                                     pk_s.at[b, pl.ds(packed_row, rows), :], sem.at[b])

    @pl.when(i == 0)
    def _():
        pk_s[...] = jnp.zeros_like(pk_s)
        _for_each_run_chunk(plan_refs, i, lambda p, s, n: copy(buf, p, s, n).start())

    @pl.when(i + 1 < pl.num_programs(0))
    def _():
        _for_each_run_chunk(plan_refs, i + 1, lambda p, s, n: copy(1 - buf, p, s, n).start())

    _for_each_run_chunk(plan_refs, i, lambda p, s, n: copy(buf, p, s, n).wait())
    slot = lax.broadcasted_iota(I32, (tm, _packed_rows(tm)), 1)
    packed = pk_s[buf].astype(BF16)
    x2 = x1_ref[...]
    for k in range(2):
        sel = jnp.where(slot == rf_ref[:, 2 + k:3 + k].astype(I32), 1.0, 0.0)
        x2 = x2 + rf_ref[:, k:k + 1] * _dot(sel, packed)
    out_ref[...] = _norm_rows(x2, fw_ref[...]) if final else x2


def _combine(plan, x1, rf, fw, ys, final):
    n = x1.shape[0]
    tm = min(TM_ROUTE, n)
    row = lambda c: pl.BlockSpec((tm, c), lambda i, *_: (i, 0))
    return pl.pallas_call(
        functools.partial(_combine_kernel, final=final),
        grid_spec=pltpu.PrefetchScalarGridSpec(
            num_scalar_prefetch=4,
            grid=(n // tm,),
            in_specs=[row(D_MODEL), row(LANES), pl.BlockSpec((1, D_MODEL), lambda i, *_: (0, 0)),
                      pl.BlockSpec(memory_space=pl.ANY)],
            out_specs=row(D_MODEL),
            scratch_shapes=[pltpu.VMEM((2, _packed_rows(tm), D_MODEL), F32), pltpu.SemaphoreType.DMA((2,))]),
        out_shape=jax.ShapeDtypeStruct((n, D_MODEL), F32),
        compiler_params=_cparams(("arbitrary",)),
        name="combine",
    )(*plan, x1, rf, fw, ys)


def _tri(n, kind, chunk=1):
    r = np.arange(n)[:, None]
    c = np.arange(n)[None, :]
    m = {"chunk_lower_incl": (c <= r) & (r // chunk == c // chunk),
         "lower_strict": c < r,
         "row_gt_col": r > c,
         "row_lt_col": r < c,
         "later_page": (r % 8 == c % 8) & (c // 8 > r // 8)}[kind]
    return jnp.asarray(m.astype(np.float32), dtype=BF16)


def _rope_tables(pos):
    half = HEAD_DIM // 2
    inv = ROPE_BASE ** (-np.arange(half, dtype=np.float64) / half)
    ang = np.asarray(pos, np.float64)[:, None] * inv[None, :]
    cos = np.concatenate([np.cos(ang), np.cos(ang)], axis=1)
    sin = np.concatenate([-np.sin(ang), np.sin(ang)], axis=1)
    return cos.astype(np.float32), sin.astype(np.float32)


def _retention_consts(L):
    log_gamma = np.log1p(-np.exp2(-5.0 - np.arange(N_HEADS, dtype=np.float64)))
    idx = np.arange(L, dtype=np.float64)
    diff = np.maximum(idx[:, None] - idx[None, :], 0.0)
    dmat = np.where(idx[None, :] <= idx[:, None], np.exp(log_gamma[:, None, None] * diff), 0.0)
    xi = np.exp(log_gamma[:, None] * (idx + 1.0))
    zeta = np.exp(log_gamma[:, None] * (L - 1.0 - idx))
    xz = np.zeros((L, LANES), np.float64)
    xz[:, 0:N_HEADS] = xi.T
    xz[:, N_HEADS:2 * N_HEADS] = zeta.T
    return (jnp.asarray(dmat, F32), jnp.asarray(xz, F32), jnp.asarray(np.exp(log_gamma * L), F32),
            np.exp(log_gamma))


def _hi_lo(w):
    hi = w.astype(BF16)
    return hi, (w - hi.astype(F32)).astype(BF16)


def _pad_lanes(w):
    return jnp.pad(w, ((0, 0), (0, LANES - w.shape[1])))


def _prep_w_in(w):
    ml0 = 3 * GROUP_W
    mlg = ml0 + 4 * GROUP_W
    gd0 = mlg + 2 * N_HEADS
    gdg = gd0 + 4 * GROUP_W
    rt0 = gdg + 2 * N_HEADS
    main = jnp.concatenate([w[:, 0:ml0], w[:, ml0:mlg], w[:, gd0:gdg], w[:, rt0:]], axis=1).astype(BF16)
    gates = _pad_lanes(jnp.concatenate([w[:, mlg:gd0], w[:, gdg:rt0]], axis=1))
    return (main,) + _hi_lo(gates)


def _tile_plan(tile_info):
    t = tile_info.reshape(-1, 8, LANES)[:, :, :N_EXPERTS]
    return t[:, 0].reshape(-1), t[:, 1].reshape(-1), t[:, 2].reshape(-1)


def _dispatch_plan(cnt):
    ntile = (cnt + TM_EXPERT - 1) // TM_EXPERT
    tile_end = jnp.cumsum(ntile)
    offs = (tile_end - ntile) * TM_EXPERT
    tid = jnp.arange(N_TILES, dtype=I32)
    te = jnp.minimum(jnp.sum(tid[:, None] >= tile_end[None, :], axis=1), N_EXPERTS - 1).astype(I32)
    tile_src = jnp.minimum(tid, tile_end[-1] - 1).astype(I32)
    return offs.astype(I32), te, tile_src


def kernel(x_prompt, x_sample, cache_sb_k, cache_sb_v, state_mlstm_c, state_mlstm_n, state_mlstm_m, state_gdn_s, state_gdn_conv, state_ret_s, page_table, norm_attn_w, w_in, sb_logit_bias, mlstm_gate_bias, gdn_conv_w, gdn_a_log, gdn_dt_bias, head_norm_w, w_out, norm_ffn_w, router_group_w, router_group_b, router_expert_w, router_expert_b, expert_w_gate, expert_w_up, expert_w_down, final_norm_w):
    D = HEAD_DIM
    xp = x_prompt.reshape(N_PROMPT, D_MODEL)
    xd = x_sample.reshape(DEC_BATCH, D_MODEL)
    n_pool = cache_sb_k.shape[1]
    cache_kt = cache_sb_k.transpose(0, 1, 3, 4, 2).reshape(DEPTH, n_pool, GROUP_W, PAGE_SIZE)
    cache_vt = cache_sb_v.transpose(0, 1, 3, 4, 2).reshape(DEPTH, n_pool, GROUP_W, PAGE_SIZE)

    tril_ml = _tri(MLSTM_STEP, "chunk_lower_incl", CHUNK)
    tril_gd = _tri(GDN_STEP, "chunk_lower_incl", GDN_CHUNK)
    tri_sb = _tri(SB_BLOCK, "row_gt_col")
    tri_page = _tri(PAGE_SIZE, "row_gt_col")
    later_pages = _tri(8 * N_PAGES, "later_page")
    lanes_lt = _tri(LANES, "row_lt_col")
    tri_rp = _tri(TM_ROUTE, "lower_strict")
    tri_rd = _tri(DEC_BATCH, "lower_strict")
    cos_p, sin_p = _rope_tables(np.arange(SEQ))
    cos_p = jnp.asarray(np.tile(cos_p, (1, N_HEADS)))
    sin_p = jnp.asarray(np.tile(sin_p, (1, N_HEADS)))
    cos_d, sin_d = _rope_tables([PAST_LEN])
    cos_d = jnp.asarray(np.tile(cos_d.T, (1, LANES)))
    sin_d = jnp.asarray(np.tile(sin_d.T, (1, LANES)))
    dmat, xz, gch, gamma = _retention_consts(CHUNK)
    zeros4 = jnp.zeros((N_HEADS,), F32)
    xs = None

    outs = {k: [] for k in ("kp", "vp", "ks", "vs", "cp", "np", "mp", "cs", "ns", "ms", "gp", "gcp", "gs", "gcs",
                            "rp", "rs")}
    yp = yd = None
    for l in range(DEPTH):
        w_main, wg_hi, wg_lo = _prep_w_in(w_in[l])
        nw = norm_attn_w[l][None, :]
        gate_bias_row = _pad_lanes(jnp.concatenate([mlstm_gate_bias[l], zeros4, gdn_dt_bias[l]])[None, :])
        a_log_row = _pad_lanes(jnp.concatenate([zeros4, zeros4, zeros4, gdn_a_log[l]])[None, :])
        bias = sb_logit_bias[l]

        sbq, sbk, sbv, ml, gd, rt, gt = _inproj_prompt(xp, nw, w_main, wg_hi, wg_lo)
        osb_p = _sb_prompt(sbq, sbk, sbv, bias, tri_sb)
        oml_p, c1p, n1p, m1p = _mlstm_prompt(ml, gt, gate_bias_row, tril_ml)
        ogd_p, s1p, cv1p = _gdn_prompt(gd, gt, gate_bias_row, a_log_row, gdn_conv_w[l], tril_gd)
        ort_p, r1p = _ret_prompt(rt, cos_p, sin_p, dmat, xz, gch)

        sbq_d, gdraw_d, pt = _inproj_decode(xd, nw, w_main, wg_hi, wg_lo)
        osb_d = _sb_decode(sbq_d, cache_kt, cache_vt, page_table, bias, tri_page, later_pages, l)
        qkvt, cv1d = _gdn_decode_conv(gdraw_d, state_gdn_conv[l].transpose(1, 0, 2), gdn_conv_w[l])
        scalars = jnp.stack([mlstm_gate_bias[l][:N_HEADS], mlstm_gate_bias[l][N_HEADS:], gdn_dt_bias[l],
                             jnp.exp(gdn_a_log[l]), jnp.asarray(gamma, F32), zeros4, zeros4, zeros4])
        oml_d, ogd_d, ort_d, c1d, n1d, m1d, s1d, r1d = _decode_rec(
            scalars, pt, qkvt, cos_d, sin_d,
            state_mlstm_c[l].transpose(1, 2, 3, 0), state_mlstm_n[l].transpose(1, 2, 0),
            state_mlstm_m[l].T.reshape(N_HEADS, 1, DEC_BATCH),
            state_gdn_s[l].transpose(1, 2, 3, 0), state_ret_s[l].transpose(1, 2, 3, 0))

        gain = head_norm_w[l][None, :]
        wo = w_out[l].astype(BF16)
        nfw = norm_ffn_w[l][None, :]
        rw_hi, rw_lo = _hi_lo(_pad_lanes(jnp.concatenate([router_group_w[l], router_expert_w[l]], axis=1)))
        rb = _pad_lanes(jnp.concatenate([router_group_b[l], router_expert_b[l]])[None, :])
        x1p, h2p, lgp = _outproj(xp, osb_p, oml_p, ogd_p, ort_p, gain, wo, nfw, rw_hi, rw_lo, False)
        x1d, h2d, lgd = _outproj(xd, osb_d, oml_d, ogd_d, ort_d, gain, wo, nfw, rw_hi, rw_lo, True)
        prp, rfp, tip, cnt_p = _route(lgp, rb, tri_rp, lanes_lt, jnp.zeros((1, LANES), F32))
        prd, rfd, tid, cnt = _route(lgd, rb, tri_rd, lanes_lt, cnt_p)

        offs, te, tile_src = _dispatch_plan(cnt[0, :N_EXPERTS].astype(I32))
        plan_p = (offs,) + _tile_plan(tip)
        plan_d = (offs,) + _tile_plan(tid)
        xs = _dispatch(plan_p, h2p, prp, jnp.zeros((N_SLOTS, D_MODEL), F32) if xs is None else xs)
        xs = _dispatch(plan_d, h2d, prd, xs)
        ys = _experts(te, tile_src, xs, expert_w_gate, expert_w_up, expert_w_down, l)
        final = l == DEPTH - 1
        fw = final_norm_w[None, :]
        xp_next = _combine(plan_p, x1p, rfp, fw, ys, final)
        xd_next = _combine(plan_d, x1d, rfd, fw, ys, final)
        if final:
            yp, yd = xp_next, xd_next
        else:
            xp, xd = xp_next, xd_next

        heads_p = lambda a: a.reshape(BATCH, SEQ, N_HEADS, D)
        heads_t = lambda a: a.reshape(N_HEADS, D, DEC_BATCH).transpose(2, 0, 1)[:, None]
        outs["kp"].append(heads_p(sbk))
        outs["vp"].append(heads_p(sbv))
        outs["ks"].append(heads_t(pt[GROUP_W:2 * GROUP_W]))
        outs["vs"].append(heads_t(pt[2 * GROUP_W:3 * GROUP_W]))
        outs["cp"].append(c1p)
        outs["np"].append(n1p)
        outs["mp"].append(m1p[:, 0, :N_HEADS])
        outs["cs"].append(c1d.transpose(3, 0, 1, 2))
        outs["ns"].append(n1d.transpose(2, 0, 1))
        outs["ms"].append(m1d[:, 0, :].T)
        outs["gp"].append(s1p)
        outs["gcp"].append(cv1p)
        outs["gs"].append(s1d.transpose(3, 0, 1, 2))
        outs["gcs"].append(cv1d.transpose(1, 0, 2))
        outs["rp"].append(r1p)
        outs["rs"].append(r1d.transpose(3, 0, 1, 2))

    st = lambda k: jnp.stack(outs[k], axis=0)
    return (yp.reshape(BATCH, SEQ, D_MODEL), yd.reshape(DEC_BATCH, 1, D_MODEL),
            st("kp"), st("vp"), st("ks"), st("vs"),
            st("cp"), st("np"), st("mp"), st("cs"), st("ns"), st("ms"),
            st("gp"), st("gcp"), st("gs"), st("gcs"), st("rp"), st("rs"))
```

```python
import functools
import math

import numpy as np
import jax
import jax.numpy as jnp
from jax import lax
from jax.experimental import pallas as pl
from jax.experimental.pallas import tpu as pltpu

F32 = jnp.float32
BF16 = jnp.bfloat16
I32 = jnp.int32

D_MODEL = 1024
BATCH = 8
SEQ = 2048
DEPTH = 2
DEC_BATCH = 128
PAST_LEN = 2048
PAGE_SIZE = 128
N_PAGES = PAST_LEN // PAGE_SIZE
HEAD_DIM = 64
N_HEADS = 4
GROUP_W = N_HEADS * HEAD_DIM
CHUNK = 128
GDN_CHUNK = 64
CONV_W = 4
N_GROUPS = 4
EXPERTS_PER_GROUP = 8
N_EXPERTS = N_GROUPS * EXPERTS_PER_GROUP
D_EXPERT = D_MODEL // 2
ROPE_BASE = 10000.0
EPS = 1e-6
SCALE = HEAD_DIM ** -0.5

N_PROMPT = BATCH * SEQ
LANES = 128
N_MAIN = 3 * GROUP_W + 3 * 4 * GROUP_W
N_PROJ = N_MAIN + LANES
COL_ML = 3 * GROUP_W
COL_GD = COL_ML + 4 * GROUP_W
COL_RT = COL_GD + 4 * GROUP_W
COL_GATES = N_MAIN

MLSTM_STEP = 2 * CHUNK
GDN_STEP = 4 * GDN_CHUNK
RET_STEP = 4 * CHUNK
SB_BLOCK = 256
SB_DEC_SEQS = 4
TM_PROMPT = 512
TM_OUTPROJ = 256
TM_EXPERT = 512
N_ASSIGN = 2 * (N_PROMPT + DEC_BATCH)
TM_ROUTE = 512
RUN_ALIGN = 16
RUN_CHUNK = 2 * RUN_ALIGN
N_TOKEN_TILES = N_PROMPT // TM_ROUTE + 1
N_TILES = -(-(N_ASSIGN + N_TOKEN_TILES * N_EXPERTS * (RUN_ALIGN - 1)) // TM_EXPERT) + N_EXPERTS
N_SLOTS = N_TILES * TM_EXPERT
VMEM_LIMIT = 48 * 1024 * 1024

_NT = (((1,), (1,)), ((), ()))
_TN = (((0,), (0,)), ((), ()))


_NN = (((1,), (0,)), ((), ()))


def _mm(a, b, dims):
    return lax.dot_general(a.astype(BF16), b.astype(BF16), dims, preferred_element_type=F32)


def _dot(a, b):
    return _mm(a, b, _NN)


def _dot_nt(a, b):
    return _mm(a, b, _NT)


def _dot_tn(a, b):
    return _mm(a, b, _TN)


def _split3(x):
    x1 = x.astype(BF16)
    r1 = x - x1.astype(F32)
    x2 = r1.astype(BF16)
    x3 = (r1 - x2.astype(F32)).astype(BF16)
    return x1, x2, x3


def _tri_dot(tri, x):
    x1, x2, x3 = _split3(x)
    return _dot(tri, x1) + _dot(tri, x2) + _dot(tri, x3)


def _log_sigmoid(z):
    return jnp.minimum(z, 0.0) - jnp.log(1.0 + jnp.exp(-jnp.abs(z)))


def _softplus(z):
    return jnp.maximum(z, 0.0) + jnp.log(1.0 + jnp.exp(-jnp.abs(z)))


def _sigmoid(z):
    return 1.0 / (1.0 + jnp.exp(-z))


def _head_rms(x):
    return x * lax.rsqrt(jnp.mean(x * x, axis=-1, keepdims=True) + EPS)


def _cparams(sem):
    return pltpu.CompilerParams(dimension_semantics=sem, vmem_limit_bytes=VMEM_LIMIT)


def _const_spec(shape):
    nd = len(shape)
    return pl.BlockSpec(shape, lambda *_: (0,) * nd)


def _norm_rows(x, w):
    return x * lax.rsqrt(jnp.mean(x * x, axis=-1, keepdims=True) + EPS) * w


def _gates_dot(h, wgh_ref, wgl_ref):
    hb = h.astype(BF16)
    hl = (h - hb.astype(F32)).astype(BF16)
    return _dot(hb, wgh_ref[...]) + _dot(hl, wgh_ref[...]) + _dot(hb, wgl_ref[...])


def _inproj_prompt_kernel(x_ref, nw_ref, w_ref, wgh_ref, wgl_ref,
                          sbq_ref, sbk_ref, sbv_ref, ml_ref, gd_ref, rt_ref, gt_ref):
    h = _norm_rows(x_ref[...], nw_ref[...])
    hb = h.astype(BF16)
    seg = lambda a, b: _dot(hb, w_ref[:, a:b])
    sbq_ref[...] = seg(0, GROUP_W)
    sbk_ref[...] = seg(GROUP_W, 2 * GROUP_W)
    sbv_ref[...] = seg(2 * GROUP_W, 3 * GROUP_W)
    ml_ref[...] = seg(COL_ML, COL_GD)
    gd_ref[...] = seg(COL_GD, COL_RT)
    rt_ref[...] = seg(COL_RT, N_MAIN)
    gt_ref[...] = _gates_dot(h, wgh_ref, wgl_ref)


def _inproj_prompt(x, nw, w, wgh, wgl):
    n = x.shape[0]
    tm = TM_PROMPT
    row = lambda c: pl.BlockSpec((tm, c), lambda i: (i, 0))
    S = jax.ShapeDtypeStruct
    return pl.pallas_call(
        _inproj_prompt_kernel,
        grid=(n // tm,),
        in_specs=[row(D_MODEL), _const_spec((1, D_MODEL)), _const_spec((D_MODEL, N_MAIN)),
                  _const_spec((D_MODEL, LANES)), _const_spec((D_MODEL, LANES))],
        out_specs=[row(GROUP_W), row(GROUP_W), row(GROUP_W), row(4 * GROUP_W), row(4 * GROUP_W),
                   row(4 * GROUP_W), row(LANES)],
        out_shape=[S((n, GROUP_W), F32)] * 3 + [S((n, 4 * GROUP_W), F32)] * 3 + [S((n, LANES), F32)],
        compiler_params=_cparams(("parallel",)),
        name="inproj_prompt",
    )(x, nw, w, wgh, wgl)


def _inproj_decode_kernel(x_ref, nw_ref, w_ref, wgh_ref, wgl_ref, sbq_ref, gdraw_ref, pt_ref):
    h = _norm_rows(x_ref[...], nw_ref[...])
    hb = h.astype(BF16)
    for j in range(N_MAIN // LANES):
        p = _dot(hb, w_ref[:, j * LANES:(j + 1) * LANES])
        if j < GROUP_W // LANES:
            sbq_ref[:, j * LANES:(j + 1) * LANES] = p
        c0 = j * LANES - COL_GD
        if 0 <= c0 < 3 * GROUP_W:
            gdraw_ref[:, c0:c0 + LANES] = p
        pt_ref[j * LANES:(j + 1) * LANES, :] = p.T
    pt_ref[N_MAIN:N_PROJ, :] = _gates_dot(h, wgh_ref, wgl_ref).T


def _inproj_decode(x, nw, w, wgh, wgl):
    S = jax.ShapeDtypeStruct
    return pl.pallas_call(
        _inproj_decode_kernel,
        out_shape=[S((DEC_BATCH, GROUP_W), F32), S((DEC_BATCH, 3 * GROUP_W), F32), S((N_PROJ, DEC_BATCH), F32)],
        compiler_params=_cparams(None),
        name="inproj_decode",
    )(x, nw, w, wgh, wgl)


def _sb_prompt_kernel(bias_ref, q_ref, k_ref, v_ref, tri_ref, o_ref, acc_s, carry_s):
    tb = SB_BLOCK
    qi = pl.program_id(1)
    tri = tri_ref[...]
    r = lax.broadcasted_iota(I32, (tb, tb), 0)
    c = lax.broadcasted_iota(I32, (tb, tb), 1)
    dmask = c < r
    acc_s[...] = jnp.zeros_like(acc_s)
    carry_s[...] = jnp.zeros_like(carry_s)
    q = q_ref[...] * SCALE

    def block(j, mask):
        start = pl.multiple_of(j * tb, tb)
        kj = k_ref[pl.ds(start, tb), :]
        vj = v_ref[pl.ds(start, tb), :]
        heads = range(N_HEADS)
        sls = [slice(h * HEAD_DIM, (h + 1) * HEAD_DIM) for h in heads]
        z = [_dot_nt(q[:, sls[h]], kj[:, sls[h]]) + bias_ref[h] for h in heads]
        lsz = [_log_sigmoid(z[h]) for h in heads]
        lk = [lsz[h] - z[h] for h in heads]
        if mask is not None:
            lk = [jnp.where(mask, x, 0.0) for x in lk]
        lw_all = _dot(jnp.concatenate([x.astype(BF16) for x in lk], axis=0), tri)
        lw = [lw_all[h * tb:(h + 1) * tb, :] for h in heads]
        a = [jnp.exp(lsz[h] + lw[h] + carry_s[:, h:h + 1]) for h in heads]
        if mask is not None:
            a = [jnp.where(mask, x, 0.0) for x in a]
        av = [_dot(a[h], vj[:, sls[h]]) for h in heads]
        for h in heads:
            acc_s[:, sls[h]] += av[h]
            carry_s[:, h:h + 1] += lw[h][:, 0:1] + lk[h][:, 0:1]

    block(qi, dmask)

    def body(it, _):
        block(qi - 1 - it, None)
        return 0

    lax.fori_loop(0, qi, body, 0)
    for h in range(N_HEADS):
        sl = slice(h * HEAD_DIM, (h + 1) * HEAD_DIM)
        o_ref[:, sl] = _head_rms(acc_s[:, sl])


def _sb_prompt(q, k, v, bias, tri):
    tb = SB_BLOCK
    nq = SEQ // tb
    return pl.pallas_call(
        _sb_prompt_kernel,
        grid_spec=pltpu.PrefetchScalarGridSpec(
            num_scalar_prefetch=0,
            grid=(BATCH, nq),
            in_specs=[pl.BlockSpec(memory_space=pltpu.SMEM),
                      pl.BlockSpec((tb, GROUP_W), lambda b, i: (b * nq + i, 0)),
                      pl.BlockSpec((SEQ, GROUP_W), lambda b, i: (b, 0)),
                      pl.BlockSpec((SEQ, GROUP_W), lambda b, i: (b, 0)),
                      _const_spec((tb, tb))],
            out_specs=pl.BlockSpec((tb, GROUP_W), lambda b, i: (b * nq + i, 0)),
            scratch_shapes=[pltpu.VMEM((tb, GROUP_W), F32), pltpu.VMEM((tb, LANES), F32)]),
        out_shape=jax.ShapeDtypeStruct((N_PROMPT, GROUP_W), F32),
        compiler_params=_cparams(("parallel", "parallel")),
        name="sb_prompt",
    )(bias, q, k, v, tri)


def _sb_decode_kernel(pt_ref, bias_ref, q_ref, *rest):
    S = SB_DEC_SEQS
    n_in = S * N_PAGES
    k_refs = rest[:n_in]
    v_refs = rest[n_in:2 * n_in]
    tri_ref, pg_ref, o_ref = rest[2 * n_in:]
    n_rows = 8 * N_PAGES
    r0 = (pl.program_id(0) * S) % 8
    seqs = range(S)
    rowi = lax.broadcasted_iota(I32, (8, GROUP_W), 0)
    lanei = lax.broadcasted_iota(I32, (8, GROUP_W), 1)
    head_of_lane = jnp.right_shift(lanei, 6)
    row8 = lax.broadcasted_iota(I32, (8, 1), 0)
    bias = jnp.zeros((8, 1), F32)
    for h in range(N_HEADS):
        bias = jnp.where(row8 == h, bias_ref[h], bias)
    qbd = [jnp.where(head_of_lane == rowi, q_ref[pl.ds(r0 + s, 1), :] * SCALE, 0.0).astype(BF16) for s in seqs]
    z = [jnp.concatenate([_dot(qbd[s], k_refs[s * N_PAGES + j][0, 0]) + bias for j in range(N_PAGES)], axis=0)
         for s in seqs]
    lsz = [_log_sigmoid(z[s]) for s in seqs]
    lk = [lsz[s] - z[s] for s in seqs]
    lw = [_dot(lk[s], tri_ref[...]) for s in seqs]
    tot = [jnp.broadcast_to(lw[s][:, 0:1] + lk[s][:, 0:1], (n_rows, LANES)) for s in seqs]
    later_pages = [_tri_dot(pg_ref[...], tot[s]) for s in seqs]
    a = [jnp.exp(lsz[s] + lw[s] + later_pages[s]) for s in seqs]
    for s in seqs:
        acc = jnp.zeros((8, GROUP_W), F32)
        for j in range(N_PAGES):
            acc = acc + _dot_nt(a[s][j * 8:(j + 1) * 8, :], v_refs[s * N_PAGES + j][0, 0])
        orow = jnp.sum(jnp.where(head_of_lane == rowi, acc, 0.0), axis=0, keepdims=True)
        pieces = [_head_rms(orow[:, h * HEAD_DIM:(h + 1) * HEAD_DIM]) for h in range(N_HEADS)]
        o_ref[pl.ds(r0 + s, 1), :] = jnp.concatenate(pieces, axis=1)


def _sb_decode(q, cache_kt, cache_vt, page_table, bias, tri, later_pages, layer):
    S = SB_DEC_SEQS

    def page_spec(s, j):
        return pl.BlockSpec((1, 1, GROUP_W, PAGE_SIZE), lambda i, pt: (layer, pt[i * S + s, j], 0, 0))

    pages = [page_spec(s, j) for s in range(S) for j in range(N_PAGES)]
    in_specs = ([pl.BlockSpec(memory_space=pltpu.SMEM),
                 pl.BlockSpec((8, GROUP_W), lambda i, pt: (i * S // 8, 0))]
                + pages * 2
                + [pl.BlockSpec((PAGE_SIZE, PAGE_SIZE), lambda i, pt: (0, 0)),
                   pl.BlockSpec((8 * N_PAGES, 8 * N_PAGES), lambda i, pt: (0, 0))])
    return pl.pallas_call(
        _sb_decode_kernel,
        grid_spec=pltpu.PrefetchScalarGridSpec(
            num_scalar_prefetch=1,
            grid=(DEC_BATCH // S,),
            in_specs=in_specs,
            out_specs=pl.BlockSpec((8, GROUP_W), lambda i, pt: (i * S // 8, 0))),
        out_shape=jax.ShapeDtypeStruct((DEC_BATCH, GROUP_W), F32),
        compiler_params=_cparams(("arbitrary",)),
        name="sb_decode",
    )(page_table, bias, q, *([cache_kt] * (S * N_PAGES)), *([cache_vt] * (S * N_PAGES)), tri, later_pages)


def _chunk_masks(L):
    r = lax.broadcasted_iota(I32, (L, L), 0)
    c = lax.broadcasted_iota(I32, (L, L), 1)
    return c <= r, c < r


def _head_cols(blk, h):
    return slice(blk * GROUP_W + h * HEAD_DIM, blk * GROUP_W + (h + 1) * HEAD_DIM)


def _mlstm_prompt_kernel(x_ref, g_ref, gb_ref, tril_ref, o_ref, c1_ref, n1_ref, m1_ref, c_s, n_s, m_s):
    L = CHUNK
    R = x_ref.shape[0]
    ci = pl.program_id(1)

    @pl.when(ci == 0)
    def _():
        c_s[...] = jnp.zeros_like(c_s)
        n_s[...] = jnp.zeros_like(n_s)
        m_s[...] = jnp.zeros_like(m_s)

    g = g_ref[...] + gb_ref[...]
    lane = lax.broadcasted_iota(I32, (R, LANES), 1)
    lf = jnp.where((lane >= N_HEADS) & (lane < 2 * N_HEADS), _log_sigmoid(g), 0.0)
    cum = _tri_dot(tril_ref[...], lf)
    xt = jnp.where(lane < N_HEADS, g, cum).T
    causal, _ = _chunk_masks(L)
    P = [(s_i, h) for s_i in range(R // L) for h in range(N_HEADS)]
    rows = [slice(s_i * L, (s_i + 1) * L) for s_i, _ in P]
    hs = [h for _, h in P]
    np_ = range(len(P))
    bc = [cum[rows[p], N_HEADS + hs[p]:N_HEADS + hs[p] + 1] for p in np_]
    li = [g[rows[p], hs[p]:hs[p] + 1] for p in np_]
    log_d = [jnp.where(causal, bc[p] - xt[N_HEADS + hs[p]:N_HEADS + hs[p] + 1, rows[p]] + xt[hs[p]:hs[p] + 1, rows[p]],
                       -jnp.inf) for p in np_]
    b_last = [bc[p][L - 1:L, :] for p in np_]
    log_w = [b_last[p] - bc[p] + li[p] for p in np_]
    d_max = [jnp.max(log_d[p], axis=1, keepdims=True) for p in np_]
    w_max = [jnp.max(log_w[p], axis=0, keepdims=True) for p in np_]
    m = [m_s[0:1, h:h + 1] for h in range(N_HEADS)]
    m_in, m_out = [], []
    for p in np_:
        m_in.append(m[hs[p]])
        m[hs[p]] = jnp.maximum(b_last[p] + m[hs[p]], w_max[p])
        m_out.append(m[hs[p]])
    log_inter = [bc[p] + m_in[p] for p in np_]
    m_row = [jnp.maximum(log_inter[p], d_max[p]) for p in np_]
    dexp = [jnp.exp(log_d[p] - m_row[p]) for p in np_]
    w_inter = [jnp.exp(log_inter[p] - m_row[p]) for p in np_]
    decay = [jnp.exp(b_last[p] + m_in[p] - m_out[p]) for p in np_]
    q = [x_ref[rows[p], _head_cols(0, hs[p])] for p in np_]
    k = [x_ref[rows[p], _head_cols(1, hs[p])] * SCALE for p in np_]
    v = [x_ref[rows[p], _head_cols(2, hs[p])] for p in np_]
    kw = [k[p] * jnp.exp(log_w[p] - m_out[p]) for p in np_]
    s_mat = [_dot_nt(q[p], k[p]) * dexp[p] for p in np_]
    sv = [_dot(s_mat[p], v[p]) for p in np_]
    kv = [_dot_tn(kw[p], v[p]) for p in np_]
    kw_sum = [jnp.sum(kw[p], axis=0, keepdims=True) for p in np_]
    c = [c_s[h] for h in range(N_HEADS)]
    n = [n_s[h:h + 1, :] for h in range(N_HEADS)]
    c_in, n_in = [], []
    for p in np_:
        c_in.append(c[hs[p]])
        n_in.append(n[hs[p]])
        c[hs[p]] = decay[p] * c[hs[p]] + kv[p]
        n[hs[p]] = decay[p] * n[hs[p]] + kw_sum[p]
    qc = [_dot(q[p], c_in[p]) for p in np_]
    s_sum = [jnp.sum(s_mat[p], axis=1, keepdims=True) for p in np_]
    qn = [jnp.sum(q[p] * n_in[p], axis=1, keepdims=True) for p in np_]
    hh = [(sv[p] + w_inter[p] * qc[p]) / jnp.maximum(jnp.abs(s_sum[p] + w_inter[p] * qn[p]), jnp.exp(-m_row[p]))
          for p in np_]
    ms = [jnp.mean(hh[p] * hh[p], axis=1, keepdims=True) for p in np_]
    for p in np_:
        og = x_ref[rows[p], _head_cols(3, hs[p])]
        o_ref[rows[p], hs[p] * HEAD_DIM:(hs[p] + 1) * HEAD_DIM] = hh[p] * lax.rsqrt(ms[p] + EPS) * _sigmoid(og)
    for h in range(N_HEADS):
        c_s[h] = c[h]
        n_s[h:h + 1, :] = n[h]
        m_s[0:1, h:h + 1] = m[h]

    @pl.when(ci == pl.num_programs(1) - 1)
    def _():
        c1_ref[0] = c_s[...]
        n1_ref[0] = n_s[...]
        m1_ref[0] = m_s[...]


def _mlstm_prompt(ml, gates, gate_bias_row, tril):
    L = MLSTM_STEP
    nc = SEQ // L
    S = jax.ShapeDtypeStruct
    return pl.pallas_call(
        _mlstm_prompt_kernel,
        grid=(BATCH, nc),
        in_specs=[pl.BlockSpec((L, 4 * GROUP_W), lambda b, c: (b * nc + c, 0)),
                  pl.BlockSpec((L, LANES), lambda b, c: (b * nc + c, 0)),
                  _const_spec((1, LANES)), _const_spec((L, L))],
        out_specs=[pl.BlockSpec((L, GROUP_W), lambda b, c: (b * nc + c, 0)),
                   pl.BlockSpec((1, N_HEADS, HEAD_DIM, HEAD_DIM), lambda b, c: (b, 0, 0, 0)),
                   pl.BlockSpec((1, N_HEADS, HEAD_DIM), lambda b, c: (b, 0, 0)),
                   pl.BlockSpec((1, 1, LANES), lambda b, c: (b, 0, 0))],
        out_shape=[S((N_PROMPT, GROUP_W), F32), S((BATCH, N_HEADS, HEAD_DIM, HEAD_DIM), F32),
                   S((BATCH, N_HEADS, HEAD_DIM), F32), S((BATCH, 1, LANES), F32)],
        scratch_shapes=[pltpu.VMEM((N_HEADS, HEAD_DIM, HEAD_DIM), F32), pltpu.VMEM((N_HEADS, HEAD_DIM), F32),
                        pltpu.VMEM((1, LANES), F32)],
        compiler_params=_cparams(("parallel", "arbitrary")),
        name="mlstm_prompt",
    )(ml, gates, gate_bias_row, tril)


def _unit_lower_inverse(a, L):
    r = lax.broadcasted_iota(I32, (L, L), 0)
    c = lax.broadcasted_iota(I32, (L, L), 1)
    p = jnp.where(r == c, 1.0, 0.0) - a
    x = a
    power = 1
    while 2 * power < L:
        x = _dot(x, x)
        p = p + _dot(p, x)
        power *= 2
    return p


def _gdn_prompt_kernel(x_ref, g_ref, gb_ref, al_ref, cw_ref, tril_ref, o_ref, s1_ref, cv_ref, s_s, xe_s):
    L = GDN_CHUNK
    R = x_ref.shape[0]
    nq = 3 * GROUP_W
    ci = pl.program_id(1)

    @pl.when(ci == 0)
    def _():
        s_s[...] = jnp.zeros_like(s_s)
        xe_s[0:8, :] = jnp.zeros((8, nq), F32)

    raw = x_ref[:, 0:nq]
    xe_s[8:8 + R, :] = raw
    conv = (cw_ref[3:4, :] * raw + cw_ref[2:3, :] * xe_s[7:7 + R, :]
            + cw_ref[1:2, :] * xe_s[6:6 + R, :] + cw_ref[0:1, :] * xe_s[5:5 + R, :])
    xe_s[0:8, :] = raw[R - 8:R, :]
    qkv = conv * _sigmoid(conv)

    g = g_ref[...] + gb_ref[...]
    lane = lax.broadcasted_iota(I32, (R, LANES), 1)
    beta_all = _sigmoid(g)
    gd = jnp.where((lane >= 3 * N_HEADS) & (lane < 4 * N_HEADS), -jnp.exp(al_ref[...]) * _softplus(g), 0.0)
    gcum = _tri_dot(tril_ref[...], gd)
    xt = gcum.T
    incl, strict = _chunk_masks(L)
    prob = [(s_i, h) for s_i in range(R // L) for h in range(N_HEADS)]
    loc = []
    for s_i, h in prob:
        rows = slice(s_i * L, (s_i + 1) * L)
        q = qkv[rows, _head_cols(0, h)]
        k = qkv[rows, _head_cols(1, h)]
        v = qkv[rows, _head_cols(2, h)]
        q = q * lax.rsqrt(jnp.sum(q * q, axis=1, keepdims=True) + EPS) * SCALE
        k = k * lax.rsqrt(jnp.sum(k * k, axis=1, keepdims=True) + EPS)
        b = beta_all[rows, 2 * N_HEADS + h:2 * N_HEADS + h + 1]
        gc = gcum[rows, 3 * N_HEADS + h:3 * N_HEADS + h + 1]
        gc_row = xt[3 * N_HEADS + h:3 * N_HEADS + h + 1, rows]
        decay = jnp.exp(jnp.where(incl, gc - gc_row, -jnp.inf))
        kb = k * b
        egc = jnp.exp(gc)
        gl = gc[L - 1:L, :]
        loc.append(dict(q=q, k=k, kb=kb, decay=decay, vb=v * b, kbe=kb * egc, q_dec=q * egc,
                        k_dec=k * jnp.exp(gl - gc), g_last=jnp.exp(gl)))
    a_mat = [jnp.where(strict, _dot_nt(d["kb"], d["k"]) * d["decay"], 0.0) for d in loc]
    qk_mat = [_dot_nt(d["q"], d["k"]) * d["decay"] for d in loc]
    eye = jnp.where(incl & jnp.logical_not(strict), 1.0, 0.0)
    pw = a_mat
    inv = [eye - a for a in a_mat]
    power = 1
    while 2 * power < L:
        pw = [_dot(x, x) for x in pw]
        inv = [p + _dot(p, x) for p, x in zip(inv, pw)]
        power *= 2
    us = [_dot(t, d["vb"]) for t, d in zip(inv, loc)]
    ws = [_dot(t, d["kbe"]) for t, d in zip(inv, loc)]
    s = [s_s[h] for h in range(N_HEADS)]
    heads = range(N_HEADS)
    for s_i in range(R // L):
        rows = slice(s_i * L, (s_i + 1) * L)
        p0 = s_i * N_HEADS
        w_s = [_dot(ws[p0 + h], s[h]) for h in heads]
        q_s = [_dot(loc[p0 + h]["q_dec"], s[h]) for h in heads]
        v_new = [us[p0 + h] - w_s[h] for h in heads]
        o = [q_s[h] + _dot(qk_mat[p0 + h], v_new[h]) for h in heads]
        s = [s[h] * loc[p0 + h]["g_last"] + _dot_tn(loc[p0 + h]["k_dec"], v_new[h]) for h in heads]
        for h in heads:
            gate = x_ref[rows, _head_cols(3, h)]
            o_ref[rows, h * HEAD_DIM:(h + 1) * HEAD_DIM] = _head_rms(o[h]) * (gate * _sigmoid(gate))
    for h in range(N_HEADS):
        s_s[h] = s[h]

    @pl.when(ci == pl.num_programs(1) - 1)
    def _():
        s1_ref[0] = s_s[...]
        cv_ref[0] = xe_s[8 - (CONV_W - 1):8, :]


def _gdn_prompt(gd, gates, gate_bias_row, a_log_row, conv_w, tril):
    L = GDN_STEP
    nc = SEQ // L
    S = jax.ShapeDtypeStruct
    return pl.pallas_call(
        _gdn_prompt_kernel,
        grid=(BATCH, nc),
        in_specs=[pl.BlockSpec((L, 4 * GROUP_W), lambda b, c: (b * nc + c, 0)),
                  pl.BlockSpec((L, LANES), lambda b, c: (b * nc + c, 0)),
                  _const_spec((1, LANES)), _const_spec((1, LANES)), _const_spec((CONV_W, 3 * GROUP_W)),
                  _const_spec((L, L))],
        out_specs=[pl.BlockSpec((L, GROUP_W), lambda b, c: (b * nc + c, 0)),
                   pl.BlockSpec((1, N_HEADS, HEAD_DIM, HEAD_DIM), lambda b, c: (b, 0, 0, 0)),
                   pl.BlockSpec((1, CONV_W - 1, 3 * GROUP_W), lambda b, c: (b, 0, 0))],
        out_shape=[S((N_PROMPT, GROUP_W), F32), S((BATCH, N_HEADS, HEAD_DIM, HEAD_DIM), F32),
                   S((BATCH, CONV_W - 1, 3 * GROUP_W), F32)],
        scratch_shapes=[pltpu.VMEM((N_HEADS, HEAD_DIM, HEAD_DIM), F32), pltpu.VMEM((8 + L, 3 * GROUP_W), F32)],
        compiler_params=_cparams(("parallel", "arbitrary")),
        name="gdn_prompt",
    )(gd, gates, gate_bias_row, a_log_row, conv_w, tril)


def _rope_rows(x, cos, sin_signed):
    lane = lax.broadcasted_iota(I32, x.shape, 1)
    first = jnp.bitwise_and(lane, HEAD_DIM - 1) < HEAD_DIM // 2
    w = x.shape[1]
    swapped = jnp.where(first, pltpu.roll(x, w - HEAD_DIM // 2, 1), pltpu.roll(x, HEAD_DIM // 2, 1))
    return x * cos + swapped * sin_signed


def _ret_prompt_kernel(x_ref, cos_ref, sin_ref, dm_ref, xz_ref, gch_ref, o_ref, s1_ref, s_s):
    ci = pl.program_id(1)

    @pl.when(ci == 0)
    def _():
        s_s[...] = jnp.zeros_like(s_s)

    L = CHUNK
    cos = cos_ref[...]
    sin = sin_ref[...]
    qr = _rope_rows(x_ref[:, 0:GROUP_W], cos, sin)
    kr = _rope_rows(x_ref[:, GROUP_W:2 * GROUP_W], cos, sin) * SCALE
    prob = []
    for s_i in range(x_ref.shape[0] // L):
        rows = slice(s_i * L, (s_i + 1) * L)
        for h in range(N_HEADS):
            sl = slice(h * HEAD_DIM, (h + 1) * HEAD_DIM)
            prob.append(dict(rows=rows, h=h, sl=sl, q=qr[rows, sl], k=kr[rows, sl], v=x_ref[rows, _head_cols(2, h)]))
    qk = [_dot_nt(p["q"], p["k"]) * dm_ref[p["h"]] for p in prob]
    intra = [_dot(a, p["v"]) for a, p in zip(qk, prob)]
    kv = [_dot_tn(p["k"] * xz_ref[:, N_HEADS + p["h"]:N_HEADS + p["h"] + 1], p["v"]) for p in prob]
    s = [s_s[h] for h in range(N_HEADS)]
    s_in = []
    for p, kv_p in zip(prob, kv):
        s_in.append(s[p["h"]])
        s[p["h"]] = s[p["h"]] * gch_ref[p["h"]] + kv_p
    inter = [_dot(p["q"], s0) * xz_ref[:, p["h"]:p["h"] + 1] for p, s0 in zip(prob, s_in)]
    for p, a, b in zip(prob, intra, inter):
        gate = x_ref[p["rows"], _head_cols(3, p["h"])]
        o_ref[p["rows"], p["sl"]] = _head_rms(a + b) * (gate * _sigmoid(gate))
    for h in range(N_HEADS):
        s_s[h] = s[h]

    @pl.when(ci == pl.num_programs(1) - 1)
    def _():
        s1_ref[0] = s_s[...]


def _ret_prompt(rt, cos, sin, dmat, xz, gch):
    L = RET_STEP
    nc = SEQ // L
    S = jax.ShapeDtypeStruct
    return pl.pallas_call(
        _ret_prompt_kernel,
        grid=(BATCH, nc),
        in_specs=[pl.BlockSpec((L, 4 * GROUP_W), lambda b, c: (b * nc + c, 0)),
                  pl.BlockSpec((L, GROUP_W), lambda b, c: (c, 0)),
                  pl.BlockSpec((L, GROUP_W), lambda b, c: (c, 0)),
                  _const_spec((N_HEADS, CHUNK, CHUNK)), _const_spec((CHUNK, LANES)),
                  pl.BlockSpec(memory_space=pltpu.SMEM)],
        out_specs=[pl.BlockSpec((L, GROUP_W), lambda b, c: (b * nc + c, 0)),
                   pl.BlockSpec((1, N_HEADS, HEAD_DIM, HEAD_DIM), lambda b, c: (b, 0, 0, 0))],
        out_shape=[S((N_PROMPT, GROUP_W), F32), S((BATCH, N_HEADS, HEAD_DIM, HEAD_DIM), F32)],
        scratch_shapes=[pltpu.VMEM((N_HEADS, HEAD_DIM, HEAD_DIM), F32)],
        compiler_params=_cparams(("parallel", "arbitrary")),
        name="ret_prompt",
    )(rt, cos, sin, dmat, xz, gch)


def _gdn_decode_conv_kernel(raw_ref, c0_ref, cw_ref, qkvt_ref, cv_ref):
    raw = raw_ref[...]
    conv = (cw_ref[3:4, :] * raw + cw_ref[2:3, :] * c0_ref[2] + cw_ref[1:2, :] * c0_ref[1]
            + cw_ref[0:1, :] * c0_ref[0])
    qkv = conv * _sigmoid(conv)
    cv_ref[0] = c0_ref[1]
    cv_ref[1] = c0_ref[2]
    cv_ref[2] = raw
    for blk in range(3):
        for h in range(N_HEADS):
            lo = blk * GROUP_W + h * HEAD_DIM
            x = qkv[:, lo:lo + HEAD_DIM]
            if blk == 0:
                x = x * lax.rsqrt(jnp.sum(x * x, axis=1, keepdims=True) + EPS) * SCALE
            elif blk == 1:
                x = x * lax.rsqrt(jnp.sum(x * x, axis=1, keepdims=True) + EPS)
            if h % 2 == 0:
                pair = x
            else:
                qkvt_ref[lo - HEAD_DIM:lo + HEAD_DIM, :] = jnp.concatenate([pair, x], axis=1).T


def _gdn_decode_conv(raw, conv0, conv_w):
    S = jax.ShapeDtypeStruct
    return pl.pallas_call(
        _gdn_decode_conv_kernel,
        out_shape=[S((3 * GROUP_W, DEC_BATCH), F32), S((CONV_W - 1, DEC_BATCH, 3 * GROUP_W), F32)],
        compiler_params=_cparams(None),
        name="gdn_decode_conv",
    )(raw, conv0, conv_w)


def _decode_rec_kernel(sc_ref, mlq_ref, mlk_ref, mlv_ref, mlo_ref, gq_ref, gk_ref, gv_ref, gg_ref,
                       rq_ref, rk_ref, rv_ref, rg_ref, gt_ref, cos_ref, sin_ref,
                       c0_ref, n0_ref, m0_ref, sg0_ref, sr0_ref,
                       oml_ref, ogd_ref, ort_ref, c1_ref, n1_ref, m1_ref, sg1_ref, sr1_ref, va_s, vb_s):
    h = pl.program_id(0)
    D = HEAD_DIM
    sum0 = lambda x: jnp.sum(x, axis=0, keepdims=True)
    rms0 = lambda x: x * lax.rsqrt(sum0(x * x) * (1.0 / D) + EPS)

    li = gt_ref[pl.ds(h, 1), :] + sc_ref[0, h]
    lf = _log_sigmoid(gt_ref[pl.ds(N_HEADS + h, 1), :] + sc_ref[1, h])
    m0 = m0_ref[0]
    q = mlq_ref[...]
    k = mlk_ref[...] * SCALE
    v = mlv_ref[...]
    log_inter = lf + m0
    m_row = jnp.maximum(log_inter, li)
    s = sum0(q * k) * jnp.exp(li - m_row)
    w_inter = jnp.exp(log_inter - m_row)
    decay = jnp.exp(lf + m0 - m_row)
    kw = k * jnp.exp(li - m_row)

    va_s[...] = kw

    def ml_body(d, qc):
        c_d = c0_ref[0, d]
        c1_ref[0, d] = decay * c_d + va_s[pl.ds(d, 1), :] * v
        return qc + mlq_ref[pl.ds(d, 1), :] * c_d

    n0 = n0_ref[0]
    n1_ref[0] = decay * n0 + kw
    qc = lax.fori_loop(0, D, ml_body, jnp.zeros((D, LANES), F32))
    num = s * v + w_inter * qc
    den = s + w_inter * sum0(q * n0)
    hh = num / jnp.maximum(jnp.abs(den), jnp.exp(-m_row))
    m1_ref[0] = m_row
    oml_ref[...] = rms0(hh) * _sigmoid(mlo_ref[...])

    beta = _sigmoid(gt_ref[pl.ds(2 * N_HEADS + h, 1), :])
    gdec = -sc_ref[3, h] * _softplus(gt_ref[pl.ds(3 * N_HEADS + h, 1), :] + sc_ref[2, h])
    eg = jnp.exp(gdec)
    gv = gv_ref[...]

    def ks_body(d, acc):
        return acc + gk_ref[pl.ds(d, 1), :] * sg0_ref[0, d]

    ks = lax.fori_loop(0, D, ks_body, jnp.zeros((D, LANES), F32))
    v_new = beta * gv - (beta * eg) * ks

    def gd_body(d, acc):
        s_new = eg * sg0_ref[0, d] + gk_ref[pl.ds(d, 1), :] * v_new
        sg1_ref[0, d] = s_new
        return acc + gq_ref[pl.ds(d, 1), :] * s_new

    og = lax.fori_loop(0, D, gd_body, jnp.zeros((D, LANES), F32))
    gate = gg_ref[...]
    ogd_ref[...] = rms0(og) * (gate * _sigmoid(gate))

    half = D // 2

    def rope(ref):
        x = ref[...]
        sw = jnp.concatenate([x[half:, :], x[:half, :]], axis=0)
        return x * cos_ref[...] + sw * sin_ref[...]

    va_s[...] = rope(rq_ref)
    vb_s[...] = rope(rk_ref) * SCALE
    rv = rv_ref[...]
    gamma = sc_ref[4, h]

    def rt_body(d, acc):
        s_new = gamma * sr0_ref[0, d] + vb_s[pl.ds(d, 1), :] * rv
        sr1_ref[0, d] = s_new
        return acc + va_s[pl.ds(d, 1), :] * s_new

    ort = lax.fori_loop(0, D, rt_body, jnp.zeros((D, LANES), F32))
    gate = rg_ref[...]
    ort_ref[...] = rms0(ort) * (gate * _sigmoid(gate))


def _decode_rec(scalars, pt, qkvt, cos_t, sin_t, c0, n0, m0, sg0, sr0):
    D = HEAD_DIM
    S = jax.ShapeDtypeStruct
    prow = lambda col, blk: pl.BlockSpec((D, LANES), lambda h: ((col + blk * GROUP_W) // D + h, 0))
    vec = lambda blk: pl.BlockSpec((D, LANES), lambda h: (blk * N_HEADS + h, 0))
    st4 = pl.BlockSpec((1, D, D, LANES), lambda h: (h, 0, 0, 0))
    st3 = pl.BlockSpec((1, D, LANES), lambda h: (h, 0, 0))
    st2 = pl.BlockSpec((1, 1, LANES), lambda h: (h, 0, 0))
    in_specs = ([pl.BlockSpec(memory_space=pltpu.SMEM)]
                + [prow(COL_ML, blk) for blk in range(4)]
                + [vec(0), vec(1), vec(2), prow(COL_GD, 3)]
                + [prow(COL_RT, blk) for blk in range(4)]
                + [pl.BlockSpec((LANES, LANES), lambda h: (COL_GATES // LANES, 0)),
                   _const_spec((D, LANES)), _const_spec((D, LANES)), st4, st3, st2, st4, st4])
    return pl.pallas_call(
        _decode_rec_kernel,
        grid=(N_HEADS,),
        in_specs=in_specs,
        out_specs=[vec(0), vec(0), vec(0), st4, st3, st2, st4, st4],
        out_shape=[S((GROUP_W, LANES), F32)] * 3
        + [S((N_HEADS, D, D, LANES), F32), S((N_HEADS, D, LANES), F32), S((N_HEADS, 1, LANES), F32),
           S((N_HEADS, D, D, LANES), F32), S((N_HEADS, D, D, LANES), F32)],
        scratch_shapes=[pltpu.VMEM((D, LANES), F32), pltpu.VMEM((D, LANES), F32)],
        compiler_params=_cparams(("parallel",)),
        name="decode_rec",
    )(scalars, pt, pt, pt, pt, qkvt, qkvt, qkvt, pt, pt, pt, pt, pt, pt, cos_t, sin_t, c0, n0, m0, sg0, sr0)


def _outproj_kernel(x_ref, osb_ref, oml_ref, ogd_ref, ort_ref, gain_ref, wo_ref, nw_ref, rwh_ref, rwl_ref,
                    x1_ref, h2_ref, lg_ref, *, transposed):
    parts = [osb_ref[...]]
    for ref in (oml_ref, ogd_ref, ort_ref):
        parts.append(ref[...].T if transposed else ref[...])
    y = None
    for g, p in enumerate(parts):
        cols = slice(g * GROUP_W, (g + 1) * GROUP_W)
        t = _dot(p * gain_ref[:, cols], wo_ref[cols, :])
        y = t if y is None else y + t
    x1 = x_ref[...] + y
    h2 = _norm_rows(x1, nw_ref[...])
    x1_ref[...] = x1
    h2_ref[...] = h2.astype(BF16)
    lg_ref[...] = _gates_dot(h2, rwh_ref, rwl_ref)


def _outproj(x, osb, oml, ogd, ort, gain, wo, nw, rwh, rwl, transposed):
    n = x.shape[0]
    tm = min(TM_OUTPROJ, n)
    S = jax.ShapeDtypeStruct
    row = lambda c: pl.BlockSpec((tm, c), lambda i: (i, 0))
    mix = _const_spec((GROUP_W, DEC_BATCH)) if transposed else row(GROUP_W)
    return pl.pallas_call(
        functools.partial(_outproj_kernel, transposed=transposed),
        grid=(n // tm,),
        in_specs=[row(D_MODEL), row(GROUP_W), mix, mix, mix, _const_spec((1, D_MODEL)),
                  _const_spec((D_MODEL, D_MODEL)), _const_spec((1, D_MODEL)),
                  _const_spec((D_MODEL, LANES)), _const_spec((D_MODEL, LANES))],
        out_specs=[row(D_MODEL), row(D_MODEL), row(LANES)],
        out_shape=[S((n, D_MODEL), F32), S((n, D_MODEL), BF16), S((n, LANES), F32)],
        compiler_params=_cparams(("parallel",)),
        name="outproj",
    )(x, osb, oml, ogd, ort, gain, wo, nw, rwh, rwl)


def _route_kernel(lg_ref, rb_ref, tri_ref, lt_ref, cin_ref, pr_ref, rf_ref, ti_ref, cnt_ref):
    @pl.when(pl.program_id(0) == 0)
    def _():
        cnt_ref[...] = cin_ref[...]

    lg = lg_ref[...] + rb_ref[...]
    lane = lax.broadcasted_iota(I32, lg.shape, 1)
    ninf = -jnp.inf
    big = LANES - 1
    rmax = lambda x: jnp.max(x, axis=1, keepdims=True)
    rmin = lambda x: jnp.min(x, axis=1, keepdims=True)
    gl = jnp.where(lane < N_GROUPS, lg, ninf)
    gmax = rmax(gl)
    g_sel = rmin(jnp.where(gl == gmax, lane, big))
    g_prob = 1.0 / jnp.sum(jnp.exp(gl - gmax), axis=1, keepdims=True)
    e_lane = (lane >= N_GROUPS) & (lane < N_GROUPS + N_EXPERTS)
    em = jnp.where(e_lane & (jnp.right_shift(lane - N_GROUPS, 3) == g_sel), lg, ninf)
    v1 = rmax(em)
    i1 = rmin(jnp.where(em == v1, lane, big))
    em2 = jnp.where(lane == i1, ninf, em)
    v2 = rmax(em2)
    i2 = rmin(jnp.where(em2 == v2, lane, big))
    t = jnp.exp(v2 - v1)
    gate1 = g_prob / (1.0 + t)
    gate2 = g_prob * t / (1.0 + t)
    e1 = i1 - N_GROUPS
    e2 = i2 - N_GROUPS
    onehot = jnp.where((lane == e1) | (lane == e2), 1.0, 0.0)
    before = _dot(tri_ref[...], onehot)
    tile_cnt = jnp.sum(onehot, axis=0, keepdims=True)
    padded = jnp.floor((tile_cnt + (RUN_ALIGN - 1)) * (1.0 / RUN_ALIGN)) * RUN_ALIGN
    run_start = _dot(jnp.broadcast_to(padded, (8, LANES)), lt_ref[...])[0:1, :]
    pos = before + run_start
    p1 = jnp.sum(jnp.where(lane == e1, pos, 0.0), axis=1, keepdims=True)
    p2 = jnp.sum(jnp.where(lane == e2, pos, 0.0), axis=1, keepdims=True)
    pr = jnp.where(lane == 0, p1, jnp.where(lane == 1, p2, 0.0)).astype(I32)
    pr_ref[...] = pr.T[0:8, :]
    rf_ref[...] = jnp.where(lane == 0, gate1, jnp.where(lane == 1, gate2, jnp.where(lane == 2, p1,
                            jnp.where(lane == 3, p2, 0.0))))
    row8 = lax.broadcasted_iota(I32, (8, LANES), 0)
    ti_ref[...] = jnp.where(row8 == 0, padded, jnp.where(row8 == 1, cnt_ref[...],
                            jnp.where(row8 == 2, run_start, 0.0))).astype(I32)
    cnt_ref[...] += padded


def _route(logits, rb, tri, lanes_lt, cnt_in):
    n = logits.shape[0]
    tm = tri.shape[0]
    S = jax.ShapeDtypeStruct
    row = pl.BlockSpec((tm, LANES), lambda i: (i, 0))
    return pl.pallas_call(
        _route_kernel,
        grid=(n // tm,),
        in_specs=[row, _const_spec((1, LANES)), _const_spec((tm, tm)), _const_spec((LANES, LANES)),
                  _const_spec((1, LANES))],
        out_specs=[pl.BlockSpec((8, tm), lambda i: (0, i)), row, pl.BlockSpec((8, LANES), lambda i: (i, 0)),
                   _const_spec((1, LANES))],
        out_shape=[S((8, n), I32), S((n, LANES), F32), S((8 * (n // tm), LANES), I32), S((1, LANES), F32)],
        compiler_params=_cparams(("arbitrary",)),
        name="route",
    )(logits, rb, tri, lanes_lt, cnt_in)


def _packed_rows(tm):
    return 2 * tm + N_EXPERTS * RUN_ALIGN


def _for_each_run_chunk(plan_refs, tile, fn):
    offs_ref, len_ref, before_ref, start_ref = plan_refs
    base = tile * N_EXPERTS

    def per_expert(e, _):
        n_rows = len_ref[base + e]
        n_big = n_rows // RUN_CHUNK
        packed0 = start_ref[base + e]
        slot0 = offs_ref[e] + before_ref[base + e]

        def chunks(first, count, rows):
            def per_chunk(c, _):
                off = first + c * rows
                fn(pl.multiple_of(packed0 + off, RUN_ALIGN), pl.multiple_of(slot0 + off, RUN_ALIGN), rows)
                return 0

            lax.fori_loop(0, count, per_chunk, 0)

        chunks(0, n_big, RUN_CHUNK)
        chunks(n_big * RUN_CHUNK, (n_rows - n_big * RUN_CHUNK) // RUN_ALIGN, RUN_ALIGN)
        return 0

    lax.fori_loop(0, N_EXPERTS, per_expert, 0)


def _dispatch_kernel(offs_ref, len_ref, before_ref, start_ref, h_ref, pr_ref, xs_in_ref, xs_ref, pk_s, sem):
    del xs_in_ref
    plan_refs = (offs_ref, len_ref, before_ref, start_ref)
    tm = h_ref.shape[0]
    i = pl.program_id(0)
    slot = lax.broadcasted_iota(I32, (_packed_rows(tm), tm), 0)
    sel = jnp.where((slot == pr_ref[0:1, :]) | (slot == pr_ref[1:2, :]), 1.0, 0.0)
    pk_s[...] = _dot(sel, h_ref[...]).astype(BF16)

    def copy(packed_row, slot_row, rows):
        return pltpu.make_async_copy(pk_s.at[pl.ds(packed_row, rows), :],
                                     xs_ref.at[pl.ds(slot_row, rows), :], sem.at[0])

    _for_each_run_chunk(plan_refs, i, lambda p, s, n: copy(p, s, n).start())
    _for_each_run_chunk(plan_refs, i, lambda p, s, n: copy(p, s, n).wait())


def _dispatch(plan, h2, pr, xs):
    n = h2.shape[0]
    tm = min(TM_ROUTE, n)
    return pl.pallas_call(
        _dispatch_kernel,
        grid_spec=pltpu.PrefetchScalarGridSpec(
            num_scalar_prefetch=4,
            grid=(n // tm,),
            in_specs=[pl.BlockSpec((tm, D_MODEL), lambda i, *_: (i, 0)),
                      pl.BlockSpec((8, tm), lambda i, *_: (0, i)),
                      pl.BlockSpec(memory_space=pl.ANY)],
            out_specs=pl.BlockSpec(memory_space=pl.ANY),
            scratch_shapes=[pltpu.VMEM((_packed_rows(tm), D_MODEL), BF16), pltpu.SemaphoreType.DMA((1,))]),
        out_shape=jax.ShapeDtypeStruct(xs.shape, xs.dtype),
        input_output_aliases={6: 0},
        compiler_params=_cparams(("arbitrary",)),
        name="dispatch",
    )(*plan, h2, pr, xs)


def _experts_kernel(te_ref, ts_ref, xs_ref, wg_ref, wu_ref, wd_ref, ys_ref, wg_s, wu_s, wd_s):
    i = pl.program_id(0)
    valid = ts_ref[i] == i
    fresh = (i == 0) | (te_ref[i] != te_ref[jnp.maximum(i - 1, 0)])

    @pl.when(valid & fresh)
    def _():
        wg_s[...] = wg_ref[0].astype(BF16)
        wu_s[...] = wu_ref[0].astype(BF16)
        wd_s[...] = wd_ref[0].astype(BF16)

    @pl.when(valid)
    def _():
        x = xs_ref[...].astype(BF16)
        a = _dot(x, wg_s[...])
        u = _dot(x, wu_s[...])
        act = a * _sigmoid(a) * u
        ys_ref[...] = _dot(act, wd_s[...]).astype(BF16)

    @pl.when(jnp.logical_not(valid))
    def _():
        ys_ref[...] = jnp.zeros_like(ys_ref)


def _experts(tile_expert, tile_src, xs, wg, wu, wd, layer):
    tm = TM_EXPERT
    wspec = lambda a, b: pl.BlockSpec((None, 1, a, b), lambda i, te, tv: (layer, te[i], 0, 0))
    return pl.pallas_call(
        _experts_kernel,
        grid_spec=pltpu.PrefetchScalarGridSpec(
            num_scalar_prefetch=2,
            grid=(N_TILES,),
            in_specs=[pl.BlockSpec((tm, D_MODEL), lambda i, te, ts: (ts[i], 0)),
                      wspec(D_MODEL, D_EXPERT), wspec(D_MODEL, D_EXPERT), wspec(D_EXPERT, D_MODEL)],
            out_specs=pl.BlockSpec((tm, D_MODEL), lambda i, te, tv: (i, 0)),
            scratch_shapes=[pltpu.VMEM((D_MODEL, D_EXPERT), BF16), pltpu.VMEM((D_MODEL, D_EXPERT), BF16),
                            pltpu.VMEM((D_EXPERT, D_MODEL), BF16)]),
        out_shape=jax.ShapeDtypeStruct((N_SLOTS, D_MODEL), BF16),
        compiler_params=_cparams(("arbitrary",)),
        name="experts",
    )(tile_expert, tile_src, xs, wg, wu, wd)


def _combine_kernel(offs_ref, len_ref, before_ref, start_ref, x1_ref, rf_ref, fw_ref, ys_ref, out_ref,
                    pk_s, sem, *, final):
    plan_refs = (offs_ref, len_ref, before_ref, start_ref)
    tm = x1_ref.shape[0]
    i = pl.program_id(0)
    buf = i % 2

    def copy(b, packed_row, slot_row, rows):
        return pltpu.make_async_copy(ys_ref.at[pl.ds(slot_row, rows), :],
                                     pk_s.at[b, pl.ds(packed_row, rows), :], sem.at[b])

    @pl.when(i == 0)
    def _():
        pk_s[...] = jnp.zeros_like(pk_s)
        _for_each_run_chunk(plan_refs, i, lambda p, s, n: copy(buf, p, s, n).start())

    @pl.when(i + 1 < pl.num_programs(0))
    def _():
        _for_each_run_chunk(plan_refs, i + 1, lambda p, s, n: copy(1 - buf, p, s, n).start())

    _for_each_run_chunk(plan_refs, i, lambda p, s, n: copy(buf, p, s, n).wait())
    slot = lax.broadcasted_iota(I32, (tm, _packed_rows(tm)), 1)
    packed = pk_s[buf]
    x2 = x1_ref[...]
    for k in range(2):
        sel = jnp.where(slot == rf_ref[:, 2 + k:3 + k].astype(I32), 1.0, 0.0)
        x2 = x2 + rf_ref[:, k:k + 1] * _dot(sel, packed)
    out_ref[...] = _norm_rows(x2, fw_ref[...]) if final else x2


def _combine(plan, x1, rf, fw, ys, final):
    n = x1.shape[0]
    tm = min(TM_ROUTE, n)
    row = lambda c: pl.BlockSpec((tm, c), lambda i, *_: (i, 0))
    return pl.pallas_call(
        functools.partial(_combine_kernel, final=final),
        grid_spec=pltpu.PrefetchScalarGridSpec(
            num_scalar_prefetch=4,
            grid=(n // tm,),
            in_specs=[row(D_MODEL), row(LANES), pl.BlockSpec((1, D_MODEL), lambda i, *_: (0, 0)),
                      pl.BlockSpec(memory_space=pl.ANY)],
            out_specs=row(D_MODEL),
            scratch_shapes=[pltpu.VMEM((2, _packed_rows(tm), D_MODEL), BF16), pltpu.SemaphoreType.DMA((2,))]),
        out_shape=jax.ShapeDtypeStruct((n, D_MODEL), F32),
        compiler_params=_cparams(("arbitrary",)),
        name="combine",
    )(*plan, x1, rf, fw, ys)


def _tri(n, kind, chunk=1):
    r = np.arange(n)[:, None]
    c = np.arange(n)[None, :]
    m = {"chunk_lower_incl": (c <= r) & (r // chunk == c // chunk),
         "lower_strict": c < r,
         "row_gt_col": r > c,
         "row_lt_col": r < c,
         "later_page": (r % 8 == c % 8) & (c // 8 > r // 8)}[kind]
    return jnp.asarray(m.astype(np.float32), dtype=BF16)


def _rope_tables(pos):
    half = HEAD_DIM // 2
    inv = ROPE_BASE ** (-np.arange(half, dtype=np.float64) / half)
    ang = np.asarray(pos, np.float64)[:, None] * inv[None, :]
    cos = np.concatenate([np.cos(ang), np.cos(ang)], axis=1)
    sin = np.concatenate([-np.sin(ang), np.sin(ang)], axis=1)
    return cos.astype(np.float32), sin.astype(np.float32)


def _retention_consts(L):
    log_gamma = np.log1p(-np.exp2(-5.0 - np.arange(N_HEADS, dtype=np.float64)))
    idx = np.arange(L, dtype=np.float64)
    diff = np.maximum(idx[:, None] - idx[None, :], 0.0)
    dmat = np.where(idx[None, :] <= idx[:, None], np.exp(log_gamma[:, None, None] * diff), 0.0)
    xi = np.exp(log_gamma[:, None] * (idx + 1.0))
    zeta = np.exp(log_gamma[:, None] * (L - 1.0 - idx))
    xz = np.zeros((L, LANES), np.float64)
    xz[:, 0:N_HEADS] = xi.T
    xz[:, N_HEADS:2 * N_HEADS] = zeta.T
    return (jnp.asarray(dmat, F32), jnp.asarray(xz, F32), jnp.asarray(np.exp(log_gamma * L), F32),
            np.exp(log_gamma))


def _hi_lo(w):
    hi = w.astype(BF16)
    return hi, (w - hi.astype(F32)).astype(BF16)


def _pad_lanes(w):
    return jnp.pad(w, ((0, 0), (0, LANES - w.shape[1])))


def _prep_w_in(w):
    ml0 = 3 * GROUP_W
    mlg = ml0 + 4 * GROUP_W
    gd0 = mlg + 2 * N_HEADS
    gdg = gd0 + 4 * GROUP_W
    rt0 = gdg + 2 * N_HEADS
    main = jnp.concatenate([w[:, 0:ml0], w[:, ml0:mlg], w[:, gd0:gdg], w[:, rt0:]], axis=1).astype(BF16)
    gates = _pad_lanes(jnp.concatenate([w[:, mlg:gd0], w[:, gdg:rt0]], axis=1))
    return (main,) + _hi_lo(gates)


def _tile_plan(tile_info):
    t = tile_info.reshape(-1, 8, LANES)[:, :, :N_EXPERTS]
    return t[:, 0].reshape(-1), t[:, 1].reshape(-1), t[:, 2].reshape(-1)


def _dispatch_plan(cnt):
    ntile = (cnt + TM_EXPERT - 1) // TM_EXPERT
    tile_end = jnp.cumsum(ntile)
    offs = (tile_end - ntile) * TM_EXPERT
    tid = jnp.arange(N_TILES, dtype=I32)
    te = jnp.minimum(jnp.sum(tid[:, None] >= tile_end[None, :], axis=1), N_EXPERTS - 1).astype(I32)
    tile_src = jnp.minimum(tid, tile_end[-1] - 1).astype(I32)
    return offs.astype(I32), te, tile_src


def kernel(x_prompt, x_sample, cache_sb_k, cache_sb_v, state_mlstm_c, state_mlstm_n, state_mlstm_m, state_gdn_s, state_gdn_conv, state_ret_s, page_table, norm_attn_w, w_in, sb_logit_bias, mlstm_gate_bias, gdn_conv_w, gdn_a_log, gdn_dt_bias, head_norm_w, w_out, norm_ffn_w, router_group_w, router_group_b, router_expert_w, router_expert_b, expert_w_gate, expert_w_up, expert_w_down, final_norm_w):
    D = HEAD_DIM
    xp = x_prompt.reshape(N_PROMPT, D_MODEL)
    xd = x_sample.reshape(DEC_BATCH, D_MODEL)
    n_pool = cache_sb_k.shape[1]
    cache_kt = cache_sb_k.transpose(0, 1, 3, 4, 2).reshape(DEPTH, n_pool, GROUP_W, PAGE_SIZE)
    cache_vt = cache_sb_v.transpose(0, 1, 3, 4, 2).reshape(DEPTH, n_pool, GROUP_W, PAGE_SIZE)

    tril_ml = _tri(MLSTM_STEP, "chunk_lower_incl", CHUNK)
    tril_gd = _tri(GDN_STEP, "chunk_lower_incl", GDN_CHUNK)
    tri_sb = _tri(SB_BLOCK, "row_gt_col")
    tri_page = _tri(PAGE_SIZE, "row_gt_col")
    later_pages = _tri(8 * N_PAGES, "later_page")
    lanes_lt = _tri(LANES, "row_lt_col")
    tri_rp = _tri(TM_ROUTE, "lower_strict")
    tri_rd = _tri(DEC_BATCH, "lower_strict")
    cos_p, sin_p = _rope_tables(np.arange(SEQ))
    cos_p = jnp.asarray(np.tile(cos_p, (1, N_HEADS)))
    sin_p = jnp.asarray(np.tile(sin_p, (1, N_HEADS)))
    cos_d, sin_d = _rope_tables([PAST_LEN])
    cos_d = jnp.asarray(np.tile(cos_d.T, (1, LANES)))
    sin_d = jnp.asarray(np.tile(sin_d.T, (1, LANES)))
    dmat, xz, gch, gamma = _retention_consts(CHUNK)
    zeros4 = jnp.zeros((N_HEADS,), F32)
    xs = None

    outs = {k: [] for k in ("kp", "vp", "ks", "vs", "cp", "np", "mp", "cs", "ns", "ms", "gp", "gcp", "gs", "gcs",
                            "rp", "rs")}
    yp = yd = None
    for l in range(DEPTH):
        w_main, wg_hi, wg_lo = _prep_w_in(w_in[l])
        nw = norm_attn_w[l][None, :]
        gate_bias_row = _pad_lanes(jnp.concatenate([mlstm_gate_bias[l], zeros4, gdn_dt_bias[l]])[None, :])
        a_log_row = _pad_lanes(jnp.concatenate([zeros4, zeros4, zeros4, gdn_a_log[l]])[None, :])
        bias = sb_logit_bias[l]

        sbq, sbk, sbv, ml, gd, rt, gt = _inproj_prompt(xp, nw, w_main, wg_hi, wg_lo)
        osb_p = _sb_prompt(sbq, sbk, sbv, bias, tri_sb)
        oml_p, c1p, n1p, m1p = _mlstm_prompt(ml, gt, gate_bias_row, tril_ml)
        ogd_p, s1p, cv1p = _gdn_prompt(gd, gt, gate_bias_row, a_log_row, gdn_conv_w[l], tril_gd)
        ort_p, r1p = _ret_prompt(rt, cos_p, sin_p, dmat, xz, gch)

        sbq_d, gdraw_d, pt = _inproj_decode(xd, nw, w_main, wg_hi, wg_lo)
        osb_d = _sb_decode(sbq_d, cache_kt, cache_vt, page_table, bias, tri_page, later_pages, l)
        qkvt, cv1d = _gdn_decode_conv(gdraw_d, state_gdn_conv[l].transpose(1, 0, 2), gdn_conv_w[l])
        scalars = jnp.stack([mlstm_gate_bias[l][:N_HEADS], mlstm_gate_bias[l][N_HEADS:], gdn_dt_bias[l],
                             jnp.exp(gdn_a_log[l]), jnp.asarray(gamma, F32), zeros4, zeros4, zeros4])
        oml_d, ogd_d, ort_d, c1d, n1d, m1d, s1d, r1d = _decode_rec(
            scalars, pt, qkvt, cos_d, sin_d,
            state_mlstm_c[l].transpose(1, 2, 3, 0), state_mlstm_n[l].transpose(1, 2, 0),
            state_mlstm_m[l].T.reshape(N_HEADS, 1, DEC_BATCH),
            state_gdn_s[l].transpose(1, 2, 3, 0), state_ret_s[l].transpose(1, 2, 3, 0))

        gain = head_norm_w[l][None, :]
        wo = w_out[l].astype(BF16)
        nfw = norm_ffn_w[l][None, :]
        rw_hi, rw_lo = _hi_lo(_pad_lanes(jnp.concatenate([router_group_w[l], router_expert_w[l]], axis=1)))
        rb = _pad_lanes(jnp.concatenate([router_group_b[l], router_expert_b[l]])[None, :])
        x1p, h2p, lgp = _outproj(xp, osb_p, oml_p, ogd_p, ort_p, gain, wo, nfw, rw_hi, rw_lo, False)
        x1d, h2d, lgd = _outproj(xd, osb_d, oml_d, ogd_d, ort_d, gain, wo, nfw, rw_hi, rw_lo, True)
        prp, rfp, tip, cnt_p = _route(lgp, rb, tri_rp, lanes_lt, jnp.zeros((1, LANES), F32))
        prd, rfd, tid, cnt = _route(lgd, rb, tri_rd, lanes_lt, cnt_p)

        offs, te, tile_src = _dispatch_plan(cnt[0, :N_EXPERTS].astype(I32))
        plan_p = (offs,) + _tile_plan(tip)
        plan_d = (offs,) + _tile_plan(tid)
        xs = _dispatch(plan_p, h2p, prp, jnp.zeros((N_SLOTS, D_MODEL), BF16) if xs is None else xs)
        xs = _dispatch(plan_d, h2d, prd, xs)
        ys = _experts(te, tile_src, xs, expert_w_gate, expert_w_up, expert_w_down, l)
        final = l == DEPTH - 1
        fw = final_norm_w[None, :]
        xp_next = _combine(plan_p, x1p, rfp, fw, ys, final)
        xd_next = _combine(plan_d, x1d, rfd, fw, ys, final)
        if final:
            yp, yd = xp_next, xd_next
        else:
            xp, xd = xp_next, xd_next

        heads_p = lambda a: a.reshape(BATCH, SEQ, N_HEADS, D)
        heads_t = lambda a: a.reshape(N_HEADS, D, DEC_BATCH).transpose(2, 0, 1)[:, None]
        outs["kp"].append(heads_p(sbk))
        outs["vp"].append(heads_p(sbv))
        outs["ks"].append(heads_t(pt[GROUP_W:2 * GROUP_W]))
        outs["vs"].append(heads_t(pt[2 * GROUP_W:3 * GROUP_W]))
        outs["cp"].append(c1p)
        outs["np"].append(n1p)
        outs["mp"].append(m1p[:, 0, :N_HEADS])
        outs["cs"].append(c1d.transpose(3, 0, 1, 2))
        outs["ns"].append(n1d.transpose(2, 0, 1))
        outs["ms"].append(m1d[:, 0, :].T)
        outs["gp"].append(s1p)
        outs["gcp"].append(cv1p)
        outs["gs"].append(s1d.transpose(3, 0, 1, 2))
        outs["gcs"].append(cv1d.transpose(1, 0, 2))
        outs["rp"].append(r1p)
        outs["rs"].append(r1d.transpose(3, 0, 1, 2))

    st = lambda k: jnp.stack(outs[k], axis=0)
    return (yp.reshape(BATCH, SEQ, D_MODEL), yd.reshape(DEC_BATCH, 1, D_MODEL),
            st("kp"), st("vp"), st("ks"), st("vs"),
            st("cp"), st("np"), st("mp"), st("cs"), st("ns"), st("ms"),
            st("gp"), st("gcp"), st("gs"), st("gcs"), st("rp"), st("rs"))
```

```python
import functools
import math

import numpy as np
import jax
import jax.numpy as jnp
from jax import lax
from jax.experimental import pallas as pl
from jax.experimental.pallas import tpu as pltpu

F32 = jnp.float32
BF16 = jnp.bfloat16
I32 = jnp.int32

D_MODEL = 1024
BATCH = 8
SEQ = 2048
DEPTH = 2
DEC_BATCH = 128
PAST_LEN = 2048
PAGE_SIZE = 128
N_PAGES = PAST_LEN // PAGE_SIZE
HEAD_DIM = 64
N_HEADS = 4
GROUP_W = N_HEADS * HEAD_DIM
CHUNK = 128
GDN_CHUNK = 64
CONV_W = 4
N_GROUPS = 4
EXPERTS_PER_GROUP = 8
N_EXPERTS = N_GROUPS * EXPERTS_PER_GROUP
D_EXPERT = D_MODEL // 2
ROPE_BASE = 10000.0
EPS = 1e-6
SCALE = HEAD_DIM ** -0.5

N_PROMPT = BATCH * SEQ
LANES = 128
N_MAIN = 3 * GROUP_W + 3 * 4 * GROUP_W
N_PROJ = N_MAIN + LANES
COL_ML = 3 * GROUP_W
COL_GD = COL_ML + 4 * GROUP_W
COL_RT = COL_GD + 4 * GROUP_W
COL_GATES = N_MAIN

MLSTM_STEP = 2 * CHUNK
GDN_STEP = 4 * GDN_CHUNK
RET_STEP = 4 * CHUNK
SB_BLOCK = 256
SB_DEC_SEQS = 4
TM_PROMPT = 512
TM_OUTPROJ = 256
TM_EXPERT = 512
N_ASSIGN = 2 * (N_PROMPT + DEC_BATCH)
TM_ROUTE = 512
RUN_ALIGN = 16
RUN_CHUNK = 2 * RUN_ALIGN
N_TOKEN_TILES = N_PROMPT // TM_ROUTE + 1
N_TILES = -(-(N_ASSIGN + N_TOKEN_TILES * N_EXPERTS * (RUN_ALIGN - 1)) // TM_EXPERT) + N_EXPERTS
N_SLOTS = N_TILES * TM_EXPERT
VMEM_LIMIT = 48 * 1024 * 1024

_NT = (((1,), (1,)), ((), ()))
_TN = (((0,), (0,)), ((), ()))


_NN = (((1,), (0,)), ((), ()))


def _mm(a, b, dims):
    return lax.dot_general(a.astype(BF16), b.astype(BF16), dims, preferred_element_type=F32)


def _dot(a, b):
    return _mm(a, b, _NN)


def _dot_nt(a, b):
    return _mm(a, b, _NT)


def _dot_tn(a, b):
    return _mm(a, b, _TN)


def _split3(x):
    x1 = x.astype(BF16)
    r1 = x - x1.astype(F32)
    x2 = r1.astype(BF16)
    x3 = (r1 - x2.astype(F32)).astype(BF16)
    return x1, x2, x3


def _tri_dot(tri, x):
    x1, x2, x3 = _split3(x)
    return _dot(tri, x1) + _dot(tri, x2) + _dot(tri, x3)


def _log_sigmoid(z):
    return jnp.minimum(z, 0.0) - jnp.log(1.0 + jnp.exp(-jnp.abs(z)))


def _softplus(z):
    return jnp.maximum(z, 0.0) + jnp.log(1.0 + jnp.exp(-jnp.abs(z)))


def _sigmoid(z):
    return 1.0 / (1.0 + jnp.exp(-z))


def _head_rms(x):
    return x * lax.rsqrt(jnp.mean(x * x, axis=-1, keepdims=True) + EPS)


def _cparams(sem):
    return pltpu.CompilerParams(dimension_semantics=sem, vmem_limit_bytes=VMEM_LIMIT)


def _const_spec(shape):
    nd = len(shape)
    return pl.BlockSpec(shape, lambda *_: (0,) * nd)


def _norm_rows(x, w):
    return x * lax.rsqrt(jnp.mean(x * x, axis=-1, keepdims=True) + EPS) * w


def _gates_dot(h, wgh_ref, wgl_ref):
    hb = h.astype(BF16)
    hl = (h - hb.astype(F32)).astype(BF16)
    return _dot(hb, wgh_ref[...]) + _dot(hl, wgh_ref[...]) + _dot(hb, wgl_ref[...])


def _inproj_prompt_kernel(x_ref, nw_ref, w_ref, wgh_ref, wgl_ref,
                          sbq_ref, sbk_ref, sbv_ref, ml_ref, gd_ref, rt_ref, gt_ref):
    h = _norm_rows(x_ref[...], nw_ref[...])
    hb = h.astype(BF16)
    seg = lambda a, b: _dot(hb, w_ref[:, a:b])
    sbq_ref[...] = seg(0, GROUP_W)
    sbk_ref[...] = seg(GROUP_W, 2 * GROUP_W)
    sbv_ref[...] = seg(2 * GROUP_W, 3 * GROUP_W)
    ml_ref[...] = seg(COL_ML, COL_GD)
    gd_ref[...] = seg(COL_GD, COL_RT)
    rt_ref[...] = seg(COL_RT, N_MAIN)
    gt_ref[...] = _gates_dot(h, wgh_ref, wgl_ref)


def _inproj_prompt(x, nw, w, wgh, wgl):
    n = x.shape[0]
    tm = TM_PROMPT
    row = lambda c: pl.BlockSpec((tm, c), lambda i: (i, 0))
    S = jax.ShapeDtypeStruct
    return pl.pallas_call(
        _inproj_prompt_kernel,
        grid=(n // tm,),
        in_specs=[row(D_MODEL), _const_spec((1, D_MODEL)), _const_spec((D_MODEL, N_MAIN)),
                  _const_spec((D_MODEL, LANES)), _const_spec((D_MODEL, LANES))],
        out_specs=[row(GROUP_W), row(GROUP_W), row(GROUP_W), row(4 * GROUP_W), row(4 * GROUP_W),
                   row(4 * GROUP_W), row(LANES)],
        out_shape=[S((n, GROUP_W), F32)] * 3 + [S((n, 4 * GROUP_W), F32)] * 3 + [S((n, LANES), F32)],
        compiler_params=_cparams(("parallel",)),
        name="inproj_prompt",
    )(x, nw, w, wgh, wgl)


def _inproj_decode_kernel(x_ref, nw_ref, w_ref, wgh_ref, wgl_ref, sbq_ref, gdraw_ref, pt_ref):
    h = _norm_rows(x_ref[...], nw_ref[...])
    hb = h.astype(BF16)
    for j in range(N_MAIN // LANES):
        p = _dot(hb, w_ref[:, j * LANES:(j + 1) * LANES])
        if j < GROUP_W // LANES:
            sbq_ref[:, j * LANES:(j + 1) * LANES] = p
        c0 = j * LANES - COL_GD
        if 0 <= c0 < 3 * GROUP_W:
            gdraw_ref[:, c0:c0 + LANES] = p
        pt_ref[j * LANES:(j + 1) * LANES, :] = p.T
    pt_ref[N_MAIN:N_PROJ, :] = _gates_dot(h, wgh_ref, wgl_ref).T


def _inproj_decode(x, nw, w, wgh, wgl):
    S = jax.ShapeDtypeStruct
    return pl.pallas_call(
        _inproj_decode_kernel,
        out_shape=[S((DEC_BATCH, GROUP_W), F32), S((DEC_BATCH, 3 * GROUP_W), F32), S((N_PROJ, DEC_BATCH), F32)],
        compiler_params=_cparams(None),
        name="inproj_decode",
    )(x, nw, w, wgh, wgl)


def _sb_prompt_kernel(bias_ref, q_ref, k_ref, v_ref, tri_ref, o_ref, acc_s, carry_s):
    tb = SB_BLOCK
    qi = pl.program_id(1)
    tri = tri_ref[...]
    r = lax.broadcasted_iota(I32, (tb, tb), 0)
    c = lax.broadcasted_iota(I32, (tb, tb), 1)
    dmask = c < r
    acc_s[...] = jnp.zeros_like(acc_s)
    carry_s[...] = jnp.zeros_like(carry_s)
    q = q_ref[...] * SCALE

    def block(j, mask):
        start = pl.multiple_of(j * tb, tb)
        kj = k_ref[pl.ds(start, tb), :]
        vj = v_ref[pl.ds(start, tb), :]
        heads = range(N_HEADS)
        sls = [slice(h * HEAD_DIM, (h + 1) * HEAD_DIM) for h in heads]
        z = [_dot_nt(q[:, sls[h]], kj[:, sls[h]]) + bias_ref[h] for h in heads]
        lsz = [_log_sigmoid(z[h]) for h in heads]
        lk = [lsz[h] - z[h] for h in heads]
        if mask is not None:
            lk = [jnp.where(mask, x, 0.0) for x in lk]
        lw_all = _dot(jnp.concatenate([x.astype(BF16) for x in lk], axis=0), tri)
        lw = [lw_all[h * tb:(h + 1) * tb, :] for h in heads]
        a = [jnp.exp(lsz[h] + lw[h] + carry_s[:, h:h + 1]) for h in heads]
        if mask is not None:
            a = [jnp.where(mask, x, 0.0) for x in a]
        av = [_dot(a[h], vj[:, sls[h]]) for h in heads]
        for h in heads:
            acc_s[:, sls[h]] += av[h]
            carry_s[:, h:h + 1] += lw[h][:, 0:1] + lk[h][:, 0:1]

    block(qi, dmask)

    def body(it, _):
        block(qi - 1 - it, None)
        return 0

    lax.fori_loop(0, qi, body, 0)
    for h in range(N_HEADS):
        sl = slice(h * HEAD_DIM, (h + 1) * HEAD_DIM)
        o_ref[:, sl] = _head_rms(acc_s[:, sl])


def _sb_prompt(q, k, v, bias, tri):
    tb = SB_BLOCK
    nq = SEQ // tb
    return pl.pallas_call(
        _sb_prompt_kernel,
        grid_spec=pltpu.PrefetchScalarGridSpec(
            num_scalar_prefetch=0,
            grid=(BATCH, nq),
            in_specs=[pl.BlockSpec(memory_space=pltpu.SMEM),
                      pl.BlockSpec((tb, GROUP_W), lambda b, i: (b * nq + i, 0)),
                      pl.BlockSpec((SEQ, GROUP_W), lambda b, i: (b, 0)),
                      pl.BlockSpec((SEQ, GROUP_W), lambda b, i: (b, 0)),
                      _const_spec((tb, tb))],
            out_specs=pl.BlockSpec((tb, GROUP_W), lambda b, i: (b * nq + i, 0)),
            scratch_shapes=[pltpu.VMEM((tb, GROUP_W), F32), pltpu.VMEM((tb, LANES), F32)]),
        out_shape=jax.ShapeDtypeStruct((N_PROMPT, GROUP_W), F32),
        compiler_params=_cparams(("parallel", "parallel")),
        name="sb_prompt",
    )(bias, q, k, v, tri)


def _sb_decode_kernel(pt_ref, bias_ref, q_ref, *rest):
    S = SB_DEC_SEQS
    n_in = S * N_PAGES
    k_refs = rest[:n_in]
    v_refs = rest[n_in:2 * n_in]
    tri_ref, pg_ref, o_ref = rest[2 * n_in:]
    n_rows = 8 * N_PAGES
    r0 = (pl.program_id(0) * S) % 8
    seqs = range(S)
    rowi = lax.broadcasted_iota(I32, (8, GROUP_W), 0)
    lanei = lax.broadcasted_iota(I32, (8, GROUP_W), 1)
    head_of_lane = jnp.right_shift(lanei, 6)
    row8 = lax.broadcasted_iota(I32, (8, 1), 0)
    bias = jnp.zeros((8, 1), F32)
    for h in range(N_HEADS):
        bias = jnp.where(row8 == h, bias_ref[h], bias)
    qbd = [jnp.where(head_of_lane == rowi, q_ref[pl.ds(r0 + s, 1), :] * SCALE, 0.0).astype(BF16) for s in seqs]
    z = [jnp.concatenate([_dot(qbd[s], k_refs[s * N_PAGES + j][0, 0]) + bias for j in range(N_PAGES)], axis=0)
         for s in seqs]
    lsz = [_log_sigmoid(z[s]) for s in seqs]
    lk = [lsz[s] - z[s] for s in seqs]
    lw = [_dot(lk[s], tri_ref[...]) for s in seqs]
    tot = [jnp.broadcast_to(lw[s][:, 0:1] + lk[s][:, 0:1], (n_rows, LANES)) for s in seqs]
    later_pages = [_tri_dot(pg_ref[...], tot[s]) for s in seqs]
    a = [jnp.exp(lsz[s] + lw[s] + later_pages[s]) for s in seqs]
    for s in seqs:
        acc = jnp.zeros((8, GROUP_W), F32)
        for j in range(N_PAGES):
            acc = acc + _dot_nt(a[s][j * 8:(j + 1) * 8, :], v_refs[s * N_PAGES + j][0, 0])
        orow = jnp.sum(jnp.where(head_of_lane == rowi, acc, 0.0), axis=0, keepdims=True)
        pieces = [_head_rms(orow[:, h * HEAD_DIM:(h + 1) * HEAD_DIM]) for h in range(N_HEADS)]
        o_ref[pl.ds(r0 + s, 1), :] = jnp.concatenate(pieces, axis=1)


def _sb_decode(q, cache_kt, cache_vt, page_table, bias, tri, later_pages, layer):
    S = SB_DEC_SEQS

    def page_spec(s, j):
        return pl.BlockSpec((1, 1, GROUP_W, PAGE_SIZE), lambda i, pt: (layer, pt[i * S + s, j], 0, 0))

    pages = [page_spec(s, j) for s in range(S) for j in range(N_PAGES)]
    in_specs = ([pl.BlockSpec(memory_space=pltpu.SMEM),
                 pl.BlockSpec((8, GROUP_W), lambda i, pt: (i * S // 8, 0))]
                + pages * 2
                + [pl.BlockSpec((PAGE_SIZE, PAGE_SIZE), lambda i, pt: (0, 0)),
                   pl.BlockSpec((8 * N_PAGES, 8 * N_PAGES), lambda i, pt: (0, 0))])
    return pl.pallas_call(
        _sb_decode_kernel,
        grid_spec=pltpu.PrefetchScalarGridSpec(
            num_scalar_prefetch=1,
            grid=(DEC_BATCH // S,),
            in_specs=in_specs,
            out_specs=pl.BlockSpec((8, GROUP_W), lambda i, pt: (i * S // 8, 0))),
        out_shape=jax.ShapeDtypeStruct((DEC_BATCH, GROUP_W), F32),
        compiler_params=_cparams(("arbitrary",)),
        name="sb_decode",
    )(page_table, bias, q, *([cache_kt] * (S * N_PAGES)), *([cache_vt] * (S * N_PAGES)), tri, later_pages)


def _chunk_masks(L):
    r = lax.broadcasted_iota(I32, (L, L), 0)
    c = lax.broadcasted_iota(I32, (L, L), 1)
    return c <= r, c < r


def _head_cols(blk, h):
    return slice(blk * GROUP_W + h * HEAD_DIM, blk * GROUP_W + (h + 1) * HEAD_DIM)


def _mlstm_prompt_kernel(x_ref, g_ref, gb_ref, tril_ref, o_ref, c1_ref, n1_ref, m1_ref, c_s, n_s, m_s):
    L = CHUNK
    R = x_ref.shape[0]
    ci = pl.program_id(1)

    @pl.when(ci == 0)
    def _():
        c_s[...] = jnp.zeros_like(c_s)
        n_s[...] = jnp.zeros_like(n_s)
        m_s[...] = jnp.zeros_like(m_s)

    g = g_ref[...] + gb_ref[...]
    lane = lax.broadcasted_iota(I32, (R, LANES), 1)
    lf = jnp.where((lane >= N_HEADS) & (lane < 2 * N_HEADS), _log_sigmoid(g), 0.0)
    cum = _tri_dot(tril_ref[...], lf)
    xt = jnp.where(lane < N_HEADS, g, cum).T
    causal, _ = _chunk_masks(L)
    P = [(s_i, h) for s_i in range(R // L) for h in range(N_HEADS)]
    rows = [slice(s_i * L, (s_i + 1) * L) for s_i, _ in P]
    hs = [h for _, h in P]
    np_ = range(len(P))
    bc = [cum[rows[p], N_HEADS + hs[p]:N_HEADS + hs[p] + 1] for p in np_]
    li = [g[rows[p], hs[p]:hs[p] + 1] for p in np_]
    log_d = [jnp.where(causal, bc[p] - xt[N_HEADS + hs[p]:N_HEADS + hs[p] + 1, rows[p]] + xt[hs[p]:hs[p] + 1, rows[p]],
                       -jnp.inf) for p in np_]
    b_last = [bc[p][L - 1:L, :] for p in np_]
    log_w = [b_last[p] - bc[p] + li[p] for p in np_]
    d_max = [jnp.max(log_d[p], axis=1, keepdims=True) for p in np_]
    w_max = [jnp.max(log_w[p], axis=0, keepdims=True) for p in np_]
    m = [m_s[0:1, h:h + 1] for h in range(N_HEADS)]
    m_in, m_out = [], []
    for p in np_:
        m_in.append(m[hs[p]])
        m[hs[p]] = jnp.maximum(b_last[p] + m[hs[p]], w_max[p])
        m_out.append(m[hs[p]])
    log_inter = [bc[p] + m_in[p] for p in np_]
    m_row = [jnp.maximum(log_inter[p], d_max[p]) for p in np_]
    dexp = [jnp.exp(log_d[p] - m_row[p]) for p in np_]
    w_inter = [jnp.exp(log_inter[p] - m_row[p]) for p in np_]
    decay = [jnp.exp(b_last[p] + m_in[p] - m_out[p]) for p in np_]
    q = [x_ref[rows[p], _head_cols(0, hs[p])] for p in np_]
    k = [x_ref[rows[p], _head_cols(1, hs[p])] * SCALE for p in np_]
    v = [x_ref[rows[p], _head_cols(2, hs[p])] for p in np_]
    kw = [k[p] * jnp.exp(log_w[p] - m_out[p]) for p in np_]
    s_mat = [_dot_nt(q[p], k[p]) * dexp[p] for p in np_]
    sv = [_dot(s_mat[p], v[p]) for p in np_]
    kv = [_dot_tn(kw[p], v[p]) for p in np_]
    kw_sum = [jnp.sum(kw[p], axis=0, keepdims=True) for p in np_]
    c = [c_s[h] for h in range(N_HEADS)]
    n = [n_s[h:h + 1, :] for h in range(N_HEADS)]
    c_in, n_in = [], []
    for p in np_:
        c_in.append(c[hs[p]])
        n_in.append(n[hs[p]])
        c[hs[p]] = decay[p] * c[hs[p]] + kv[p]
        n[hs[p]] = decay[p] * n[hs[p]] + kw_sum[p]
    qc = [_dot(q[p], c_in[p]) for p in np_]
    s_sum = [jnp.sum(s_mat[p], axis=1, keepdims=True) for p in np_]
    qn = [jnp.sum(q[p] * n_in[p], axis=1, keepdims=True) for p in np_]
    hh = [(sv[p] + w_inter[p] * qc[p]) / jnp.maximum(jnp.abs(s_sum[p] + w_inter[p] * qn[p]), jnp.exp(-m_row[p]))
          for p in np_]
    ms = [jnp.mean(hh[p] * hh[p], axis=1, keepdims=True) for p in np_]
    for p in np_:
        og = x_ref[rows[p], _head_cols(3, hs[p])]
        o_ref[rows[p], hs[p] * HEAD_DIM:(hs[p] + 1) * HEAD_DIM] = hh[p] * lax.rsqrt(ms[p] + EPS) * _sigmoid(og)
    for h in range(N_HEADS):
        c_s[h] = c[h]
        n_s[h:h + 1, :] = n[h]
        m_s[0:1, h:h + 1] = m[h]

    @pl.when(ci == pl.num_programs(1) - 1)
    def _():
        c1_ref[0] = c_s[...]
        n1_ref[0] = n_s[...]
        m1_ref[0] = m_s[...]


def _mlstm_prompt(ml, gates, gate_bias_row, tril):
    L = MLSTM_STEP
    nc = SEQ // L
    S = jax.ShapeDtypeStruct
    return pl.pallas_call(
        _mlstm_prompt_kernel,
        grid=(BATCH, nc),
        in_specs=[pl.BlockSpec((L, 4 * GROUP_W), lambda b, c: (b * nc + c, 0)),
                  pl.BlockSpec((L, LANES), lambda b, c: (b * nc + c, 0)),
                  _const_spec((1, LANES)), _const_spec((L, L))],
        out_specs=[pl.BlockSpec((L, GROUP_W), lambda b, c: (b * nc + c, 0)),
                   pl.BlockSpec((1, N_HEADS, HEAD_DIM, HEAD_DIM), lambda b, c: (b, 0, 0, 0)),
                   pl.BlockSpec((1, N_HEADS, HEAD_DIM), lambda b, c: (b, 0, 0)),
                   pl.BlockSpec((1, 1, LANES), lambda b, c: (b, 0, 0))],
        out_shape=[S((N_PROMPT, GROUP_W), F32), S((BATCH, N_HEADS, HEAD_DIM, HEAD_DIM), F32),
                   S((BATCH, N_HEADS, HEAD_DIM), F32), S((BATCH, 1, LANES), F32)],
        scratch_shapes=[pltpu.VMEM((N_HEADS, HEAD_DIM, HEAD_DIM), F32), pltpu.VMEM((N_HEADS, HEAD_DIM), F32),
                        pltpu.VMEM((1, LANES), F32)],
        compiler_params=_cparams(("parallel", "arbitrary")),
        name="mlstm_prompt",
    )(ml, gates, gate_bias_row, tril)


def _unit_lower_inverse(a, L):
    r = lax.broadcasted_iota(I32, (L, L), 0)
    c = lax.broadcasted_iota(I32, (L, L), 1)
    p = jnp.where(r == c, 1.0, 0.0) - a
    x = a
    power = 1
    while 2 * power < L:
        x = _dot(x, x)
        p = p + _dot(p, x)
        power *= 2
    return p


def _gdn_prompt_kernel(x_ref, g_ref, gb_ref, al_ref, cw_ref, tril_ref, o_ref, s1_ref, cv_ref, s_s, xe_s):
    L = GDN_CHUNK
    R = x_ref.shape[0]
    nq = 3 * GROUP_W
    ci = pl.program_id(1)

    @pl.when(ci == 0)
    def _():
        s_s[...] = jnp.zeros_like(s_s)
        xe_s[0:8, :] = jnp.zeros((8, nq), F32)

    raw = x_ref[:, 0:nq]
    xe_s[8:8 + R, :] = raw
    conv = (cw_ref[3:4, :] * raw + cw_ref[2:3, :] * xe_s[7:7 + R, :]
            + cw_ref[1:2, :] * xe_s[6:6 + R, :] + cw_ref[0:1, :] * xe_s[5:5 + R, :])
    xe_s[0:8, :] = raw[R - 8:R, :]
    qkv = conv * _sigmoid(conv)

    g = g_ref[...] + gb_ref[...]
    lane = lax.broadcasted_iota(I32, (R, LANES), 1)
    beta_all = _sigmoid(g)
    gd = jnp.where((lane >= 3 * N_HEADS) & (lane < 4 * N_HEADS), -jnp.exp(al_ref[...]) * _softplus(g), 0.0)
    gcum = _tri_dot(tril_ref[...], gd)
    xt = gcum.T
    incl, strict = _chunk_masks(L)
    prob = [(s_i, h) for s_i in range(R // L) for h in range(N_HEADS)]
    loc = []
    for s_i, h in prob:
        rows = slice(s_i * L, (s_i + 1) * L)
        q = qkv[rows, _head_cols(0, h)]
        k = qkv[rows, _head_cols(1, h)]
        v = qkv[rows, _head_cols(2, h)]
        q = q * lax.rsqrt(jnp.sum(q * q, axis=1, keepdims=True) + EPS) * SCALE
        k = k * lax.rsqrt(jnp.sum(k * k, axis=1, keepdims=True) + EPS)
        b = beta_all[rows, 2 * N_HEADS + h:2 * N_HEADS + h + 1]
        gc = gcum[rows, 3 * N_HEADS + h:3 * N_HEADS + h + 1]
        gc_row = xt[3 * N_HEADS + h:3 * N_HEADS + h + 1, rows]
        decay = jnp.exp(jnp.where(incl, gc - gc_row, -jnp.inf))
        kb = k * b
        egc = jnp.exp(gc)
        gl = gc[L - 1:L, :]
        loc.append(dict(q=q, k=k, kb=kb, decay=decay, vb=v * b, kbe=kb * egc, q_dec=q * egc,
                        k_dec=k * jnp.exp(gl - gc), g_last=jnp.exp(gl)))
    a_mat = [jnp.where(strict, _dot_nt(d["kb"], d["k"]) * d["decay"], 0.0) for d in loc]
    qk_mat = [_dot_nt(d["q"], d["k"]) * d["decay"] for d in loc]
    eye = jnp.where(incl & jnp.logical_not(strict), 1.0, 0.0)
    pw = a_mat
    inv = [eye - a for a in a_mat]
    power = 1
    while 2 * power < L:
        pw = [_dot(x, x) for x in pw]
        inv = [p + _dot(p, x) for p, x in zip(inv, pw)]
        power *= 2
    us = [_dot(t, d["vb"]) for t, d in zip(inv, loc)]
    ws = [_dot(t, d["kbe"]) for t, d in zip(inv, loc)]
    s = [s_s[h] for h in range(N_HEADS)]
    heads = range(N_HEADS)
    for s_i in range(R // L):
        rows = slice(s_i * L, (s_i + 1) * L)
        p0 = s_i * N_HEADS
        w_s = [_dot(ws[p0 + h], s[h]) for h in heads]
        q_s = [_dot(loc[p0 + h]["q_dec"], s[h]) for h in heads]
        v_new = [us[p0 + h] - w_s[h] for h in heads]
        o = [q_s[h] + _dot(qk_mat[p0 + h], v_new[h]) for h in heads]
        s = [s[h] * loc[p0 + h]["g_last"] + _dot_tn(loc[p0 + h]["k_dec"], v_new[h]) for h in heads]
        for h in heads:
            gate = x_ref[rows, _head_cols(3, h)]
            o_ref[rows, h * HEAD_DIM:(h + 1) * HEAD_DIM] = _head_rms(o[h]) * (gate * _sigmoid(gate))
    for h in range(N_HEADS):
        s_s[h] = s[h]

    @pl.when(ci == pl.num_programs(1) - 1)
    def _():
        s1_ref[0] = s_s[...]
        cv_ref[0] = xe_s[8 - (CONV_W - 1):8, :]


def _gdn_prompt(gd, gates, gate_bias_row, a_log_row, conv_w, tril):
    L = GDN_STEP
    nc = SEQ // L
    S = jax.ShapeDtypeStruct
    return pl.pallas_call(
        _gdn_prompt_kernel,
        grid=(BATCH, nc),
        in_specs=[pl.BlockSpec((L, 4 * GROUP_W), lambda b, c: (b * nc + c, 0)),
                  pl.BlockSpec((L, LANES), lambda b, c: (b * nc + c, 0)),
                  _const_spec((1, LANES)), _const_spec((1, LANES)), _const_spec((CONV_W, 3 * GROUP_W)),
                  _const_spec((L, L))],
        out_specs=[pl.BlockSpec((L, GROUP_W), lambda b, c: (b * nc + c, 0)),
                   pl.BlockSpec((1, N_HEADS, HEAD_DIM, HEAD_DIM), lambda b, c: (b, 0, 0, 0)),
                   pl.BlockSpec((1, CONV_W - 1, 3 * GROUP_W), lambda b, c: (b, 0, 0))],
        out_shape=[S((N_PROMPT, GROUP_W), F32), S((BATCH, N_HEADS, HEAD_DIM, HEAD_DIM), F32),
                   S((BATCH, CONV_W - 1, 3 * GROUP_W), F32)],
        scratch_shapes=[pltpu.VMEM((N_HEADS, HEAD_DIM, HEAD_DIM), F32), pltpu.VMEM((8 + L, 3 * GROUP_W), F32)],
        compiler_params=_cparams(("parallel", "arbitrary")),
        name="gdn_prompt",
    )(gd, gates, gate_bias_row, a_log_row, conv_w, tril)


def _rope_rows(x, cos, sin_signed):
    lane = lax.broadcasted_iota(I32, x.shape, 1)
    first = jnp.bitwise_and(lane, HEAD_DIM - 1) < HEAD_DIM // 2
    w = x.shape[1]
    swapped = jnp.where(first, pltpu.roll(x, w - HEAD_DIM // 2, 1), pltpu.roll(x, HEAD_DIM // 2, 1))
    return x * cos + swapped * sin_signed


def _ret_prompt_kernel(x_ref, cos_ref, sin_ref, dm_ref, xz_ref, gch_ref, o_ref, s1_ref, s_s):
    ci = pl.program_id(1)

    @pl.when(ci == 0)
    def _():
        s_s[...] = jnp.zeros_like(s_s)

    L = CHUNK
    cos = cos_ref[...]
    sin = sin_ref[...]
    qr = _rope_rows(x_ref[:, 0:GROUP_W], cos, sin)
    kr = _rope_rows(x_ref[:, GROUP_W:2 * GROUP_W], cos, sin) * SCALE
    prob = []
    for s_i in range(x_ref.shape[0] // L):
        rows = slice(s_i * L, (s_i + 1) * L)
        for h in range(N_HEADS):
            sl = slice(h * HEAD_DIM, (h + 1) * HEAD_DIM)
            prob.append(dict(rows=rows, h=h, sl=sl, q=qr[rows, sl], k=kr[rows, sl], v=x_ref[rows, _head_cols(2, h)]))
    qk = [_dot_nt(p["q"], p["k"]) * dm_ref[p["h"]] for p in prob]
    intra = [_dot(a, p["v"]) for a, p in zip(qk, prob)]
    kv = [_dot_tn(p["k"] * xz_ref[:, N_HEADS + p["h"]:N_HEADS + p["h"] + 1], p["v"]) for p in prob]
    s = [s_s[h] for h in range(N_HEADS)]
    s_in = []
    for p, kv_p in zip(prob, kv):
        s_in.append(s[p["h"]])
        s[p["h"]] = s[p["h"]] * gch_ref[p["h"]] + kv_p
    inter = [_dot(p["q"], s0) * xz_ref[:, p["h"]:p["h"] + 1] for p, s0 in zip(prob, s_in)]
    for p, a, b in zip(prob, intra, inter):
        gate = x_ref[p["rows"], _head_cols(3, p["h"])]
        o_ref[p["rows"], p["sl"]] = _head_rms(a + b) * (gate * _sigmoid(gate))
    for h in range(N_HEADS):
        s_s[h] = s[h]

    @pl.when(ci == pl.num_programs(1) - 1)
    def _():
        s1_ref[0] = s_s[...]


def _ret_prompt(rt, cos, sin, dmat, xz, gch):
    L = RET_STEP
    nc = SEQ // L
    S = jax.ShapeDtypeStruct
    return pl.pallas_call(
        _ret_prompt_kernel,
        grid=(BATCH, nc),
        in_specs=[pl.BlockSpec((L, 4 * GROUP_W), lambda b, c: (b * nc + c, 0)),
                  pl.BlockSpec((L, GROUP_W), lambda b, c: (c, 0)),
                  pl.BlockSpec((L, GROUP_W), lambda b, c: (c, 0)),
                  _const_spec((N_HEADS, CHUNK, CHUNK)), _const_spec((CHUNK, LANES)),
                  pl.BlockSpec(memory_space=pltpu.SMEM)],
        out_specs=[pl.BlockSpec((L, GROUP_W), lambda b, c: (b * nc + c, 0)),
                   pl.BlockSpec((1, N_HEADS, HEAD_DIM, HEAD_DIM), lambda b, c: (b, 0, 0, 0))],
        out_shape=[S((N_PROMPT, GROUP_W), F32), S((BATCH, N_HEADS, HEAD_DIM, HEAD_DIM), F32)],
        scratch_shapes=[pltpu.VMEM((N_HEADS, HEAD_DIM, HEAD_DIM), F32)],
        compiler_params=_cparams(("parallel", "arbitrary")),
        name="ret_prompt",
    )(rt, cos, sin, dmat, xz, gch)


def _gdn_decode_conv_kernel(raw_ref, c0_ref, cw_ref, qkvt_ref, cv_ref):
    raw = raw_ref[...]
    conv = (cw_ref[3:4, :] * raw + cw_ref[2:3, :] * c0_ref[2] + cw_ref[1:2, :] * c0_ref[1]
            + cw_ref[0:1, :] * c0_ref[0])
    qkv = conv * _sigmoid(conv)
    cv_ref[0] = c0_ref[1]
    cv_ref[1] = c0_ref[2]
    cv_ref[2] = raw
    for blk in range(3):
        for h in range(N_HEADS):
            lo = blk * GROUP_W + h * HEAD_DIM
            x = qkv[:, lo:lo + HEAD_DIM]
            if blk == 0:
                x = x * lax.rsqrt(jnp.sum(x * x, axis=1, keepdims=True) + EPS) * SCALE
            elif blk == 1:
                x = x * lax.rsqrt(jnp.sum(x * x, axis=1, keepdims=True) + EPS)
            if h % 2 == 0:
                pair = x
            else:
                qkvt_ref[lo - HEAD_DIM:lo + HEAD_DIM, :] = jnp.concatenate([pair, x], axis=1).T


def _gdn_decode_conv(raw, conv0, conv_w):
    S = jax.ShapeDtypeStruct
    return pl.pallas_call(
        _gdn_decode_conv_kernel,
        out_shape=[S((3 * GROUP_W, DEC_BATCH), F32), S((CONV_W - 1, DEC_BATCH, 3 * GROUP_W), F32)],
        compiler_params=_cparams(None),
        name="gdn_decode_conv",
    )(raw, conv0, conv_w)


def _decode_rec_kernel(sc_ref, mlq_ref, mlk_ref, mlv_ref, mlo_ref, gq_ref, gk_ref, gv_ref, gg_ref,
                       rq_ref, rk_ref, rv_ref, rg_ref, gt_ref, cos_ref, sin_ref,
                       c0_ref, n0_ref, m0_ref, sg0_ref, sr0_ref,
                       oml_ref, ogd_ref, ort_ref, c1_ref, n1_ref, m1_ref, sg1_ref, sr1_ref, va_s, vb_s):
    h = pl.program_id(0)
    D = HEAD_DIM
    sum0 = lambda x: jnp.sum(x, axis=0, keepdims=True)
    rms0 = lambda x: x * lax.rsqrt(sum0(x * x) * (1.0 / D) + EPS)

    li = gt_ref[pl.ds(h, 1), :] + sc_ref[0, h]
    lf = _log_sigmoid(gt_ref[pl.ds(N_HEADS + h, 1), :] + sc_ref[1, h])
    m0 = m0_ref[0]
    q = mlq_ref[...]
    k = mlk_ref[...] * SCALE
    v = mlv_ref[...]
    log_inter = lf + m0
    m_row = jnp.maximum(log_inter, li)
    s = sum0(q * k) * jnp.exp(li - m_row)
    w_inter = jnp.exp(log_inter - m_row)
    decay = jnp.exp(lf + m0 - m_row)
    kw = k * jnp.exp(li - m_row)

    va_s[...] = kw

    def ml_body(d, qc):
        c_d = c0_ref[0, d]
        c1_ref[0, d] = decay * c_d + va_s[pl.ds(d, 1), :] * v
        return qc + mlq_ref[pl.ds(d, 1), :] * c_d

    n0 = n0_ref[0]
    n1_ref[0] = decay * n0 + kw
    qc = lax.fori_loop(0, D, ml_body, jnp.zeros((D, LANES), F32))
    num = s * v + w_inter * qc
    den = s + w_inter * sum0(q * n0)
    hh = num / jnp.maximum(jnp.abs(den), jnp.exp(-m_row))
    m1_ref[0] = m_row
    oml_ref[...] = rms0(hh) * _sigmoid(mlo_ref[...])

    beta = _sigmoid(gt_ref[pl.ds(2 * N_HEADS + h, 1), :])
    gdec = -sc_ref[3, h] * _softplus(gt_ref[pl.ds(3 * N_HEADS + h, 1), :] + sc_ref[2, h])
    eg = jnp.exp(gdec)
    gv = gv_ref[...]

    def ks_body(d, acc):
        return acc + gk_ref[pl.ds(d, 1), :] * sg0_ref[0, d]

    ks = lax.fori_loop(0, D, ks_body, jnp.zeros((D, LANES), F32))
    v_new = beta * gv - (beta * eg) * ks

    def gd_body(d, acc):
        s_new = eg * sg0_ref[0, d] + gk_ref[pl.ds(d, 1), :] * v_new
        sg1_ref[0, d] = s_new
        return acc + gq_ref[pl.ds(d, 1), :] * s_new

    og = lax.fori_loop(0, D, gd_body, jnp.zeros((D, LANES), F32))
    gate = gg_ref[...]
    ogd_ref[...] = rms0(og) * (gate * _sigmoid(gate))

    half = D // 2

    def rope(ref):
        x = ref[...]
        sw = jnp.concatenate([x[half:, :], x[:half, :]], axis=0)
        return x * cos_ref[...] + sw * sin_ref[...]

    va_s[...] = rope(rq_ref)
    vb_s[...] = rope(rk_ref) * SCALE
    rv = rv_ref[...]
    gamma = sc_ref[4, h]

    def rt_body(d, acc):
        s_new = gamma * sr0_ref[0, d] + vb_s[pl.ds(d, 1), :] * rv
        sr1_ref[0, d] = s_new
        return acc + va_s[pl.ds(d, 1), :] * s_new

    ort = lax.fori_loop(0, D, rt_body, jnp.zeros((D, LANES), F32))
    gate = rg_ref[...]
    ort_ref[...] = rms0(ort) * (gate * _sigmoid(gate))


def _decode_rec(scalars, pt, qkvt, cos_t, sin_t, c0, n0, m0, sg0, sr0):
    D = HEAD_DIM
    S = jax.ShapeDtypeStruct
    prow = lambda col, blk: pl.BlockSpec((D, LANES), lambda h: ((col + blk * GROUP_W) // D + h, 0))
    vec = lambda blk: pl.BlockSpec((D, LANES), lambda h: (blk * N_HEADS + h, 0))
    st4 = pl.BlockSpec((1, D, D, LANES), lambda h: (h, 0, 0, 0))
    st3 = pl.BlockSpec((1, D, LANES), lambda h: (h, 0, 0))
    st2 = pl.BlockSpec((1, 1, LANES), lambda h: (h, 0, 0))
    in_specs = ([pl.BlockSpec(memory_space=pltpu.SMEM)]
                + [prow(COL_ML, blk) for blk in range(4)]
                + [vec(0), vec(1), vec(2), prow(COL_GD, 3)]
                + [prow(COL_RT, blk) for blk in range(4)]
                + [pl.BlockSpec((LANES, LANES), lambda h: (COL_GATES // LANES, 0)),
                   _const_spec((D, LANES)), _const_spec((D, LANES)), st4, st3, st2, st4, st4])
    return pl.pallas_call(
        _decode_rec_kernel,
        grid=(N_HEADS,),
        in_specs=in_specs,
        out_specs=[vec(0), vec(0), vec(0), st4, st3, st2, st4, st4],
        out_shape=[S((GROUP_W, LANES), F32)] * 3
        + [S((N_HEADS, D, D, LANES), F32), S((N_HEADS, D, LANES), F32), S((N_HEADS, 1, LANES), F32),
           S((N_HEADS, D, D, LANES), F32), S((N_HEADS, D, D, LANES), F32)],
        scratch_shapes=[pltpu.VMEM((D, LANES), F32), pltpu.VMEM((D, LANES), F32)],
        compiler_params=_cparams(("parallel",)),
        name="decode_rec",
    )(scalars, pt, pt, pt, pt, qkvt, qkvt, qkvt, pt, pt, pt, pt, pt, pt, cos_t, sin_t, c0, n0, m0, sg0, sr0)


def _outproj_kernel(x_ref, osb_ref, oml_ref, ogd_ref, ort_ref, gain_ref, wo_ref, nw_ref, rwh_ref, rwl_ref,
                    x1_ref, h2_ref, lg_ref, *, transposed):
    parts = [osb_ref[...]]
    for ref in (oml_ref, ogd_ref, ort_ref):
        parts.append(ref[...].T if transposed else ref[...])
    y = None
    for g, p in enumerate(parts):
        cols = slice(g * GROUP_W, (g + 1) * GROUP_W)
        t = _dot(p * gain_ref[:, cols], wo_ref[cols, :])
        y = t if y is None else y + t
    x1 = x_ref[...] + y
    h2 = _norm_rows(x1, nw_ref[...])
    x1_ref[...] = x1
    h2_ref[...] = h2.astype(BF16)
    lg_ref[...] = _gates_dot(h2, rwh_ref, rwl_ref)


def _outproj(x, osb, oml, ogd, ort, gain, wo, nw, rwh, rwl, transposed):
    n = x.shape[0]
    tm = min(TM_OUTPROJ, n)
    S = jax.ShapeDtypeStruct
    row = lambda c: pl.BlockSpec((tm, c), lambda i: (i, 0))
    mix = _const_spec((GROUP_W, DEC_BATCH)) if transposed else row(GROUP_W)
    return pl.pallas_call(
        functools.partial(_outproj_kernel, transposed=transposed),
        grid=(n // tm,),
        in_specs=[row(D_MODEL), row(GROUP_W), mix, mix, mix, _const_spec((1, D_MODEL)),
                  _const_spec((D_MODEL, D_MODEL)), _const_spec((1, D_MODEL)),
                  _const_spec((D_MODEL, LANES)), _const_spec((D_MODEL, LANES))],
        out_specs=[row(D_MODEL), row(D_MODEL), row(LANES)],
        out_shape=[S((n, D_MODEL), F32), S((n, D_MODEL), BF16), S((n, LANES), F32)],
        compiler_params=_cparams(("parallel",)),
        name="outproj",
    )(x, osb, oml, ogd, ort, gain, wo, nw, rwh, rwl)


def _route_kernel(lg_ref, rb_ref, tri_ref, lt_ref, cin_ref, pr_ref, rf_ref, ti_ref, cnt_ref):
    @pl.when(pl.program_id(0) == 0)
    def _():
        cnt_ref[...] = cin_ref[...]

    lg = lg_ref[...] + rb_ref[...]
    lane = lax.broadcasted_iota(I32, lg.shape, 1)
    ninf = -jnp.inf
    big = LANES - 1
    rmax = lambda x: jnp.max(x, axis=1, keepdims=True)
    rmin = lambda x: jnp.min(x, axis=1, keepdims=True)
    gl = jnp.where(lane < N_GROUPS, lg, ninf)
    gmax = rmax(gl)
    g_sel = rmin(jnp.where(gl == gmax, lane, big))
    g_prob = 1.0 / jnp.sum(jnp.exp(gl - gmax), axis=1, keepdims=True)
    e_lane = (lane >= N_GROUPS) & (lane < N_GROUPS + N_EXPERTS)
    em = jnp.where(e_lane & (jnp.right_shift(lane - N_GROUPS, 3) == g_sel), lg, ninf)
    v1 = rmax(em)
    i1 = rmin(jnp.where(em == v1, lane, big))
    em2 = jnp.where(lane == i1, ninf, em)
    v2 = rmax(em2)
    i2 = rmin(jnp.where(em2 == v2, lane, big))
    t = jnp.exp(v2 - v1)
    gate1 = g_prob / (1.0 + t)
    gate2 = g_prob * t / (1.0 + t)
    e1 = i1 - N_GROUPS
    e2 = i2 - N_GROUPS
    onehot = jnp.where((lane == e1) | (lane == e2), 1.0, 0.0)
    before = _dot(tri_ref[...], onehot)
    tile_cnt = jnp.sum(onehot, axis=0, keepdims=True)
    padded = jnp.floor((tile_cnt + (RUN_ALIGN - 1)) * (1.0 / RUN_ALIGN)) * RUN_ALIGN
    run_start = _dot(jnp.broadcast_to(padded, (8, LANES)), lt_ref[...])[0:1, :]
    pos = before + run_start
    p1 = jnp.sum(jnp.where(lane == e1, pos, 0.0), axis=1, keepdims=True)
    p2 = jnp.sum(jnp.where(lane == e2, pos, 0.0), axis=1, keepdims=True)
    pr = jnp.where(lane == 0, p1, jnp.where(lane == 1, p2, 0.0)).astype(I32)
    pr_ref[...] = pr.T[0:8, :]
    rf_ref[...] = jnp.where(lane == 0, gate1, jnp.where(lane == 1, gate2, jnp.where(lane == 2, p1,
                            jnp.where(lane == 3, p2, 0.0))))
    row8 = lax.broadcasted_iota(I32, (8, LANES), 0)
    ti_ref[...] = jnp.where(row8 == 0, padded, jnp.where(row8 == 1, cnt_ref[...],
                            jnp.where(row8 == 2, run_start, 0.0))).astype(I32)
    cnt_ref[...] += padded


def _route(logits, rb, tri, lanes_lt, cnt_in):
    n = logits.shape[0]
    tm = tri.shape[0]
    S = jax.ShapeDtypeStruct
    row = pl.BlockSpec((tm, LANES), lambda i: (i, 0))
    return pl.pallas_call(
        _route_kernel,
        grid=(n // tm,),
        in_specs=[row, _const_spec((1, LANES)), _const_spec((tm, tm)), _const_spec((LANES, LANES)),
                  _const_spec((1, LANES))],
        out_specs=[pl.BlockSpec((8, tm), lambda i: (0, i)), row, pl.BlockSpec((8, LANES), lambda i: (i, 0)),
                   _const_spec((1, LANES))],
        out_shape=[S((8, n), I32), S((n, LANES), F32), S((8 * (n // tm), LANES), I32), S((1, LANES), F32)],
        compiler_params=_cparams(("arbitrary",)),
        name="route",
    )(logits, rb, tri, lanes_lt, cnt_in)


def _packed_rows(tm):
    return 2 * tm + N_EXPERTS * RUN_ALIGN


def _copy_priority(rows):
    return 0 if rows == RUN_CHUNK else 1


def _for_each_run_chunk(plan_refs, tile, fn):
    offs_ref, len_ref, before_ref, start_ref = plan_refs
    base = tile * N_EXPERTS

    def per_expert(e, _):
        n_rows = len_ref[base + e]
        n_big = n_rows // RUN_CHUNK
        packed0 = start_ref[base + e]
        slot0 = offs_ref[e] + before_ref[base + e]

        def chunks(first, count, rows):
            def per_chunk(c, _):
                off = first + c * rows
                fn(pl.multiple_of(packed0 + off, RUN_ALIGN), pl.multiple_of(slot0 + off, RUN_ALIGN), rows)
                return 0

            lax.fori_loop(0, count, per_chunk, 0)

        chunks(0, n_big, RUN_CHUNK)
        chunks(n_big * RUN_CHUNK, (n_rows - n_big * RUN_CHUNK) // RUN_ALIGN, RUN_ALIGN)
        return 0

    lax.fori_loop(0, N_EXPERTS, per_expert, 0)


def _dispatch_kernel(offs_ref, len_ref, before_ref, start_ref, h_ref, pr_ref, xs_in_ref, xs_ref, pk_s, sem):
    del xs_in_ref
    plan_refs = (offs_ref, len_ref, before_ref, start_ref)
    tm = h_ref.shape[0]
    i = pl.program_id(0)
    slot = lax.broadcasted_iota(I32, (_packed_rows(tm), tm), 0)
    sel = jnp.where((slot == pr_ref[0:1, :]) | (slot == pr_ref[1:2, :]), 1.0, 0.0)
    pk_s[...] = _dot(sel, h_ref[...]).astype(BF16)

    def copy(packed_row, slot_row, rows):
        return pltpu.make_async_copy(pk_s.at[pl.ds(packed_row, rows), :],
                                     xs_ref.at[pl.ds(slot_row, rows), :], sem.at[0])

    _for_each_run_chunk(plan_refs, i, lambda p, s, n: copy(p, s, n).start(priority=_copy_priority(n)))
    _for_each_run_chunk(plan_refs, i, lambda p, s, n: copy(p, s, n).wait())


def _dispatch(plan, h2, pr, xs):
    n = h2.shape[0]
    tm = min(TM_ROUTE, n)
    return pl.pallas_call(
        _dispatch_kernel,
        grid_spec=pltpu.PrefetchScalarGridSpec(
            num_scalar_prefetch=4,
            grid=(n // tm,),
            in_specs=[pl.BlockSpec((tm, D_MODEL), lambda i, *_: (i, 0)),
                      pl.BlockSpec((8, tm), lambda i, *_: (0, i)),
                      pl.BlockSpec(memory_space=pl.ANY)],
            out_specs=pl.BlockSpec(memory_space=pl.ANY),
            scratch_shapes=[pltpu.VMEM((_packed_rows(tm), D_MODEL), BF16), pltpu.SemaphoreType.DMA((1,))]),
        out_shape=jax.ShapeDtypeStruct(xs.shape, xs.dtype),
        input_output_aliases={6: 0},
        compiler_params=_cparams(("arbitrary",)),
        name="dispatch",
    )(*plan, h2, pr, xs)


def _experts_kernel(te_ref, ts_ref, xs_ref, wg_ref, wu_ref, wd_ref, ys_ref, wg_s, wu_s, wd_s):
    i = pl.program_id(0)
    valid = ts_ref[i] == i
    fresh = (i == 0) | (te_ref[i] != te_ref[jnp.maximum(i - 1, 0)])

    @pl.when(valid & fresh)
    def _():
        wg_s[...] = wg_ref[0].astype(BF16)
        wu_s[...] = wu_ref[0].astype(BF16)
        wd_s[...] = wd_ref[0].astype(BF16)

    @pl.when(valid)
    def _():
        x = xs_ref[...].astype(BF16)
        a = _dot(x, wg_s[...])
        u = _dot(x, wu_s[...])
        act = a * _sigmoid(a) * u
        ys_ref[...] = _dot(act, wd_s[...]).astype(BF16)

    @pl.when(jnp.logical_not(valid))
    def _():
        ys_ref[...] = jnp.zeros_like(ys_ref)


def _experts(tile_expert, tile_src, xs, wg, wu, wd, layer):
    tm = TM_EXPERT
    wspec = lambda a, b: pl.BlockSpec((None, 1, a, b), lambda i, te, tv: (layer, te[i], 0, 0))
    return pl.pallas_call(
        _experts_kernel,
        grid_spec=pltpu.PrefetchScalarGridSpec(
            num_scalar_prefetch=2,
            grid=(N_TILES,),
            in_specs=[pl.BlockSpec((tm, D_MODEL), lambda i, te, ts: (ts[i], 0)),
                      wspec(D_MODEL, D_EXPERT), wspec(D_MODEL, D_EXPERT), wspec(D_EXPERT, D_MODEL)],
            out_specs=pl.BlockSpec((tm, D_MODEL), lambda i, te, tv: (i, 0)),
            scratch_shapes=[pltpu.VMEM((D_MODEL, D_EXPERT), BF16), pltpu.VMEM((D_MODEL, D_EXPERT), BF16),
                            pltpu.VMEM((D_EXPERT, D_MODEL), BF16)]),
        out_shape=jax.ShapeDtypeStruct((N_SLOTS, D_MODEL), BF16),
        compiler_params=_cparams(("arbitrary",)),
        name="experts",
    )(tile_expert, tile_src, xs, wg, wu, wd)


def _combine_kernel(offs_ref, len_ref, before_ref, start_ref, x1_ref, rf_ref, fw_ref, ys_ref, out_ref,
                    pk_s, sem, *, final):
    plan_refs = (offs_ref, len_ref, before_ref, start_ref)
    tm = x1_ref.shape[0]
    i = pl.program_id(0)
    buf = i % 2

    def copy(b, packed_row, slot_row, rows):
        return pltpu.make_async_copy(ys_ref.at[pl.ds(slot_row, rows), :],
                                     pk_s.at[b, pl.ds(packed_row, rows), :], sem.at[b])

    @pl.when(i == 0)
    def _():
        pk_s[...] = jnp.zeros_like(pk_s)
        _for_each_run_chunk(plan_refs, i, lambda p, s, n: copy(buf, p, s, n).start(priority=_copy_priority(n)))

    @pl.when(i + 1 < pl.num_programs(0))
    def _():
        _for_each_run_chunk(plan_refs, i + 1, lambda p, s, n: copy(1 - buf, p, s, n).start(priority=_copy_priority(n)))

    _for_each_run_chunk(plan_refs, i, lambda p, s, n: copy(buf, p, s, n).wait())
    slot = lax.broadcasted_iota(I32, (tm, _packed_rows(tm)), 1)
    packed = pk_s[buf]
    x2 = x1_ref[...]
    for k in range(2):
        sel = jnp.where(slot == rf_ref[:, 2 + k:3 + k].astype(I32), 1.0, 0.0)
        x2 = x2 + rf_ref[:, k:k + 1] * _dot(sel, packed)
    out_ref[...] = _norm_rows(x2, fw_ref[...]) if final else x2


def _combine(plan, x1, rf, fw, ys, final):
    n = x1.shape[0]
    tm = min(TM_ROUTE, n)
    row = lambda c: pl.BlockSpec((tm, c), lambda i, *_: (i, 0))
    return pl.pallas_call(
        functools.partial(_combine_kernel, final=final),
        grid_spec=pltpu.PrefetchScalarGridSpec(
            num_scalar_prefetch=4,
            grid=(n // tm,),
            in_specs=[row(D_MODEL), row(LANES), pl.BlockSpec((1, D_MODEL), lambda i, *_: (0, 0)),
                      pl.BlockSpec(memory_space=pl.ANY)],
            out_specs=row(D_MODEL),
            scratch_shapes=[pltpu.VMEM((2, _packed_rows(tm), D_MODEL), BF16), pltpu.SemaphoreType.DMA((2,))]),
        out_shape=jax.ShapeDtypeStruct((n, D_MODEL), F32),
        compiler_params=_cparams(("arbitrary",)),
        name="combine",
    )(*plan, x1, rf, fw, ys)


def _tri(n, kind, chunk=1):
    r = np.arange(n)[:, None]
    c = np.arange(n)[None, :]
    m = {"chunk_lower_incl": (c <= r) & (r // chunk == c // chunk),
         "lower_strict": c < r,
         "row_gt_col": r > c,
         "row_lt_col": r < c,
         "later_page": (r % 8 == c % 8) & (c // 8 > r // 8)}[kind]
    return jnp.asarray(m.astype(np.float32), dtype=BF16)


def _rope_tables(pos):
    half = HEAD_DIM // 2
    inv = ROPE_BASE ** (-np.arange(half, dtype=np.float64) / half)
    ang = np.asarray(pos, np.float64)[:, None] * inv[None, :]
    cos = np.concatenate([np.cos(ang), np.cos(ang)], axis=1)
    sin = np.concatenate([-np.sin(ang), np.sin(ang)], axis=1)
    return cos.astype(np.float32), sin.astype(np.float32)


def _retention_consts(L):
    log_gamma = np.log1p(-np.exp2(-5.0 - np.arange(N_HEADS, dtype=np.float64)))
    idx = np.arange(L, dtype=np.float64)
    diff = np.maximum(idx[:, None] - idx[None, :], 0.0)
    dmat = np.where(idx[None, :] <= idx[:, None], np.exp(log_gamma[:, None, None] * diff), 0.0)
    xi = np.exp(log_gamma[:, None] * (idx + 1.0))
    zeta = np.exp(log_gamma[:, None] * (L - 1.0 - idx))
    xz = np.zeros((L, LANES), np.float64)
    xz[:, 0:N_HEADS] = xi.T
    xz[:, N_HEADS:2 * N_HEADS] = zeta.T
    return (jnp.asarray(dmat, F32), jnp.asarray(xz, F32), jnp.asarray(np.exp(log_gamma * L), F32),
            np.exp(log_gamma))


def _hi_lo(w):
    hi = w.astype(BF16)
    return hi, (w - hi.astype(F32)).astype(BF16)


def _pad_lanes(w):
    return jnp.pad(w, ((0, 0), (0, LANES - w.shape[1])))


def _prep_w_in(w):
    ml0 = 3 * GROUP_W
    mlg = ml0 + 4 * GROUP_W
    gd0 = mlg + 2 * N_HEADS
    gdg = gd0 + 4 * GROUP_W
    rt0 = gdg + 2 * N_HEADS
    main = jnp.concatenate([w[:, 0:ml0], w[:, ml0:mlg], w[:, gd0:gdg], w[:, rt0:]], axis=1).astype(BF16)
    gates = _pad_lanes(jnp.concatenate([w[:, mlg:gd0], w[:, gdg:rt0]], axis=1))
    return (main,) + _hi_lo(gates)


def _tile_plan(tile_info):
    t = tile_info.reshape(-1, 8, LANES)[:, :, :N_EXPERTS]
    return t[:, 0].reshape(-1), t[:, 1].reshape(-1), t[:, 2].reshape(-1)


def _dispatch_plan(cnt):
    ntile = (cnt + TM_EXPERT - 1) // TM_EXPERT
    tile_end = jnp.cumsum(ntile)
    offs = (tile_end - ntile) * TM_EXPERT
    tid = jnp.arange(N_TILES, dtype=I32)
    te = jnp.minimum(jnp.sum(tid[:, None] >= tile_end[None, :], axis=1), N_EXPERTS - 1).astype(I32)
    tile_src = jnp.minimum(tid, tile_end[-1] - 1).astype(I32)
    return offs.astype(I32), te, tile_src


def kernel(x_prompt, x_sample, cache_sb_k, cache_sb_v, state_mlstm_c, state_mlstm_n, state_mlstm_m, state_gdn_s, state_gdn_conv, state_ret_s, page_table, norm_attn_w, w_in, sb_logit_bias, mlstm_gate_bias, gdn_conv_w, gdn_a_log, gdn_dt_bias, head_norm_w, w_out, norm_ffn_w, router_group_w, router_group_b, router_expert_w, router_expert_b, expert_w_gate, expert_w_up, expert_w_down, final_norm_w):
    D = HEAD_DIM
    xp = x_prompt.reshape(N_PROMPT, D_MODEL)
    xd = x_sample.reshape(DEC_BATCH, D_MODEL)
    n_pool = cache_sb_k.shape[1]
    cache_kt = cache_sb_k.transpose(0, 1, 3, 4, 2).reshape(DEPTH, n_pool, GROUP_W, PAGE_SIZE)
    cache_vt = cache_sb_v.transpose(0, 1, 3, 4, 2).reshape(DEPTH, n_pool, GROUP_W, PAGE_SIZE)

    tril_ml = _tri(MLSTM_STEP, "chunk_lower_incl", CHUNK)
    tril_gd = _tri(GDN_STEP, "chunk_lower_incl", GDN_CHUNK)
    tri_sb = _tri(SB_BLOCK, "row_gt_col")
    tri_page = _tri(PAGE_SIZE, "row_gt_col")
    later_pages = _tri(8 * N_PAGES, "later_page")
    lanes_lt = _tri(LANES, "row_lt_col")
    tri_rp = _tri(TM_ROUTE, "lower_strict")
    tri_rd = _tri(DEC_BATCH, "lower_strict")
    cos_p, sin_p = _rope_tables(np.arange(SEQ))
    cos_p = jnp.asarray(np.tile(cos_p, (1, N_HEADS)))
    sin_p = jnp.asarray(np.tile(sin_p, (1, N_HEADS)))
    cos_d, sin_d = _rope_tables([PAST_LEN])
    cos_d = jnp.asarray(np.tile(cos_d.T, (1, LANES)))
    sin_d = jnp.asarray(np.tile(sin_d.T, (1, LANES)))
    dmat, xz, gch, gamma = _retention_consts(CHUNK)
    zeros4 = jnp.zeros((N_HEADS,), F32)
    xs = None

    outs = {k: [] for k in ("kp", "vp", "ks", "vs", "cp", "np", "mp", "cs", "ns", "ms", "gp", "gcp", "gs", "gcs",
                            "rp", "rs")}
    yp = yd = None
    for l in range(DEPTH):
        w_main, wg_hi, wg_lo = _prep_w_in(w_in[l])
        nw = norm_attn_w[l][None, :]
        gate_bias_row = _pad_lanes(jnp.concatenate([mlstm_gate_bias[l], zeros4, gdn_dt_bias[l]])[None, :])
        a_log_row = _pad_lanes(jnp.concatenate([zeros4, zeros4, zeros4, gdn_a_log[l]])[None, :])
        bias = sb_logit_bias[l]

        sbq, sbk, sbv, ml, gd, rt, gt = _inproj_prompt(xp, nw, w_main, wg_hi, wg_lo)
        osb_p = _sb_prompt(sbq, sbk, sbv, bias, tri_sb)
        oml_p, c1p, n1p, m1p = _mlstm_prompt(ml, gt, gate_bias_row, tril_ml)
        ogd_p, s1p, cv1p = _gdn_prompt(gd, gt, gate_bias_row, a_log_row, gdn_conv_w[l], tril_gd)
        ort_p, r1p = _ret_prompt(rt, cos_p, sin_p, dmat, xz, gch)

        sbq_d, gdraw_d, pt = _inproj_decode(xd, nw, w_main, wg_hi, wg_lo)
        osb_d = _sb_decode(sbq_d, cache_kt, cache_vt, page_table, bias, tri_page, later_pages, l)
        qkvt, cv1d = _gdn_decode_conv(gdraw_d, state_gdn_conv[l].transpose(1, 0, 2), gdn_conv_w[l])
        scalars = jnp.stack([mlstm_gate_bias[l][:N_HEADS], mlstm_gate_bias[l][N_HEADS:], gdn_dt_bias[l],
                             jnp.exp(gdn_a_log[l]), jnp.asarray(gamma, F32), zeros4, zeros4, zeros4])
        oml_d, ogd_d, ort_d, c1d, n1d, m1d, s1d, r1d = _decode_rec(
            scalars, pt, qkvt, cos_d, sin_d,
            state_mlstm_c[l].transpose(1, 2, 3, 0), state_mlstm_n[l].transpose(1, 2, 0),
            state_mlstm_m[l].T.reshape(N_HEADS, 1, DEC_BATCH),
            state_gdn_s[l].transpose(1, 2, 3, 0), state_ret_s[l].transpose(1, 2, 3, 0))

        gain = head_norm_w[l][None, :]
        wo = w_out[l].astype(BF16)
        nfw = norm_ffn_w[l][None, :]
        rw_hi, rw_lo = _hi_lo(_pad_lanes(jnp.concatenate([router_group_w[l], router_expert_w[l]], axis=1)))
        rb = _pad_lanes(jnp.concatenate([router_group_b[l], router_expert_b[l]])[None, :])
        x1p, h2p, lgp = _outproj(xp, osb_p, oml_p, ogd_p, ort_p, gain, wo, nfw, rw_hi, rw_lo, False)
        x1d, h2d, lgd = _outproj(xd, osb_d, oml_d, ogd_d, ort_d, gain, wo, nfw, rw_hi, rw_lo, True)
        prp, rfp, tip, cnt_p = _route(lgp, rb, tri_rp, lanes_lt, jnp.zeros((1, LANES), F32))
        prd, rfd, tid, cnt = _route(lgd, rb, tri_rd, lanes_lt, cnt_p)

        offs, te, tile_src = _dispatch_plan(cnt[0, :N_EXPERTS].astype(I32))
        plan_p = (offs,) + _tile_plan(tip)
        plan_d = (offs,) + _tile_plan(tid)
        xs = _dispatch(plan_p, h2p, prp, jnp.zeros((N_SLOTS, D_MODEL), BF16) if xs is None else xs)
        xs = _dispatch(plan_d, h2d, prd, xs)
        ys = _experts(te, tile_src, xs, expert_w_gate, expert_w_up, expert_w_down, l)
        final = l == DEPTH - 1
        fw = final_norm_w[None, :]
        xp_next = _combine(plan_p, x1p, rfp, fw, ys, final)
        xd_next = _combine(plan_d, x1d, rfd, fw, ys, final)
        if final:
            yp, yd = xp_next, xd_next
        else:
            xp, xd = xp_next, xd_next

        heads_p = lambda a: a.reshape(BATCH, SEQ, N_HEADS, D)
        heads_t = lambda a: a.reshape(N_HEADS, D, DEC_BATCH).transpose(2, 0, 1)[:, None]
        outs["kp"].append(heads_p(sbk))
        outs["vp"].append(heads_p(sbv))
        outs["ks"].append(heads_t(pt[GROUP_W:2 * GROUP_W]))
        outs["vs"].append(heads_t(pt[2 * GROUP_W:3 * GROUP_W]))
        outs["cp"].append(c1p)
        outs["np"].append(n1p)
        outs["mp"].append(m1p[:, 0, :N_HEADS])
        outs["cs"].append(c1d.transpose(3, 0, 1, 2))
        outs["ns"].append(n1d.transpose(2, 0, 1))
        outs["ms"].append(m1d[:, 0, :].T)
        outs["gp"].append(s1p)
        outs["gcp"].append(cv1p)
        outs["gs"].append(s1d.transpose(3, 0, 1, 2))
        outs["gcs"].append(cv1d.transpose(1, 0, 2))
        outs["rp"].append(r1p)
        outs["rs"].append(r1d.transpose(3, 0, 1, 2))

    st = lambda k: jnp.stack(outs[k], axis=0)
    return (yp.reshape(BATCH, SEQ, D_MODEL), yd.reshape(DEC_BATCH, 1, D_MODEL),
            st("kp"), st("vp"), st("ks"), st("vs"),
            st("cp"), st("np"), st("mp"), st("cs"), st("ns"), st("ms"),
            st("gp"), st("gcp"), st("gs"), st("gcs"), st("rp"), st("rs"))
```
